```python
import jax, jax.numpy as jnp
from jax import lax
import numpy as np

D_MODEL = 1024
BATCH = 4
SEQ = 4096
DEPTH = 2
DEC_BATCH = 32
DEC_SEQ = 4
PAST_LEN = 16384
PAGE_SIZE = 128

HEAD_DIM = 64
GW = D_MODEL // 4
N_HEADS = GW // HEAD_DIM
SCALE = HEAD_DIM ** -0.5
MLSTM_CHUNK = 64
MLSTM_CONV = 4
M_INIT = -1e30
NSA_KV_HEADS = 2
NSA_GROUP = N_HEADS // NSA_KV_HEADS
KVW = NSA_KV_HEADS * HEAD_DIM
CMP_BLOCK = 32
CMP_STRIDE = 16
CMP_HIDDEN = HEAD_DIM
SLC_BLOCK = 64
N_SELECT = 16
WINDOW = 512
SLC_Q_BLOCK = 32
WIN_Q_BLOCK = 128
DECAY_LORA = 64
AAA_LORA = 64
RWKV_SHIFT = 3 * GW + DECAY_LORA + AAA_LORA
RWKV_LN_EPS = HEAD_DIM * 1e-5
N_MEM = 256
MEM_HEADS = 4
MEM_HEAD_DIM = GW // MEM_HEADS
ROPE_THETA = 10000.0
NORM_EPS = 1e-6
MLSTM_SIZES = (GW, GW, GW, N_HEADS, N_HEADS, GW, GW)
NSA_SIZES = (GW, KVW, KVW, KVW, KVW, KVW, KVW, 3 * N_HEADS, GW)
RWKV_SIZES = (RWKV_SHIFT, GW)
MEM_SIZES = (GW, GW)
GROUP_SIZES = (sum(MLSTM_SIZES), sum(NSA_SIZES), sum(RWKV_SIZES), sum(MEM_SIZES))
D_IN = sum(GROUP_SIZES)

kernel_name = 'hymba_mlstm_nsa_rwkv7_mem_step'


def _split(x, sizes):
    out, start = [], 0
    for s in sizes:
        out.append(x[..., start:start + s])
        start += s
    return out


def _stack_layers(states, i):
    return jnp.stack([s[i] for s in states])


def rms_norm(x, g, eps=NORM_EPS):
    xf = x.astype(jnp.float32)
    y = xf * lax.rsqrt(jnp.mean(xf * xf, -1, keepdims=True) + eps)
    return (y * g.astype(jnp.float32)).astype(x.dtype)


def rope(x, pos):
    half = HEAD_DIM // 2
    inv = ROPE_THETA ** (-jnp.arange(half, dtype=jnp.float32) / half)
    ang = pos.astype(jnp.float32)[:, None] * inv
    cos, sin = jnp.cos(ang)[:, None, :], jnp.sin(ang)[:, None, :]
    xf = x.astype(jnp.float32)
    x1, x2 = xf[..., :half], xf[..., half:]
    return jnp.concatenate([x1 * cos - x2 * sin, x1 * sin + x2 * cos], -1).astype(x.dtype)


def masked_softmax(s, mask):
    s = jnp.where(mask, s.astype(jnp.float32), -jnp.inf)
    m = jnp.max(s, -1, keepdims=True)
    e = jnp.exp(s - jnp.where(jnp.isfinite(m), m, 0.0))
    return e / jnp.maximum(jnp.sum(e, -1, keepdims=True), 1e-30)


def mlstm_chunkwise(q, k, v, log_i, log_f, C0, n0, m0):
    B, T, H, D = q.shape
    L = min(MLSTM_CHUNK, T)
    nc = T // L

    def chunks(a):
        return jnp.moveaxis(a.reshape((B, nc, L) + a.shape[2:]), (1, 3), (0, 2))

    causal = jnp.tril(jnp.ones((L, L), bool))

    def step(carry, xs):
        C, n, m = carry
        qc, kc, vc, lic, lfc = xs
        b = jnp.cumsum(lfc, -1)
        log_d = jnp.where(causal, b[..., :, None] - b[..., None, :] + lic[..., None, :], -jnp.inf)
        log_inter = b + m[..., None]
        m_t = jnp.maximum(jnp.max(log_d, -1), log_inter)
        s = jnp.einsum('bhtd,bhsd->bhts', qc, kc) * jnp.exp(log_d - m_t[..., None])
        w_inter = jnp.exp(log_inter - m_t)
        num = jnp.einsum('bhts,bhse->bhte', s, vc) + w_inter[..., None] * jnp.einsum('bhtd,bhde->bhte', qc, C)
        den = jnp.sum(s, -1) + w_inter * jnp.einsum('bhtd,bhd->bht', qc, n)
        h = num / jnp.maximum(jnp.abs(den), jnp.exp(-m_t))[..., None]
        b_end = b[..., -1]
        log_w = b_end[..., None] - b + lic
        m_new = jnp.maximum(b_end + m, jnp.max(log_w, -1))
        wk = jnp.exp(log_w - m_new[..., None])
        decay = jnp.exp(b_end + m - m_new)
        C = decay[..., None, None] * C + jnp.einsum('bhs,bhsd,bhse->bhde', wk, kc, vc)
        n = decay[..., None] * n + jnp.einsum('bhs,bhsd->bhd', wk, kc)
        return (C, n, m_new), h

    xs = tuple(chunks(a) for a in (q * D ** -0.5, k, v, log_i, log_f))
    (C, n, m), hs = lax.scan(step, (C0, n0, m0), xs)
    h = jnp.moveaxis(hs, (0, 2), (1, 3)).reshape(B, T, H, D)
    return h, C, n, m


def mlstm_mixer(p, conv_buf, C0, n0, m0, prm):
    B, T, _ = p.shape
    f32 = jnp.float32
    q, k, v, ig, fg, og, z = _split(p, MLSTM_SIZES)
    ext = jnp.concatenate([conv_buf.astype(p.dtype), jnp.concatenate([q, k], -1)], 1)
    w = prm['mlstm_conv_w']
    qk = jax.nn.silu(prm['mlstm_conv_b'] + sum(ext[:, j:j + T] * w[j] for j in range(MLSTM_CONV)))
    heads = lambda a: a.astype(f32).reshape(B, T, N_HEADS, HEAD_DIM)
    log_i = (ig + prm['mlstm_gate_b'][0]).astype(f32)
    log_f = jax.nn.log_sigmoid((fg + prm['mlstm_gate_b'][1]).astype(f32))
    h, C, n, m = mlstm_chunkwise(heads(qk[..., :GW]), heads(qk[..., GW:]), heads(v), log_i, log_f,
                                 C0.astype(f32), n0.astype(f32), m0.astype(f32))
    h = rms_norm(h, prm['mlstm_norm_g'].reshape(N_HEADS, HEAD_DIM)).reshape(B, T, GW).astype(p.dtype)
    out = jax.nn.sigmoid(og) * h * jax.nn.silu(z)
    return out, C, n, m, ext[:, -(MLSTM_CONV - 1):]


def gather_blocks(blocks, idx, blk0):
    B, Hkv = idx.shape[:2]
    bi = jnp.arange(B)[:, None, None, None, None]
    hi = jnp.arange(Hkv)[None, :, None, None, None]
    j = jnp.clip(idx - blk0, 0, blocks.shape[1] - 1)[..., None]
    return blocks[bi, j, jnp.arange(SLC_BLOCK), hi]


def gather_paged_blocks(cache, l, page_table, idx):
    bpp = PAGE_SIZE // SLC_BLOCK
    B, Hkv = idx.shape[:2]
    idx = jnp.clip(idx, 0, page_table.shape[1] * bpp - 1)
    phys = page_table[jnp.arange(B)[:, None, None, None], idx // bpp]
    tok = (idx % bpp)[..., None] * SLC_BLOCK + jnp.arange(SLC_BLOCK)
    hi = jnp.arange(Hkv)[None, :, None, None, None]
    return cache[l, phys[..., None], tok, hi]


def nsa_compressed(qn, rows, pos, prm):
    B, L = rows.shape[:2]
    R = CMP_BLOCK // CMP_STRIDE
    n_sub = max(-(-L // CMP_STRIDE), R)
    rows = jnp.pad(rows, ((0, 0), (0, n_sub * CMP_STRIDE - L), (0, 0), (0, 0), (0, 0)))
    sub = rows.reshape(B, n_sub, CMP_STRIDE, NSA_KV_HEADS, 2, HEAD_DIM)
    n_cmp = n_sub - R + 1
    w1 = prm['nsa_cmp_w1']
    hid = jnp.einsum('csd,csde->ce', prm['nsa_pe'], w1)[None, None, None]
    for r in range(R):
        hid = hid + jnp.einsum('bnshcd,csde->bnhce', sub[:, r:r + n_cmp],
                               w1[:, r * CMP_STRIDE:(r + 1) * CMP_STRIDE])
    kv = jnp.einsum('bnhce,ced->bnhcd', jax.nn.gelu(hid), prm['nsa_cmp_w2'])
    k_cmp = rms_norm(kv[..., 0, :], prm['nsa_qk_g'][1])
    qg = qn.reshape(B, -1, NSA_KV_HEADS, NSA_GROUP, HEAD_DIM)
    s = jnp.einsum('bthgd,bnhd->bhgtn', qg, k_cmp).astype(jnp.float32) * SCALE
    end = jnp.arange(n_cmp) * CMP_STRIDE + CMP_BLOCK - 1
    pr = masked_softmax(s, end[None, :] <= pos[:, None])
    o = jnp.einsum('bhgtn,bnhd->bthgd', pr, kv[..., 1, :].astype(jnp.float32))
    return o.reshape(qn.shape), pr


def nsa_select(p_cmp, pos, n_slc):
    imp = jnp.sum(p_cmp, 2)
    n_cmp = imp.shape[-1]
    start = jnp.arange(n_cmp)[:, None] * CMP_STRIDE
    blk = jnp.arange(n_slc)
    cover = ((start < (blk + 1) * SLC_BLOCK) & (start + CMP_BLOCK > blk * SLC_BLOCK)).astype(imp.dtype)
    imp = jnp.einsum('bhtn,ns->bhts', imp, cover)
    cur = (pos // SLC_BLOCK)[:, None]
    forced = (blk == 0) | (blk == cur) | (blk == cur - 1)
    imp = jnp.where(forced, jnp.inf, jnp.where(blk <= cur, imp, -jnp.inf))
    top, idx = lax.top_k(imp, min(N_SELECT, n_slc))
    return idx, top > -jnp.inf


def nsa_selected(q_rot, sel_idx, sel_ok, pos, read_blocks):
    B, T = q_rot.shape[:2]
    S = sel_idx.shape[-1]
    QB = min(SLC_Q_BLOCK, T)
    nb = T // QB
    qb = jnp.moveaxis(q_rot.reshape(B, nb, QB, NSA_KV_HEADS, NSA_GROUP, HEAD_DIM), 1, 0)
    ib = jnp.moveaxis(sel_idx.reshape(B, NSA_KV_HEADS, nb, QB, S), 2, 0)
    okb = jnp.moveaxis(sel_ok.reshape(B, NSA_KV_HEADS, nb, QB, S), 2, 0)
    pb = pos.reshape(nb, QB)

    def block(args):
        qc, ic, okc, pc = args
        kv = read_blocks(ic).reshape(B, NSA_KV_HEADS, QB, S * SLC_BLOCK, 2, HEAD_DIM)
        kpos = ic[..., None] * SLC_BLOCK + jnp.arange(SLC_BLOCK)
        mask = (okc[..., None] & (kpos <= pc[:, None, None])).reshape(B, NSA_KV_HEADS, 1, QB, S * SLC_BLOCK)
        s = jnp.einsum('bqhgd,bhqkd->bhgqk', qc, kv[..., 0, :]).astype(jnp.float32) * SCALE
        pr = masked_softmax(s, mask)
        return jnp.einsum('bhgqk,bhqkd->bqhgd', pr, kv[..., 1, :].astype(jnp.float32))

    o = lax.map(block, (qb, ib, okb, pb))
    return jnp.moveaxis(o, 0, 1).reshape(B, T, N_HEADS, HEAD_DIM)


def nsa_window(q_rot, win_ctx, pos0):
    B, T = q_rot.shape[:2]
    QB = min(WIN_Q_BLOCK, T)
    nb = T // QB
    rows = jnp.arange(nb)[:, None] * QB + jnp.arange(WINDOW + QB)
    kv = win_ctx[:, rows]
    qpos = pos0 + jnp.arange(T).reshape(nb, QB)
    kpos = pos0 - WINDOW + rows
    d = qpos[:, :, None] - kpos[:, None, :]
    mask = (d >= 0) & (d < WINDOW) & (kpos[:, None, :] >= 0)
    qg = q_rot.reshape(B, nb, QB, NSA_KV_HEADS, NSA_GROUP, HEAD_DIM)
    s = jnp.einsum('bcqhgd,bckhd->bhgcqk', qg, kv[..., 0, :]).astype(jnp.float32) * SCALE
    pr = masked_softmax(s, mask)
    o = jnp.einsum('bhgcqk,bckhd->bcqhgd', pr, kv[..., 1, :].astype(jnp.float32))
    return o.reshape(B, T, N_HEADS, HEAD_DIM)


def nsa_mixer(p, pos0, win_prefix, prm, past):
    B, T, _ = p.shape
    pos = pos0 + jnp.arange(T, dtype=jnp.int32)
    q, kc, vc, ks, vs, kw, vw, gates, z = _split(p, NSA_SIZES)
    g = prm['nsa_qk_g']
    kvh = lambda a: a.reshape(B, T, NSA_KV_HEADS, HEAD_DIM)
    qn = rms_norm(q.reshape(B, T, N_HEADS, HEAD_DIM), g[0])
    q_rot = rope(qn, pos)
    cmp_new = jnp.stack([kvh(kc), kvh(vc)], 3)
    slc_new = jnp.stack([rope(rms_norm(kvh(ks), g[2]), pos), kvh(vs)], 3)
    win_new = jnp.stack([rope(rms_norm(kvh(kw), g[3]), pos), kvh(vw)], 3)
    n_new_blk = -(-T // SLC_BLOCK)
    new_blocks = jnp.pad(slc_new, ((0, 0), (0, n_new_blk * SLC_BLOCK - T), (0, 0), (0, 0), (0, 0)))
    new_blocks = new_blocks.reshape(B, n_new_blk, SLC_BLOCK, NSA_KV_HEADS, 2, HEAD_DIM)
    blk0 = pos0 // SLC_BLOCK
    if past is None:
        cmp_rows = cmp_new
        read_blocks = lambda idx: gather_blocks(new_blocks, idx, blk0)
    else:
        cache_cmp, cache_slc, page_table, l = past
        past_rows = cache_cmp[l, page_table].reshape(B, pos0, NSA_KV_HEADS, 2, HEAD_DIM)
        cmp_rows = jnp.concatenate([past_rows, cmp_new], 1)

        def read_blocks(idx):
            old = gather_paged_blocks(cache_slc, l, page_table, idx)
            new = gather_blocks(new_blocks, idx, blk0)
            return jnp.where((idx < blk0)[..., None, None, None], old, new)
    o_cmp, p_cmp = nsa_compressed(qn, cmp_rows, pos, prm)
    sel_idx, sel_ok = nsa_select(p_cmp, pos, -(-(pos0 + T) // SLC_BLOCK))
    o_slc = nsa_selected(q_rot, sel_idx, sel_ok, pos, read_blocks)
    pad = WINDOW - win_prefix.shape[1]
    win_ctx = jnp.concatenate([jnp.pad(win_prefix.astype(win_new.dtype), ((0, 0), (pad, 0), (0, 0), (0, 0), (0, 0))),
                               win_new], 1)
    o_win = nsa_window(q_rot, win_ctx, pos0)
    gt = jax.nn.sigmoid(gates.astype(jnp.float32)).reshape(B, T, 3, N_HEADS, 1)
    o = gt[:, :, 0] * o_cmp + gt[:, :, 1] * o_slc + gt[:, :, 2] * o_win
    keep = win_prefix.shape[1] if past is not None else min(WINDOW, T)
    out = o.reshape(B, T, GW).astype(p.dtype) * jax.nn.silu(z)
    return out, cmp_new, slc_new, win_ctx[:, -keep:]


def rwkv7_scan(r, log_w, k, v, kk, a, S0):
    def step(S, xs):
        rt, wt, kt, vt, kkt, at = xs
        S = (S * jnp.exp(wt)[:, :, None, :]
             - jnp.einsum('bhvk,bhk->bhv', S, kkt)[..., None] * (kkt * at)[:, :, None, :]
             + vt[..., None] * kt[:, :, None, :])
        return S, jnp.einsum('bhvk,bhk->bhv', S, rt)
    S, ys = lax.scan(step, S0, tuple(jnp.moveaxis(t, 1, 0) for t in (r, log_w, k, v, kk, a)))
    return jnp.moveaxis(ys, 0, 1), S


def rwkv_mixer(p, shift_buf, S0, prm):
    B, T, _ = p.shape
    f32 = jnp.float32
    u, z = _split(p, RWKV_SIZES)
    prev = jnp.concatenate([shift_buf.astype(u.dtype), u[:, :-1]], 1)
    um = u + (prev - u) * prm['rwkv_mu']
    r, k, v, wl, al = _split(um, (GW, GW, GW, DECAY_LORA, AAA_LORA))
    heads = lambda t: t.astype(f32).reshape(B, T, N_HEADS, HEAD_DIM)
    w_pre = (prm['rwkv_w0'] + jnp.tanh(wl) @ prm['rwkv_w2']).astype(f32)
    log_w = -jnp.exp(-jax.nn.softplus(-w_pre) - 0.5)
    a = jax.nn.sigmoid((prm['rwkv_a0'] + al @ prm['rwkv_a2']).astype(f32))
    k_k, k_a, r_k = prm['rwkv_kk'][0], prm['rwkv_kk'][1], prm['rwkv_kk'][2]
    kk = heads(k.astype(f32) * k_k)
    kk = kk / jnp.maximum(jnp.sqrt(jnp.sum(kk * kk, -1, keepdims=True)), 1e-12)
    k_eff = heads(k.astype(f32) * (1.0 + (a - 1.0) * k_a))
    rh, vh = heads(r), heads(v)
    y, S = rwkv7_scan(rh, heads(log_w), k_eff, vh, kk, heads(a), S0.astype(f32))
    mu = jnp.mean(y, -1, keepdims=True)
    var = jnp.mean(jnp.square(y - mu), -1, keepdims=True)
    ln = prm['rwkv_ln'].reshape(2, N_HEADS, HEAD_DIM)
    y = (y - mu) * lax.rsqrt(var + RWKV_LN_EPS) * ln[0] + ln[1]
    y = y + jnp.sum(rh * k_eff * r_k.reshape(N_HEADS, HEAD_DIM), -1, keepdims=True) * vh
    out = y.reshape(B, T, GW).astype(p.dtype) * jax.nn.silu(z)
    return out, S, u[:, -1:]


def memory_kv(mem, prm):
    B = mem.shape[0]
    h = rms_norm(mem, prm['mem_norm_g'])
    kv = (h @ prm['w_mem_kv']).reshape(B, -1, 2, MEM_HEADS, MEM_HEAD_DIM)
    return jnp.stack([rms_norm(kv[:, :, 0], prm['mem_qk_g'][1]), kv[:, :, 1]], 3)


def memory_mixer(p, mem_kv, prm):
    B, T, _ = p.shape
    q, z = _split(p, MEM_SIZES)
    qn = rms_norm(q.reshape(B, T, MEM_HEADS, MEM_HEAD_DIM), prm['mem_qk_g'][0])
    s = jnp.einsum('bthd,bmhd->bhtm', qn, mem_kv[..., 0, :]).astype(jnp.float32) * MEM_HEAD_DIM ** -0.5
    pr = jax.nn.softmax(s, -1)
    o = jnp.einsum('bhtm,bmhd->bthd', pr, mem_kv[..., 1, :].astype(jnp.float32))
    return o.reshape(B, T, GW).astype(p.dtype) * jax.nn.silu(z)


def layer(x, pos0, st, mem_kv, prm, past):
    conv_buf, C0, n0, m0, S0, shift_buf, win_prefix = st
    h = rms_norm(x, prm['norm_g'])
    p_m, p_n, p_r, p_c = _split(h @ prm['w_in'], GROUP_SIZES)
    y_m, C, n, m, conv_new = mlstm_mixer(p_m, conv_buf, C0, n0, m0, prm)
    y_n, cmp_new, slc_new, win_new = nsa_mixer(p_n, pos0, win_prefix, prm, past)
    y_r, S, shift_new = rwkv_mixer(p_r, shift_buf, S0, prm)
    y_c = memory_mixer(p_c, mem_kv, prm)
    y = jnp.concatenate([y_m, y_n, y_r, y_c], -1) @ prm['w_out']
    return x + y, (cmp_new, slc_new, win_new, C, n, m, conv_new, S, shift_new)


def setup_inputs(seed: int = 0) -> dict:
    key = jax.random.key(seed)
    keys = iter(jax.random.split(key, 48))
    f32 = jnp.float32

    def nrm(shape, scale=1.0):
        return scale * jax.random.normal(next(keys), shape, f32)

    def uni(shape, lo, hi):
        return jax.random.uniform(next(keys), shape, f32, lo, hi)

    n_pages = PAST_LEN // PAGE_SIZE
    n_pool = (5 * DEC_BATCH * n_pages) // 4
    win_buf = min(WINDOW, PAST_LEN)
    perm = jax.random.permutation(next(keys), n_pool).astype(jnp.int32)
    return {
        'x_prompt': nrm((BATCH, SEQ, D_MODEL)),
        'x_sample': nrm((DEC_BATCH, DEC_SEQ, D_MODEL)),
        'cache_cmp_kv': nrm((DEPTH, n_pool, PAGE_SIZE, NSA_KV_HEADS, 2, HEAD_DIM)),
        'cache_slc_kv': nrm((DEPTH, n_pool, PAGE_SIZE, NSA_KV_HEADS, 2, HEAD_DIM)),
        'cache_win_kv': nrm((DEPTH, DEC_BATCH, win_buf, NSA_KV_HEADS, 2, HEAD_DIM)),
        'cache_mem_kv': nrm((DEPTH, DEC_BATCH, N_MEM, MEM_HEADS, 2, MEM_HEAD_DIM)),
        'state_mlstm_C': nrm((DEPTH, DEC_BATCH, N_HEADS, HEAD_DIM, HEAD_DIM), 0.1),
        'state_mlstm_n': jnp.abs(nrm((DEPTH, DEC_BATCH, N_HEADS, HEAD_DIM), 0.5)),
        'state_mlstm_m': nrm((DEPTH, DEC_BATCH, N_HEADS), 0.5),
        'state_mlstm_conv': nrm((DEPTH, DEC_BATCH, MLSTM_CONV - 1, 2 * GW)),
        'state_rwkv_S': nrm((DEPTH, DEC_BATCH, N_HEADS, HEAD_DIM, HEAD_DIM), 0.1),
        'state_rwkv_shift': nrm((DEPTH, DEC_BATCH, 1, RWKV_SHIFT)),
        'page_table': perm[:DEC_BATCH * n_pages].reshape(DEC_BATCH, n_pages),
        'mem_prompt': nrm((BATCH, N_MEM, D_MODEL)),
        'norm_g': 1.0 + nrm((DEPTH, D_MODEL), 0.02),
        'w_in': nrm((DEPTH, D_MODEL, D_IN), D_MODEL ** -0.5),
        'w_out': nrm((DEPTH, D_MODEL, D_MODEL), 0.5 * D_MODEL ** -0.5),
        'mlstm_conv_w': nrm((DEPTH, MLSTM_CONV, 2 * GW), MLSTM_CONV ** -0.5),
        'mlstm_conv_b': nrm((DEPTH, 2 * GW), 0.02),
        'mlstm_gate_b': jnp.concatenate([nrm((DEPTH, 1, N_HEADS), 0.1), uni((DEPTH, 1, N_HEADS), 3.0, 6.0)], 1),
        'mlstm_norm_g': 1.0 + nrm((DEPTH, GW), 0.02),
        'nsa_qk_g': 1.0 + nrm((DEPTH, 4, HEAD_DIM), 0.02),
        'nsa_pe': nrm((DEPTH, 2, CMP_BLOCK, HEAD_DIM), 0.1),
        'nsa_cmp_w1': nrm((DEPTH, 2, CMP_BLOCK, HEAD_DIM, CMP_HIDDEN), (CMP_BLOCK * HEAD_DIM) ** -0.5),
        'nsa_cmp_w2': nrm((DEPTH, 2, CMP_HIDDEN, HEAD_DIM), CMP_HIDDEN ** -0.5),
        'rwkv_mu': uni((DEPTH, RWKV_SHIFT), 0.0, 1.0),
        'rwkv_w0': uni((DEPTH, GW), -5.0, -0.5),
        'rwkv_w2': nrm((DEPTH, DECAY_LORA, GW), 0.5 * DECAY_LORA ** -0.5),
        'rwkv_a0': nrm((DEPTH, GW), 0.1),
        'rwkv_a2': nrm((DEPTH, AAA_LORA, GW), 0.5 * AAA_LORA ** -0.5),
        'rwkv_kk': jnp.concatenate([0.85 + nrm((DEPTH, 1, GW), 0.05), 1.0 + nrm((DEPTH, 1, GW), 0.05),
                                    nrm((DEPTH, 1, GW), 0.1)], 1),
        'rwkv_ln': jnp.concatenate([1.0 + nrm((DEPTH, 1, GW), 0.02), nrm((DEPTH, 1, GW), 0.02)], 1),
        'mem_norm_g': 1.0 + nrm((DEPTH, D_MODEL), 0.02),
        'w_mem_kv': nrm((DEPTH, D_MODEL, 2 * GW), D_MODEL ** -0.5),
        'mem_qk_g': 1.0 + nrm((DEPTH, 2, MEM_HEAD_DIM), 0.02),
    }


def reference(x_prompt, x_sample, cache_cmp_kv, cache_slc_kv, cache_win_kv, cache_mem_kv,
              state_mlstm_C, state_mlstm_n, state_mlstm_m, state_mlstm_conv, state_rwkv_S, state_rwkv_shift,
              page_table, mem_prompt,
              norm_g, w_in, w_out, mlstm_conv_w, mlstm_conv_b, mlstm_gate_b, mlstm_norm_g,
              nsa_qk_g, nsa_pe, nsa_cmp_w1, nsa_cmp_w2,
              rwkv_mu, rwkv_w0, rwkv_w2, rwkv_a0, rwkv_a2, rwkv_kk, rwkv_ln,
              mem_norm_g, w_mem_kv, mem_qk_g):
    f32 = jnp.float32
    B = x_prompt.shape[0]
    past_len = page_table.shape[1] * PAGE_SIZE
    xp, xs = x_prompt, x_sample
    new_p, new_s, new_mem = [], [], []
    for l in range(DEPTH):
        prm = {'norm_g': norm_g[l], 'w_in': w_in[l], 'w_out': w_out[l],
               'mlstm_conv_w': mlstm_conv_w[l], 'mlstm_conv_b': mlstm_conv_b[l],
               'mlstm_gate_b': mlstm_gate_b[l], 'mlstm_norm_g': mlstm_norm_g[l],
               'nsa_qk_g': nsa_qk_g[l], 'nsa_pe': nsa_pe[l], 'nsa_cmp_w1': nsa_cmp_w1[l],
               'nsa_cmp_w2': nsa_cmp_w2[l], 'rwkv_mu': rwkv_mu[l], 'rwkv_w0': rwkv_w0[l],
               'rwkv_w2': rwkv_w2[l], 'rwkv_a0': rwkv_a0[l], 'rwkv_a2': rwkv_a2[l],
               'rwkv_kk': rwkv_kk[l], 'rwkv_ln': rwkv_ln[l], 'mem_norm_g': mem_norm_g[l],
               'w_mem_kv': w_mem_kv[l], 'mem_qk_g': mem_qk_g[l]}
        mem_kv_p = memory_kv(mem_prompt, prm)
        st_p = (jnp.zeros((B, MLSTM_CONV - 1, 2 * GW), xp.dtype),
                jnp.zeros((B, N_HEADS, HEAD_DIM, HEAD_DIM), f32),
                jnp.zeros((B, N_HEADS, HEAD_DIM), f32),
                jnp.full((B, N_HEADS), M_INIT, f32),
                jnp.zeros((B, N_HEADS, HEAD_DIM, HEAD_DIM), f32),
                jnp.zeros((B, 1, RWKV_SHIFT), xp.dtype),
                jnp.zeros((B, 0, NSA_KV_HEADS, 2, HEAD_DIM), xp.dtype))
        xp, sp = layer(xp, 0, st_p, mem_kv_p, prm, None)
        st_s = (state_mlstm_conv[l], state_mlstm_C[l], state_mlstm_n[l], state_mlstm_m[l],
                state_rwkv_S[l], state_rwkv_shift[l], cache_win_kv[l])
        xs, ss = layer(xs, past_len, st_s, cache_mem_kv[l], prm, (cache_cmp_kv, cache_slc_kv, page_table, l))
        new_p.append(sp)
        new_s.append(ss)
        new_mem.append(mem_kv_p)
    return (xp, xs,
            _stack_layers(new_p, 0), _stack_layers(new_s, 0),
            _stack_layers(new_p, 1), _stack_layers(new_s, 1),
            _stack_layers(new_p, 2), _stack_layers(new_s, 2),
            jnp.stack(new_mem),
            _stack_layers(new_p, 3), _stack_layers(new_s, 3),
            _stack_layers(new_p, 4), _stack_layers(new_s, 4),
            _stack_layers(new_p, 5), _stack_layers(new_s, 5),
            _stack_layers(new_p, 6), _stack_layers(new_s, 6),
            _stack_layers(new_p, 7), _stack_layers(new_s, 7),
            _stack_layers(new_p, 8), _stack_layers(new_s, 8))
```

```python
import functools

import numpy as np
import jax
import jax.numpy as jnp
from jax import lax
from jax.experimental import pallas as pl
from jax.experimental.pallas import tpu as pltpu

F32 = jnp.float32
BF16 = jnp.bfloat16
HI = lax.Precision.HIGHEST

D_MODEL = 1024
PAGE_SIZE = 128
HEAD_DIM = 64
GW = D_MODEL // 4
N_HEADS = GW // HEAD_DIM
SCALE = HEAD_DIM ** -0.5
MLSTM_CONV = 4
M_INIT = -1e30
NSA_KV_HEADS = 2
KVW = NSA_KV_HEADS * HEAD_DIM
CMP_BLOCK = 32
CMP_STRIDE = 16
SLC_BLOCK = 64
N_SELECT = 16
WINDOW = 512
DECAY_LORA = 64
AAA_LORA = 64
RWKV_SHIFT = 3 * GW + DECAY_LORA + AAA_LORA
RWKV_LN_EPS = HEAD_DIM * 1e-5
N_MEM = 256
MEM_HEADS = 4
ROPE_THETA = 10000.0
NORM_EPS = 1e-6

LANES = 128
MLSTM_L = 128
RWKV_L = 64
NEG_BIG = -1e30

_M0, _N0, _R0, _C0 = 0, 1288, 2580, 3732
C_MQ, C_MK, C_MV, C_MO, C_MZ = 0, 256, 512, 768, 1024
C_NQ, C_NCMP, C_NSLC, C_NWIN, C_NZ = 1280, 1536, 1792, 2048, 2304
C_RR, C_RK, C_RV, C_RZ = 2560, 2816, 3072, 3328
C_CQ, C_CZ = 3584, 3840
C_RL = 4096
C_MG = 4224
C_NG = 4352
DP = 4480


def _packed_src():
    src = -np.ones((DP,), np.int64)

    def put(dst, lo, n):
        src[dst:dst + n] = np.arange(lo, lo + n)

    put(C_MQ, _M0, 256); put(C_MK, _M0 + 256, 256); put(C_MV, _M0 + 512, 256)
    put(C_MG, _M0 + 768, 8); put(C_MO, _M0 + 776, 256); put(C_MZ, _M0 + 1032, 256)
    put(C_NQ, _N0, 256)
    for i, base in enumerate((C_NCMP, C_NSLC, C_NWIN)):
        ksrc = _N0 + 256 + 256 * i
        vsrc = ksrc + 128
        for h in range(2):
            put(base + 128 * h, ksrc + 64 * h, 64)
            put(base + 128 * h + 64, vsrc + 64 * h, 64)
    put(C_NG, _N0 + 1024, 12); put(C_NZ, _N0 + 1036, 256)
    put(C_RR, _R0, 256); put(C_RK, _R0 + 256, 256); put(C_RV, _R0 + 512, 256)
    put(C_RL, _R0 + 768, 128); put(C_RZ, _R0 + 896, 256)
    put(C_CQ, _C0, 256); put(C_CZ, _C0 + 256, 256)
    return src


_SRC = _packed_src()


def _mm(a, b, prec=None):
    return jnp.dot(a, b, preferred_element_type=F32, precision=prec)


def _mm_nt(a, b, prec=None):
    return lax.dot_general(a, b, (((1,), (1,)), ((), ())), preferred_element_type=F32, precision=prec)


def _mm_tn(a, b, prec=None):
    return lax.dot_general(a, b, (((0,), (0,)), ((), ())), preferred_element_type=F32, precision=prec)


def _cparams(sem, vmem_mb=None):
    kw = dict(dimension_semantics=sem)
    if vmem_mb is not None:
        kw['vmem_limit_bytes'] = vmem_mb * 1024 * 1024
    return pltpu.CompilerParams(**kw)


def _proj_in_body(x_ref, g_ref, w_ref, o_ref):
    x = x_ref[...]
    h = x * lax.rsqrt(jnp.mean(x * x, -1, keepdims=True) + NORM_EPS) * g_ref[...]
    o_ref[...] = jnp.dot(h.astype(BF16), w_ref[...], preferred_element_type=F32)


def _proj_in(x2d, g, w_bf16, tn):
    n, d = x2d.shape
    dn = w_bf16.shape[1]
    tm = min(1024, n)
    return pl.pallas_call(
        _proj_in_body, out_shape=jax.ShapeDtypeStruct((n, dn), F32), grid=(n // tm, dn // tn),
        in_specs=[pl.BlockSpec((tm, d), lambda i, j: (i, 0)),
                  pl.BlockSpec((1, d), lambda i, j: (0, 0)),
                  pl.BlockSpec((d, tn), lambda i, j: (0, j))],
        out_specs=pl.BlockSpec((tm, tn), lambda i, j: (i, j)),
        compiler_params=_cparams(("parallel", "arbitrary"), 48), name="proj_in")(x2d, g.reshape(1, d), w_bf16)


def _proj_out_body(x_ref, y0, y1, y2, y3, w_ref, o_ref):
    acc = x_ref[...]
    for i, y in enumerate((y0, y1, y2, y3)):
        acc = acc + jnp.dot(y[...].astype(BF16), w_ref[i * GW:(i + 1) * GW, :], preferred_element_type=F32)
    o_ref[...] = acc


def _proj_out(x2d, ys, w_bf16):
    n, d = x2d.shape
    tm = min(512, n)
    yspec = pl.BlockSpec((tm, GW), lambda i: (i, 0))
    return pl.pallas_call(
        _proj_out_body, out_shape=jax.ShapeDtypeStruct((n, d), F32), grid=(n // tm,),
        in_specs=[pl.BlockSpec((tm, d), lambda i: (i, 0)), yspec, yspec, yspec, yspec,
                  pl.BlockSpec((d, d), lambda i: (0, 0))],
        out_specs=pl.BlockSpec((tm, d), lambda i: (i, 0)),
        compiler_params=_cparams(("parallel",), 48), name="proj_out")(x2d, *ys, w_bf16)


def _mlstm_body(q_ref, k_ref, v_ref, g_ref, c0_ref, n0_ref, m0_ref, h_ref, c_ref, n_ref, m_ref,
                c_scr, n_scr, m_scr):
    c = pl.program_id(1)

    @pl.when(c == 0)
    def _():
        c_scr[...] = c0_ref[0]
        n_scr[...] = n0_ref[0]
        m_scr[...] = m0_ref[0]

    L = q_ref.shape[1]
    g = g_ref[0]
    row = lax.broadcasted_iota(jnp.int32, (L, L), 0)
    col = lax.broadcasted_iota(jnp.int32, (L, L), 1)
    causal = row >= col
    bc = _mm(causal.astype(F32), g, HI)
    gt = g.T
    br = _mm(gt, (row <= col).astype(F32), HI)
    q = q_ref[0]
    k = k_ref[0]
    v = v_ref[0]
    outs = []
    for h in range(N_HEADS):
        sl = slice(h * HEAD_DIM, (h + 1) * HEAD_DIM)
        qh = q[:, sl] * (HEAD_DIM ** -0.5)
        kh = k[:, sl]
        vh = v[:, sl]
        b_col = bc[:, 4 + h:5 + h]
        li_col = g[:, h:h + 1]
        b_row = br[4 + h:5 + h, :]
        li_row = gt[h:h + 1, :]
        m_prev = m_scr[:, h:h + 1]
        log_d = jnp.where(causal, b_col - b_row + li_row, -jnp.inf)
        log_inter = b_col + m_prev
        m_t = jnp.maximum(jnp.max(log_d, -1, keepdims=True), log_inter)
        s = _mm_nt(qh, kh) * jnp.exp(log_d - m_t)
        w_inter = jnp.exp(log_inter - m_t)
        ch = c_scr[h]
        nh = n_scr[h:h + 1, :]
        num = _mm(s, vh) + w_inter * _mm(qh, ch)
        den = jnp.sum(s, -1, keepdims=True) + w_inter * jnp.sum(qh * nh, -1, keepdims=True)
        outs.append(num / jnp.maximum(jnp.abs(den), jnp.exp(-m_t)))
        b_end = b_col[L - 1:L, :]
        log_w = b_end - b_col + li_col
        m_new = jnp.maximum(b_end + m_prev, jnp.max(log_w, 0, keepdims=True))
        wk = jnp.exp(log_w - m_new)
        decay = jnp.exp(b_end + m_prev - m_new)
        kw = kh * wk
        c_scr[h] = decay * ch + _mm_tn(kw, vh)
        n_scr[h:h + 1, :] = decay * nh + jnp.sum(kw, 0, keepdims=True)
        m_scr[:, h:h + 1] = m_new
    h_ref[0] = jnp.concatenate(outs, axis=1)

    @pl.when(c == pl.num_programs(1) - 1)
    def _():
        c_ref[0] = c_scr[...]
        n_ref[0] = n_scr[...]
        m_ref[0] = m_scr[...]


def _mlstm(q, k, v, gates, c0, n0, m0):
    B, T, _ = q.shape
    L = MLSTM_L
    tok = pl.BlockSpec((1, L, GW), lambda b, c: (b, c, 0))
    sc = pl.BlockSpec((1, N_HEADS, HEAD_DIM, HEAD_DIM), lambda b, c: (b, 0, 0, 0))
    sn = pl.BlockSpec((1, N_HEADS, HEAD_DIM), lambda b, c: (b, 0, 0))
    sm = pl.BlockSpec((1, 1, LANES), lambda b, c: (b, 0, 0))
    return pl.pallas_call(
        _mlstm_body,
        out_shape=(jax.ShapeDtypeStruct((B, T, GW), F32), jax.ShapeDtypeStruct(c0.shape, F32),
                   jax.ShapeDtypeStruct(n0.shape, F32), jax.ShapeDtypeStruct(m0.shape, F32)),
        grid=(B, T // L),
        in_specs=[tok, tok, tok, pl.BlockSpec((1, L, LANES), lambda b, c: (b, c, 0)), sc, sn, sm],
        out_specs=(tok, sc, sn, sm),
        scratch_shapes=[pltpu.VMEM((N_HEADS, HEAD_DIM, HEAD_DIM), F32), pltpu.VMEM((N_HEADS, HEAD_DIM), F32),
                        pltpu.VMEM((1, LANES), F32)],
        compiler_params=_cparams(("parallel", "arbitrary")), name="mlstm")(q, k, v, gates, c0, n0, m0)


def _rwkv_body(r_ref, w_ref, k_ref, v_ref, kk_ref, a_ref, s0_ref, y_ref, s_ref, s_scr, *, prec):
    c_id = pl.program_id(1)

    @pl.when(c_id == 0)
    def _():
        s_scr[...] = s0_ref[0]

    L = r_ref.shape[1]
    D = HEAD_DIM
    row = lax.broadcasted_iota(jnp.int32, (L, L), 0)
    col = lax.broadcasted_iota(jnp.int32, (L, L), 1)
    lower = row >= col
    strict = row > col
    tril = lower.astype(F32)
    r_all, w_all, k_all, v_all, kk_all, a_all = (x[0] for x in (r_ref, w_ref, k_ref, v_ref, kk_ref, a_ref))
    n_sq = int(np.log2(L)) - 1
    outs = []
    for h in range(N_HEADS):
        sl = slice(h * D, (h + 1) * D)
        r, w, k, v, kk, a = (x[:, sl] for x in (r_all, w_all, k_all, v_all, kk_all, a_all))
        cum = _mm(tril, w, HI)
        c_last = cum[L - 1:L, :]
        e_neg = jnp.exp(-cum)
        kh = kk * jnp.exp(cum - w)
        bt = kk * a * e_neg
        kt = k * e_neg
        rh = r * jnp.exp(cum)
        gram = _mm_nt(jnp.concatenate([kh, rh], 0), jnp.concatenate([bt, kt], 0), prec)
        A = jnp.where(strict, gram[:L, :L], 0.0)
        Bm = jnp.where(strict, gram[:L, L:], 0.0)
        Mb = jnp.where(lower, gram[L:, :L], 0.0)
        Mk = jnp.where(lower, gram[L:, L:], 0.0)
        X = jnp.concatenate([kh, _mm(Bm, v, prec)], 1)
        Pw = A
        X = X - _mm(Pw, X, prec)
        for _ in range(n_sq):
            Pw = _mm(Pw, Pw, prec)
            X = X + _mm(Pw, X, prec)
        W = X[:, :D]
        U0 = X[:, D:]
        e_end = jnp.exp(c_last - cum)
        bp = kk * a * e_end
        kp = k * e_end
        xtb = _mm_tn(X, bp, prec)
        wtb = xtb[:D]
        N = _mm_tn(v, kp, prec) - xtb[D:]
        mbx = _mm(Mb, X, prec)
        qp = rh - mbx[:, :D]
        y0 = _mm(Mk, v, prec) - mbx[:, D:]
        s0 = s_scr[h]
        outs.append(_mm_nt(qp, s0, prec) + y0)
        s_scr[h] = s0 * jnp.exp(c_last) - _mm(s0, wtb, prec) + N
    y_ref[0] = jnp.concatenate(outs, axis=1)

    @pl.when(c_id == pl.num_programs(1) - 1)
    def _():
        s_ref[0] = s_scr[...]


def _rwkv(r, w, k, v, kk, a, s0, prec=HI):
    B, T, _ = r.shape
    L = RWKV_L
    tok = pl.BlockSpec((1, L, GW), lambda b, c: (b, c, 0))
    st = pl.BlockSpec((1, N_HEADS, HEAD_DIM, HEAD_DIM), lambda b, c: (b, 0, 0, 0))
    return pl.pallas_call(
        functools.partial(_rwkv_body, prec=prec),
        out_shape=(jax.ShapeDtypeStruct((B, T, GW), F32), jax.ShapeDtypeStruct(s0.shape, F32)),
        grid=(B, T // L), in_specs=[tok] * 6 + [st], out_specs=(tok, st),
        scratch_shapes=[pltpu.VMEM((N_HEADS, HEAD_DIM, HEAD_DIM), F32)],
        compiler_params=_cparams(("parallel", "arbitrary")), name="rwkv")(r, w, k, v, kk, a, s0)


SUB_FEAT = CMP_STRIDE * 2 * KVW


def _subproj_body(z_ref, w_ref, o_ref):
    o_ref[...] = jnp.dot(z_ref[...].astype(BF16), w_ref[...], preferred_element_type=F32)


def _subproj(z, w):
    m = z.shape[0]
    tm = min(256, m)
    return pl.pallas_call(
        _subproj_body, out_shape=jax.ShapeDtypeStruct((m, 512), F32), grid=(m // tm,),
        in_specs=[pl.BlockSpec((tm, SUB_FEAT), lambda i: (i, 0)), pl.BlockSpec((SUB_FEAT, 512), lambda i: (0, 0))],
        out_specs=pl.BlockSpec((tm, 512), lambda i: (i, 0)),
        compiler_params=_cparams(("parallel",), 48), name="cmp_subproj")(z, w)


_PAGES_PER_STEP = 32


def _subproj_pages_body(pt_ref, *refs):
    del pt_ref
    npg = len(refs) - 2
    w_ref, o_ref = refs[npg], refs[npg + 1]
    z = jnp.concatenate([refs[p][0] for p in range(npg)], 0)
    o_ref[0] = jnp.dot(z.astype(BF16), w_ref[...], preferred_element_type=F32)


def _subproj_pages(cache_sub, page_table, layer_off, w):
    B, n_pages = page_table.shape
    npg = min(_PAGES_PER_STEP, n_pages)
    spp = PAGE_SIZE // CMP_STRIDE

    def page_spec(p):
        return pl.BlockSpec((1, spp, SUB_FEAT), lambda b, g, pt: (layer_off + pt[b, g * npg + p], 0, 0))

    gs = pltpu.PrefetchScalarGridSpec(
        num_scalar_prefetch=1, grid=(B, n_pages // npg),
        in_specs=[page_spec(p) for p in range(npg)] + [pl.BlockSpec((SUB_FEAT, 512), lambda b, g, pt: (0, 0))],
        out_specs=pl.BlockSpec((1, npg * spp, 512), lambda b, g, pt: (b, g, 0)))
    return pl.pallas_call(
        _subproj_pages_body, out_shape=jax.ShapeDtypeStruct((B, n_pages * spp, 512), F32), grid_spec=gs,
        compiler_params=_cparams(("parallel", "arbitrary"), 48), name="cmp_subproj_pages")(
            page_table, *([cache_sub] * npg), w)


def _cmp_mlp_body(h_ref, w2_ref, g_ref, o_ref):
    x = h_ref[0]
    kv = _mm(jax.nn.gelu(x), w2_ref[...])
    g = g_ref[...]
    segs = []
    for j in range(4):
        seg = kv[:, j * HEAD_DIM:(j + 1) * HEAD_DIM]
        if j % 2 == 0:
            seg = seg * lax.rsqrt(jnp.mean(seg * seg, -1, keepdims=True) + NORM_EPS) * g
        segs.append(seg)
    o_ref[0] = jnp.concatenate(segs, 1)


def _cmp_mlp(hid, w2bd, g):
    B, n, _ = hid.shape
    tn = min(512, n)
    return pl.pallas_call(
        _cmp_mlp_body, out_shape=jax.ShapeDtypeStruct((B, n, 256), F32), grid=(B, n // tn),
        in_specs=[pl.BlockSpec((1, tn, 256), lambda b, i: (b, i, 0)), pl.BlockSpec((256, 256), lambda b, i: (0, 0)),
                  pl.BlockSpec((1, HEAD_DIM), lambda b, i: (0, 0))],
        out_specs=pl.BlockSpec((1, tn, 256), lambda b, i: (b, i, 0)),
        compiler_params=_cparams(("parallel", "parallel")), name="cmp_mlp")(hid, w2bd, g.reshape(1, HEAD_DIM))


def _cmp_attn_body(q_ref, kv_ref, cov_ref, o_ref, imp_ref, *, pos0):
    qi = pl.program_id(2)
    tq = q_ref.shape[1]
    n = kv_ref.shape[1]
    kv = kv_ref[0]
    k = kv[:, :HEAD_DIM]
    v = kv[:, HEAD_DIM:]
    q = q_ref[0]
    pos = pos0 + qi * tq + lax.broadcasted_iota(jnp.int32, (tq, 1), 0)
    end = lax.broadcasted_iota(jnp.int32, (1, n), 1) * CMP_STRIDE + (CMP_BLOCK - 1)
    mask = end <= pos
    psum = jnp.zeros((tq, n), F32)
    outs = []
    for g in range(2):
        s = _mm_nt(q[:, g * HEAD_DIM:(g + 1) * HEAD_DIM], k) * SCALE
        s = jnp.where(mask, s, -jnp.inf)
        m = jnp.max(s, -1, keepdims=True)
        e = jnp.exp(s - jnp.where(m == -jnp.inf, 0.0, m))
        p = e / jnp.maximum(jnp.sum(e, -1, keepdims=True), 1e-30)
        outs.append(_mm(p, v))
        psum = psum + p
    o_ref[0] = jnp.concatenate(outs, 1)
    hi = psum.astype(BF16)
    lo = (psum - hi.astype(F32)).astype(BF16)
    cov = cov_ref[...]
    imp_ref[0, 0] = _mm(hi, cov) + _mm(lo, cov)


def _cmp_attn(qn, kv_cmp, cover, pos0, tq):
    B, T, _ = qn.shape
    n = kv_cmp.shape[1]
    nbp = cover.shape[1]
    return pl.pallas_call(
        functools.partial(_cmp_attn_body, pos0=pos0),
        out_shape=(jax.ShapeDtypeStruct((B, T, GW), F32), jax.ShapeDtypeStruct((B, NSA_KV_HEADS, T, nbp), F32)),
        grid=(B, NSA_KV_HEADS, T // tq),
        in_specs=[pl.BlockSpec((1, tq, LANES), lambda b, h, i: (b, i, h)),
                  pl.BlockSpec((1, n, LANES), lambda b, h, i: (b, 0, h)),
                  pl.BlockSpec((n, nbp), lambda b, h, i: (0, 0))],
        out_specs=(pl.BlockSpec((1, tq, LANES), lambda b, h, i: (b, i, h)),
                   pl.BlockSpec((1, 1, tq, nbp), lambda b, h, i: (b, h, i, 0))),
        compiler_params=_cparams(("parallel", "parallel", "arbitrary")), name="cmp_attn")(qn, kv_cmp, cover)


def _topk_body(imp_ref, sel_ref, idx_ref, *, pos0, t_rows, n_blk):
    ti = pl.program_id(0)
    x = imp_ref[...]
    nbp = x.shape[1]
    xt = jnp.concatenate([x[:, j * LANES:(j + 1) * LANES].T for j in range(nbp // LANES)], 0)
    r = ti * LANES + lax.broadcasted_iota(jnp.int32, (1, LANES), 1)
    cur = (pos0 + r % t_rows) // SLC_BLOCK
    blk = lax.broadcasted_iota(jnp.int32, (nbp, 1), 0)
    forced = (blk == 0) | (blk == cur) | (blk == cur - 1)
    val = jnp.where(forced, jnp.inf, jnp.where(blk <= cur, xt, -jnp.inf))
    val = jnp.where(blk < n_blk, val, -jnp.inf)
    cnt = jnp.zeros((nbp, LANES), F32)
    for i in range(n_blk):
        vi = val[i:i + 1, :]
        ahead = (vi > val) | ((vi == val) & (blk > i))
        cnt = cnt + jnp.where(ahead, 1.0, 0.0)
    chosen = (cnt < float(N_SELECT)) & (val > -jnp.inf)
    self32 = jnp.where(chosen, 1.0, 0.0)
    sel_ref[...] = jnp.concatenate([self32[j * LANES:(j + 1) * LANES, :].T for j in range(nbp // LANES)], 1)
    blk_f = blk.astype(F32)
    rows = []
    for j in range(N_SELECT):
        hit = chosen & (cnt == float(j))
        rows.append(jnp.sum(jnp.where(hit, blk_f + 1.0, 0.0), 0, keepdims=True) - 1.0)
    idx_ref[...] = jnp.concatenate(rows, 0).astype(jnp.int32)


def _topk(imp2d, pos0, t_rows, n_blk):
    R, nbp = imp2d.shape
    return pl.pallas_call(
        functools.partial(_topk_body, pos0=pos0, t_rows=t_rows, n_blk=n_blk),
        out_shape=(jax.ShapeDtypeStruct((R, nbp), F32), jax.ShapeDtypeStruct((N_SELECT, R), jnp.int32)),
        grid=(R // LANES,),
        in_specs=[pl.BlockSpec((LANES, nbp), lambda i: (i, 0))],
        out_specs=(pl.BlockSpec((LANES, nbp), lambda i: (i, 0)), pl.BlockSpec((N_SELECT, LANES), lambda i: (0, i))),
        compiler_params=_cparams(("parallel",)), name="topk")(imp2d)


def _attn_body(*refs, mode, pairs, tk, pos_q0, pos_k0, scale):
    if mode == 'slc':
        q_ref, kv_ref, sel_ref, o_ref = refs
    else:
        q_ref, kv_ref, o_ref = refs
    qi = pl.program_id(2)
    tq = q_ref.shape[1]
    n_k = kv_ref.shape[1]
    n_tiles = n_k // tk
    q = q_ref[0]
    pq0 = pos_q0 + qi * tq
    qpos = pq0 + lax.broadcasted_iota(jnp.int32, (tq, 1), 0)
    if mode == 'none':
        lo, hi = 0, n_tiles
    else:
        r_hi = jnp.minimum(n_k - 1, pq0 + tq - 1 - pos_k0)
        hi = r_hi // tk + 1
        lo = jnp.maximum(0, pq0 - (WINDOW - 1) - pos_k0) // tk if mode == 'win' else 0
    if mode == 'slc':
        sel = sel_ref[0, 0].astype(BF16)
        nbp = sel.shape[1]

    def step(kt, carry):
        r0 = pl.multiple_of(kt * tk, tk)
        kvt = kv_ref[0, pl.ds(r0, tk), :]
        krow = r0 + lax.broadcasted_iota(jnp.int32, (1, tk), 1)
        kpos = pos_k0 + krow
        if mode == 'win':
            d = qpos - kpos
            mask = (d >= 0) & (d < WINDOW) & (kpos >= 0)
        elif mode == 'slc':
            expand = (lax.broadcasted_iota(jnp.int32, (nbp, 1), 0) == kpos // SLC_BLOCK)
            selx = _mm(sel, jnp.where(expand, 1.0, 0.0).astype(BF16))
            mask = (selx > 0.5) & (kpos <= qpos)
        else:
            mask = None
        new = []
        for p, (q_lo, k_lo, v_lo) in enumerate(pairs):
            m, l, acc = carry[3 * p:3 * p + 3]
            s = _mm_nt(q[:, q_lo:q_lo + HEAD_DIM], kvt[:, k_lo:k_lo + HEAD_DIM]) * scale
            if mask is not None:
                s = jnp.where(mask, s, -jnp.inf)
            m_new = jnp.maximum(m, jnp.max(s, -1, keepdims=True))
            alpha = jnp.exp(m - m_new)
            pr = jnp.exp(s - m_new)
            l = alpha * l + jnp.sum(pr, -1, keepdims=True)
            acc = alpha * acc + _mm(pr, kvt[:, v_lo:v_lo + HEAD_DIM])
            new += [m_new, l, acc]
        return tuple(new)

    init = []
    for _ in pairs:
        init += [jnp.full((tq, 1), NEG_BIG, F32), jnp.zeros((tq, 1), F32), jnp.zeros((tq, HEAD_DIM), F32)]
    res = lax.fori_loop(lo, hi, step, tuple(init))
    o_ref[0] = jnp.concatenate([res[3 * p + 2] / jnp.maximum(res[3 * p + 1], 1e-30) for p in range(len(pairs))], 1)


def _attn(q, kv, mode, pairs, kv_width, pos_q0, pos_k0, tq, tk, sel=None):
    B, T, _ = q.shape
    n_k = kv.shape[1]
    in_specs = [pl.BlockSpec((1, tq, LANES), lambda b, h, i: (b, i, h)),
                pl.BlockSpec((1, n_k, kv_width), lambda b, h, i: (b, 0, h))]
    args = [q, kv]
    if mode == 'slc':
        in_specs.append(pl.BlockSpec((1, 1, tq, sel.shape[-1]), lambda b, h, i: (b, h, i, 0)))
        args.append(sel)
    return pl.pallas_call(
        functools.partial(_attn_body, mode=mode, pairs=pairs, tk=tk, pos_q0=pos_q0, pos_k0=pos_k0, scale=SCALE),
        out_shape=jax.ShapeDtypeStruct((B, T, GW), F32), grid=(B, 2, T // tq),
        in_specs=in_specs, out_specs=pl.BlockSpec((1, tq, LANES), lambda b, h, i: (b, i, h)),
        compiler_params=_cparams(("parallel", "parallel", "arbitrary")), name="attn_" + mode)(*args)


_GQA_PAIRS = ((0, 0, HEAD_DIM), (HEAD_DIM, 0, HEAD_DIM))
_MHA_PAIRS = ((0, 0, HEAD_DIM), (HEAD_DIM, 2 * HEAD_DIM, 3 * HEAD_DIM))


def _slc_paged_body(idx_ref, phys_ref, *refs, pos0, blk0, t_real):
    del phys_ref
    q_ref = refs[0]
    blk_refs = refs[1:1 + N_SELECT]
    new_ref = refs[1 + N_SELECT]
    o_ref = refs[2 + N_SELECT]
    b, h, t = pl.program_id(0), pl.program_id(1), pl.program_id(2)
    base = ((b * NSA_KV_HEADS + h) * t_real + t) * N_SELECT
    qrow = q_ref[0, pl.ds(t, 1), :]
    q2 = jnp.concatenate([qrow[:, :HEAD_DIM], qrow[:, HEAD_DIM:], jnp.zeros((6, HEAD_DIM), F32)], 0)
    pos = pos0 + t
    newblk = new_ref[0]
    ks, vs, oks = [], [], []
    lane = lax.broadcasted_iota(jnp.int32, (1, SLC_BLOCK), 1)
    for j in range(N_SELECT):
        idx = idx_ref[base + j]
        data = jnp.where(idx >= blk0, newblk, blk_refs[j][0])
        ks.append(data[:, :HEAD_DIM])
        vs.append(data[:, HEAD_DIM:])
        oks.append(jnp.where((idx >= 0) & (idx * SLC_BLOCK + lane <= pos), 0.0, -jnp.inf))
    kc = jnp.concatenate(ks, 0)
    vc = jnp.concatenate(vs, 0)
    s = _mm_nt(q2, kc) * SCALE + jnp.concatenate(oks, 1)
    m = jnp.max(s, -1, keepdims=True)
    e = jnp.exp(s - jnp.where(m == -jnp.inf, 0.0, m))
    p = e / jnp.maximum(jnp.sum(e, -1, keepdims=True), 1e-30)
    o = _mm(p, vc)
    orow = jnp.concatenate([o[0:1], o[1:2]], 1)
    o_ref[0, 0, 0] = jnp.broadcast_to(orow, (8, LANES))


def _slc_paged(q_rot, cache_blocks, new_rows, idx_flat, phys_flat, pos0, blk0, t_real):
    B = q_rot.shape[0]
    tp = q_rot.shape[1]

    def blk_spec(j):
        def imap(b, h, t, idx, phys):
            return (phys[((b * NSA_KV_HEADS + h) * t_real + t) * N_SELECT + j], 0, h)
        return pl.BlockSpec((1, SLC_BLOCK, LANES), imap)

    gs = pltpu.PrefetchScalarGridSpec(
        num_scalar_prefetch=2, grid=(B, NSA_KV_HEADS, t_real),
        in_specs=[pl.BlockSpec((1, tp, LANES), lambda b, h, t, idx, phys: (b, 0, h))]
        + [blk_spec(j) for j in range(N_SELECT)]
        + [pl.BlockSpec((1, SLC_BLOCK, LANES), lambda b, h, t, idx, phys: (b, 0, h))],
        out_specs=pl.BlockSpec((1, 1, 1, 8, LANES), lambda b, h, t, idx, phys: (b, h, t, 0, 0)))
    out = pl.pallas_call(
        functools.partial(_slc_paged_body, pos0=pos0, blk0=blk0, t_real=t_real),
        out_shape=jax.ShapeDtypeStruct((B, NSA_KV_HEADS, t_real, 8, LANES), F32), grid_spec=gs,
        compiler_params=_cparams(("parallel", "parallel", "arbitrary")), name="slc_paged")(
            idx_flat, phys_flat, q_rot, *([cache_blocks] * N_SELECT), new_rows)
    return jnp.transpose(out[:, :, :, 0, :], (0, 2, 1, 3)).reshape(B, t_real, GW)


def _rms(x, g):
    return x * lax.rsqrt(jnp.mean(x * x, -1, keepdims=True) + NORM_EPS) * g


def _rope(x, pos):
    half = HEAD_DIM // 2
    inv = ROPE_THETA ** (-jnp.arange(half, dtype=F32) / half)
    ang = pos.astype(F32)[:, None] * inv
    cos, sin = jnp.cos(ang)[:, None, :], jnp.sin(ang)[:, None, :]
    x1, x2 = x[..., :half], x[..., half:]
    return jnp.concatenate([x1 * cos - x2 * sin, x1 * sin + x2 * cos], -1)


def _pad_t(x, tp, value=0.0):
    t = x.shape[1]
    if t == tp:
        return x
    return jnp.pad(x, ((0, 0), (0, tp - t)) + ((0, 0),) * (x.ndim - 2), constant_values=value)


def _round_up(n, m):
    return -(-n // m) * m


def _cover_matrix(n_cmp, n_cmp_pad, n_slc, nbp):
    start = np.arange(n_cmp_pad)[:, None] * CMP_STRIDE
    blk = np.arange(nbp)[None, :]
    cov = (start < (blk + 1) * SLC_BLOCK) & (start + CMP_BLOCK > blk * SLC_BLOCK)
    cov &= (np.arange(n_cmp_pad)[:, None] < n_cmp) & (blk < n_slc)
    return jnp.asarray(cov.astype(np.float32), dtype=BF16)


def _mlstm_mixer(P, conv_buf, c0, n0, m0, prm):
    B, T, _ = P.shape
    qk_raw = jnp.concatenate([P[..., C_MQ:C_MQ + GW], P[..., C_MK:C_MK + GW]], -1)
    ext = jnp.concatenate([conv_buf, qk_raw], 1)
    w = prm['mlstm_conv_w']
    qk = jax.nn.silu(prm['mlstm_conv_b'] + sum(ext[:, j:j + T] * w[j] for j in range(MLSTM_CONV)))
    gates = P[..., C_MG:C_MG + LANES]
    li = gates[..., 0:4] + prm['mlstm_gate_b'][0]
    lf = jax.nn.log_sigmoid(gates[..., 4:8] + prm['mlstm_gate_b'][1])
    tp = _round_up(T, MLSTM_L)
    g128 = jnp.concatenate([_pad_t(li, tp, NEG_BIG), _pad_t(lf, tp), jnp.zeros((B, tp, LANES - 8), F32)], -1)
    m0p = jnp.pad(m0, ((0, 0), (0, LANES - N_HEADS))).reshape(B, 1, LANES)
    h, C, n, m = _mlstm(_pad_t(qk[..., :GW], tp), _pad_t(qk[..., GW:], tp), _pad_t(P[..., C_MV:C_MV + GW], tp),
                        g128, c0, n0, m0p)
    h = h[:, :T].reshape(B, T, N_HEADS, HEAD_DIM)
    h = _rms(h, prm['mlstm_norm_g'].reshape(N_HEADS, HEAD_DIM)).reshape(B, T, GW)
    out = jax.nn.sigmoid(P[..., C_MO:C_MO + GW]) * h * jax.nn.silu(P[..., C_MZ:C_MZ + GW])
    return out, C, n, m[:, 0, :N_HEADS], ext[:, -(MLSTM_CONV - 1):]


def _rwkv_mixer(P, shift_buf, s0, prm, prec):
    B, T, _ = P.shape
    u = jnp.concatenate([P[..., C_RR:C_RR + 3 * GW], P[..., C_RL:C_RL + LANES]], -1)
    prev = jnp.concatenate([shift_buf, u[:, :-1]], 1)
    um = u + (prev - u) * prm['rwkv_mu']
    r, k, v = um[..., :GW], um[..., GW:2 * GW], um[..., 2 * GW:3 * GW]
    wl, al = um[..., 3 * GW:3 * GW + DECAY_LORA], um[..., 3 * GW + DECAY_LORA:]
    w_pre = prm['rwkv_w0'] + jnp.dot(jnp.tanh(wl), prm['rwkv_w2'])
    log_w = -jnp.exp(-jax.nn.softplus(-w_pre) - 0.5)
    a = jax.nn.sigmoid(prm['rwkv_a0'] + jnp.dot(al, prm['rwkv_a2']))
    k_k, k_a, r_k = prm['rwkv_kk'][0], prm['rwkv_kk'][1], prm['rwkv_kk'][2]
    heads = lambda t: t.reshape(B, T, N_HEADS, HEAD_DIM)
    kk = heads(k * k_k)
    kk = (kk / jnp.maximum(jnp.sqrt(jnp.sum(kk * kk, -1, keepdims=True)), 1e-12)).reshape(B, T, GW)
    k_eff = k * (1.0 + (a - 1.0) * k_a)
    tp = _round_up(T, RWKV_L)
    y, S = _rwkv(*(_pad_t(t, tp) for t in (r, log_w, k_eff, v, kk, a)), s0, prec=prec)
    y = heads(y[:, :T])
    mu = jnp.mean(y, -1, keepdims=True)
    var = jnp.mean(jnp.square(y - mu), -1, keepdims=True)
    ln = prm['rwkv_ln'].reshape(2, N_HEADS, HEAD_DIM)
    y = (y - mu) * lax.rsqrt(var + RWKV_LN_EPS) * ln[0] + ln[1]
    rh, vh = heads(r), heads(v)
    y = y + jnp.sum(rh * heads(k_eff) * r_k.reshape(N_HEADS, HEAD_DIM), -1, keepdims=True) * vh
    out = y.reshape(B, T, GW) * jax.nn.silu(P[..., C_RZ:C_RZ + GW])
    return out, S, u[:, -1:]


def _memory_kv(mem, prm):
    B = mem.shape[0]
    kv = _proj_in(mem.reshape(B * N_MEM, D_MODEL), prm['mem_norm_g'], prm['w_mem_kv_bf16'], 2 * GW)
    kv = kv.reshape(B, N_MEM, MEM_HEADS, 2, HEAD_DIM)
    return jnp.stack([_rms(kv[:, :, :, 0], prm['mem_qk_g'][1]), kv[:, :, :, 1]], 3)


def _memory_mixer(P, mem_kv, prm):
    B, T, _ = P.shape
    tp = _round_up(T, 8)
    qn = _rms(P[..., C_CQ:C_CQ + GW].reshape(B, T, MEM_HEADS, HEAD_DIM), prm['mem_qk_g'][0]).reshape(B, T, GW)
    tq = min(256, tp)
    o = _attn(_pad_t(qn, tp), mem_kv.reshape(B, N_MEM, 2 * GW), 'none', _MHA_PAIRS, 2 * LANES, 0, 0, tq, N_MEM)
    return o[:, :T] * jax.nn.silu(P[..., C_CZ:C_CZ + GW])


def _nsa_mixer(P2d, B, T, pos0, win_prefix, prm, past):
    P = P2d.reshape(B, T, DP)
    pos = pos0 + jnp.arange(T, dtype=jnp.int32)
    g = prm['nsa_qk_g']
    tp = _round_up(T, 8)
    qn = _rms(P[..., C_NQ:C_NQ + GW].reshape(B, T, N_HEADS, HEAD_DIM), g[0])
    q_rot = _pad_t(_rope(qn, pos).reshape(B, T, GW), tp)
    qn = _pad_t(qn.reshape(B, T, GW), tp)
    kvrows = lambda c0: P[..., c0:c0 + 2 * KVW].reshape(B, T, NSA_KV_HEADS, 2, HEAD_DIM)
    cmp_new = kvrows(C_NCMP)
    slc_raw, win_raw = kvrows(C_NSLC), kvrows(C_NWIN)
    slc_new = jnp.stack([_rope(_rms(slc_raw[:, :, :, 0], g[2]), pos), slc_raw[:, :, :, 1]], 3)
    win_new = jnp.stack([_rope(_rms(win_raw[:, :, :, 0], g[3]), pos), win_raw[:, :, :, 1]], 3)

    L_all = pos0 + T
    n_sub = max(-(-L_all // CMP_STRIDE), CMP_BLOCK // CMP_STRIDE)
    n_cmp = n_sub - 1
    bd, w2bd, pe_hid = prm['cmp_bd'], prm['cmp_w2bd'], prm['cmp_pe_hid']
    if past is None:
        G = _subproj(cmp_new.reshape(B * T // CMP_STRIDE, SUB_FEAT), bd).reshape(B, T // CMP_STRIDE, 512)
    else:
        cache_cmp3d, cache_slc_blocks, page_table, layer, n_pool = past
        g_pages = _subproj_pages(cache_cmp3d, page_table, layer * n_pool, bd)
        g_new = _subproj(_pad_t(cmp_new.reshape(B, T, 2 * KVW), CMP_STRIDE).reshape(B, SUB_FEAT), bd).reshape(B, 1, 512)
        G = jnp.concatenate([g_pages, g_new], 1)
    n_cmp_pad = _round_up(n_cmp, LANES)
    gb = G[:, 1:, 256:]
    ga = G[:, :, :256]
    fit = lambda t: _pad_t(t, max(n_cmp_pad, t.shape[1]))[:, :n_cmp_pad]
    hid = fit(ga) + fit(gb) + pe_hid
    kv_cmp = _cmp_mlp(hid, w2bd, g[1])
    n_slc = -(-L_all // SLC_BLOCK)
    nbp = _round_up(n_slc, LANES)
    cover = _cover_matrix(n_cmp, n_cmp_pad, n_slc, nbp)
    tq = min(256, tp)
    o_cmp, imp = _cmp_attn(qn, kv_cmp, cover, pos0, tq)

    R = B * NSA_KV_HEADS * tp
    rp = _round_up(R, LANES)
    imp2d = jnp.pad(imp.reshape(R, nbp), ((0, rp - R), (0, 0)))
    sel, idx_t = _topk(imp2d, pos0, tp, n_slc)

    if past is None:
        sel4 = sel[:R].reshape(B, NSA_KV_HEADS, tp, nbp)
        o_slc = _attn(q_rot, slc_new.reshape(B, T, 2 * KVW), 'slc', _GQA_PAIRS, LANES, pos0, pos0, tq,
                      min(256, T), sel=sel4)
    else:
        bpp = PAGE_SIZE // SLC_BLOCK
        idx = idx_t[:, :R].T.reshape(B, NSA_KV_HEADS, tp, N_SELECT)[:, :, :T]
        idc = jnp.clip(idx, 0, page_table.shape[1] * bpp - 1)
        phys = page_table[jnp.arange(B)[:, None, None, None], idc // bpp]
        phys_blk = (layer * n_pool + phys) * bpp + idc % bpp
        new_rows = _pad_t(slc_new.reshape(B, T, 2 * KVW), SLC_BLOCK)
        o_slc = _slc_paged(q_rot, cache_slc_blocks, new_rows, idx.reshape(-1), phys_blk.reshape(-1).astype(jnp.int32),
                           pos0, pos0 // SLC_BLOCK, T)

    if win_prefix.shape[1] == 0:
        win_ctx = win_new
        pos_k0 = pos0
    else:
        win_ctx = jnp.concatenate([win_prefix, win_new], 1)
        pos_k0 = pos0 - win_prefix.shape[1]
    n_k = win_ctx.shape[1]
    tkw = min(256, _round_up(n_k, LANES))
    kv_win = _pad_t(win_ctx.reshape(B, n_k, 2 * KVW), _round_up(n_k, tkw))
    o_win = _attn(q_rot, kv_win, 'win', _GQA_PAIRS, LANES, pos0, pos_k0, tq, tkw)

    gt = jax.nn.sigmoid(P[..., C_NG:C_NG + 3 * N_HEADS]).reshape(B, T, 3, N_HEADS, 1)
    hd = lambda o: o[:, :T].reshape(B, T, N_HEADS, HEAD_DIM)
    o = gt[:, :, 0] * hd(o_cmp) + gt[:, :, 1] * hd(o_slc) + gt[:, :, 2] * hd(o_win)
    keep = win_prefix.shape[1] if past is not None else min(WINDOW, T)
    out = o.reshape(B, T, GW) * jax.nn.silu(P[..., C_NZ:C_NZ + GW])
    return out, cmp_new, slc_new, win_ctx[:, -keep:]


def _layer(x, pos0, st, mem_kv, prm, past, rwkv_prec=HI):
    conv_buf, c0, n0, m0, s0, shift_buf, win_prefix = st
    B, T, _ = x.shape
    x2d = x.reshape(B * T, D_MODEL)
    P2d = _proj_in(x2d, prm['norm_g'], prm['w_in_bf16'], 640)
    P = P2d.reshape(B, T, DP)
    y_m, C, n, m, conv_new = _mlstm_mixer(P, conv_buf, c0, n0, m0, prm)
    y_n, cmp_new, slc_new, win_new = _nsa_mixer(P2d, B, T, pos0, win_prefix, prm, past)
    y_r, S, shift_new = _rwkv_mixer(P, shift_buf, s0, prm, rwkv_prec)
    y_c = _memory_mixer(P, mem_kv, prm)
    ys = [t.reshape(B * T, GW) for t in (y_m, y_n, y_r, y_c)]
    out = _proj_out(x2d, ys, prm['w_out_bf16']).reshape(B, T, D_MODEL)
    return out, (cmp_new, slc_new, win_new, C, n, m, conv_new, S, shift_new)


def _prep_params(l, p):
    prm = {k: v[l] for k, v in p.items()}
    src = jnp.asarray(np.maximum(_SRC, 0), jnp.int32)
    keep = jnp.asarray((_SRC >= 0).astype(np.float32))
    prm['w_in_bf16'] = (jnp.take(prm['w_in'], src, axis=1) * keep).astype(BF16)
    prm['w_out_bf16'] = prm['w_out'].astype(BF16)
    wm = prm['w_mem_kv'].reshape(D_MODEL, 2, MEM_HEADS, HEAD_DIM)
    prm['w_mem_kv_bf16'] = jnp.transpose(wm, (0, 2, 1, 3)).reshape(D_MODEL, 2 * GW).astype(BF16)
    w1 = prm['nsa_cmp_w1']
    eye_h = jnp.eye(NSA_KV_HEADS, dtype=F32)
    eye_c = jnp.eye(2, dtype=F32)
    w1r = w1.reshape(2, 2, CMP_STRIDE, HEAD_DIM, HEAD_DIM)
    bd = jnp.einsum('hH,cC,crsde->shcdrHCe', eye_h, eye_c, w1r)
    prm['cmp_bd'] = bd.reshape(SUB_FEAT, 4 * KVW).astype(BF16)
    w2 = prm['nsa_cmp_w2']
    prm['cmp_w2bd'] = jnp.einsum('hH,cC,ced->hceHCd', eye_h, eye_c, w2).reshape(2 * KVW, 2 * KVW)
    pe_hid = jnp.einsum('csd,csde->ce', prm['nsa_pe'], w1, precision=HI)
    prm['cmp_pe_hid'] = jnp.tile(pe_hid.reshape(1, 2 * HEAD_DIM), (1, NSA_KV_HEADS)).reshape(2 * KVW)
    return prm


def kernel(x_prompt, x_sample, cache_cmp_kv, cache_slc_kv, cache_win_kv, cache_mem_kv, state_mlstm_C, state_mlstm_n, state_mlstm_m, state_mlstm_conv, state_rwkv_S, state_rwkv_shift, page_table, mem_prompt, norm_g, w_in, w_out, mlstm_conv_w, mlstm_conv_b, mlstm_gate_b, mlstm_norm_g, nsa_qk_g, nsa_pe, nsa_cmp_w1, nsa_cmp_w2, rwkv_mu, rwkv_w0, rwkv_w2, rwkv_a0, rwkv_a2, rwkv_kk, rwkv_ln, mem_norm_g, w_mem_kv, mem_qk_g):
    params = dict(norm_g=norm_g, w_in=w_in, w_out=w_out, mlstm_conv_w=mlstm_conv_w, mlstm_conv_b=mlstm_conv_b,
                  mlstm_gate_b=mlstm_gate_b, mlstm_norm_g=mlstm_norm_g, nsa_qk_g=nsa_qk_g, nsa_pe=nsa_pe,
                  nsa_cmp_w1=nsa_cmp_w1, nsa_cmp_w2=nsa_cmp_w2, rwkv_mu=rwkv_mu, rwkv_w0=rwkv_w0, rwkv_w2=rwkv_w2,
                  rwkv_a0=rwkv_a0, rwkv_a2=rwkv_a2, rwkv_kk=rwkv_kk, rwkv_ln=rwkv_ln, mem_norm_g=mem_norm_g,
                  w_mem_kv=w_mem_kv, mem_qk_g=mem_qk_g)
    depth = norm_g.shape[0]
    B = x_prompt.shape[0]
    n_pool = cache_cmp_kv.shape[1]
    past_len = page_table.shape[1] * PAGE_SIZE
    cache_cmp3d = cache_cmp_kv.reshape(depth * n_pool, PAGE_SIZE // CMP_STRIDE, SUB_FEAT)
    cache_slc_blocks = cache_slc_kv.reshape(depth * n_pool * (PAGE_SIZE // SLC_BLOCK), SLC_BLOCK, 2 * KVW)
    xp, xs = x_prompt, x_sample
    new_p, new_s, new_mem = [], [], []
    for l in range(depth):
        prm = _prep_params(l, params)
        mem_kv_p = _memory_kv(mem_prompt, prm)
        st_p = (jnp.zeros((B, MLSTM_CONV - 1, 2 * GW), F32),
                jnp.zeros((B, N_HEADS, HEAD_DIM, HEAD_DIM), F32),
                jnp.zeros((B, N_HEADS, HEAD_DIM), F32),
                jnp.full((B, N_HEADS), M_INIT, F32),
                jnp.zeros((B, N_HEADS, HEAD_DIM, HEAD_DIM), F32),
                jnp.zeros((B, 1, RWKV_SHIFT), F32),
                jnp.zeros((B, 0, NSA_KV_HEADS, 2, HEAD_DIM), F32))
        xp, sp = _layer(xp, 0, st_p, mem_kv_p, prm, None)
        st_s = (state_mlstm_conv[l], state_mlstm_C[l], state_mlstm_n[l], state_mlstm_m[l],
                state_rwkv_S[l], state_rwkv_shift[l], cache_win_kv[l])
        xs, ss = _layer(xs, past_len, st_s, cache_mem_kv[l], prm,
                        (cache_cmp3d, cache_slc_blocks, page_table, l, n_pool))
        new_p.append(sp)
        new_s.append(ss)
        new_mem.append(mem_kv_p)
    stack = lambda states, i: jnp.stack([s[i] for s in states])
    outs = [xp, xs]
    for i in range(3):
        outs += [stack(new_p, i), stack(new_s, i)]
    outs.append(jnp.stack(new_mem))
    for i in range(3, 9):
        outs += [stack(new_p, i), stack(new_s, i)]
    return tuple(outs)
```

```python
import functools

import numpy as np
import jax
import jax.numpy as jnp
from jax import lax
from jax.experimental import pallas as pl
from jax.experimental.pallas import tpu as pltpu

F32 = jnp.float32
BF16 = jnp.bfloat16
HI = lax.Precision.HIGHEST

D_MODEL = 1024
PAGE_SIZE = 128
HEAD_DIM = 64
GW = D_MODEL // 4
N_HEADS = GW // HEAD_DIM
SCALE = HEAD_DIM ** -0.5
MLSTM_CONV = 4
M_INIT = -1e30
NSA_KV_HEADS = 2
KVW = NSA_KV_HEADS * HEAD_DIM
CMP_BLOCK = 32
CMP_STRIDE = 16
SLC_BLOCK = 64
N_SELECT = 16
WINDOW = 512
DECAY_LORA = 64
AAA_LORA = 64
RWKV_SHIFT = 3 * GW + DECAY_LORA + AAA_LORA
RWKV_LN_EPS = HEAD_DIM * 1e-5
N_MEM = 256
MEM_HEADS = 4
ROPE_THETA = 10000.0
NORM_EPS = 1e-6

LANES = 128
MLSTM_L = 128
RWKV_L = 64
NEG_BIG = -1e30

_M0, _N0, _R0, _C0 = 0, 1288, 2580, 3732
C_MQ, C_MK, C_MV, C_MO, C_MZ = 0, 256, 512, 768, 1024
C_NQ, C_NCMP, C_NSLC, C_NWIN, C_NZ = 1280, 1536, 1792, 2048, 2304
C_RR, C_RK, C_RV, C_RZ = 2560, 2816, 3072, 3328
C_CQ, C_CZ = 3584, 3840
C_RL = 4096
C_MG = 4224
C_NG = 4352
DP = 4480


def _packed_src():
    src = -np.ones((DP,), np.int64)

    def put(dst, lo, n):
        src[dst:dst + n] = np.arange(lo, lo + n)

    put(C_MQ, _M0, 256); put(C_MK, _M0 + 256, 256); put(C_MV, _M0 + 512, 256)
    put(C_MG, _M0 + 768, 8); put(C_MO, _M0 + 776, 256); put(C_MZ, _M0 + 1032, 256)
    put(C_NQ, _N0, 256)
    for i, base in enumerate((C_NCMP, C_NSLC, C_NWIN)):
        ksrc = _N0 + 256 + 256 * i
        vsrc = ksrc + 128
        for h in range(2):
            put(base + 128 * h, ksrc + 64 * h, 64)
            put(base + 128 * h + 64, vsrc + 64 * h, 64)
    put(C_NG, _N0 + 1024, 12); put(C_NZ, _N0 + 1036, 256)
    put(C_RR, _R0, 256); put(C_RK, _R0 + 256, 256); put(C_RV, _R0 + 512, 256)
    put(C_RL, _R0 + 768, 128); put(C_RZ, _R0 + 896, 256)
    put(C_CQ, _C0, 256); put(C_CZ, _C0 + 256, 256)
    return src


_SRC = _packed_src()


def _split2(a):
    hi = a.astype(BF16)
    return hi, (a - hi.astype(F32)).astype(BF16)


def _dg(a, b, dims, prec):
    dn = (dims, ((), ()))
    if prec == 'bf16':
        return lax.dot_general(a.astype(BF16), b.astype(BF16), dn, preferred_element_type=F32)
    d = lambda x, y: lax.dot_general(x, y, dn, preferred_element_type=F32)
    if prec == 'x3':
        ah, al = _split2(a)
        bh, bl = _split2(b)
        return d(ah, bh) + (d(ah, bl) + d(al, bh))
    if prec in ('r3', 'l3'):
        exact, other = (a, b) if prec == 'r3' else (b, a)
        o1, rest = other.astype(BF16), None
        rest = other - o1.astype(F32)
        o2 = rest.astype(BF16)
        o3 = (rest - o2.astype(F32)).astype(BF16)
        e = exact.astype(BF16)
        if prec == 'r3':
            return d(e, o1) + (d(e, o2) + d(e, o3))
        return d(o1, e) + (d(o2, e) + d(o3, e))
    return lax.dot_general(a, b, dn, preferred_element_type=F32, precision=prec)


def _mm(a, b, prec=None):
    return _dg(a, b, ((1,), (0,)), prec)


def _mm_nt(a, b, prec=None):
    return _dg(a, b, ((1,), (1,)), prec)


def _mm_tn(a, b, prec=None):
    return _dg(a, b, ((0,), (0,)), prec)


def _cparams(sem, vmem_mb=None):
    kw = dict(dimension_semantics=sem)
    if vmem_mb is not None:
        kw['vmem_limit_bytes'] = vmem_mb * 1024 * 1024
    return pltpu.CompilerParams(**kw)


def _proj_in_body(x_ref, g_ref, w_ref, o_ref):
    x = x_ref[...]
    h = x * lax.rsqrt(jnp.mean(x * x, -1, keepdims=True) + NORM_EPS) * g_ref[...]
    o_ref[...] = jnp.dot(h.astype(BF16), w_ref[...], preferred_element_type=F32)


def _proj_in(x2d, g, w_bf16, tn):
    n, d = x2d.shape
    dn = w_bf16.shape[1]
    tm = min(1024, n)
    return pl.pallas_call(
        _proj_in_body, out_shape=jax.ShapeDtypeStruct((n, dn), F32), grid=(n // tm, dn // tn),
        in_specs=[pl.BlockSpec((tm, d), lambda i, j: (i, 0)),
                  pl.BlockSpec((1, d), lambda i, j: (0, 0)),
                  pl.BlockSpec((d, tn), lambda i, j: (0, j))],
        out_specs=pl.BlockSpec((tm, tn), lambda i, j: (i, j)),
        compiler_params=_cparams(("parallel", "arbitrary"), 48), name="proj_in")(x2d, g.reshape(1, d), w_bf16)


def _proj_out_body(x_ref, y0, y1, y2, y3, w_ref, o_ref):
    acc = x_ref[...]
    for i, y in enumerate((y0, y1, y2, y3)):
        acc = acc + jnp.dot(y[...].astype(BF16), w_ref[i * GW:(i + 1) * GW, :], preferred_element_type=F32)
    o_ref[...] = acc


def _proj_out(x2d, ys, w_bf16):
    n, d = x2d.shape
    tm = min(512, n)
    yspec = pl.BlockSpec((tm, GW), lambda i: (i, 0))
    return pl.pallas_call(
        _proj_out_body, out_shape=jax.ShapeDtypeStruct((n, d), F32), grid=(n // tm,),
        in_specs=[pl.BlockSpec((tm, d), lambda i: (i, 0)), yspec, yspec, yspec, yspec,
                  pl.BlockSpec((d, d), lambda i: (0, 0))],
        out_specs=pl.BlockSpec((tm, d), lambda i: (i, 0)),
        compiler_params=_cparams(("parallel",), 48), name="proj_out")(x2d, *ys, w_bf16)


def _mlstm_body(q_ref, k_ref, v_ref, g_ref, c0_ref, n0_ref, m0_ref, h_ref, c_ref, n_ref, m_ref,
                c_scr, n_scr, m_scr):
    c = pl.program_id(1)

    @pl.when(c == 0)
    def _():
        c_scr[...] = c0_ref[0]
        n_scr[...] = n0_ref[0]
        m_scr[...] = m0_ref[0]

    L = q_ref.shape[1]
    g = g_ref[0]
    row = lax.broadcasted_iota(jnp.int32, (L, L), 0)
    col = lax.broadcasted_iota(jnp.int32, (L, L), 1)
    causal = row >= col
    bc = _mm(causal.astype(F32), g, 'r3')
    gt = g.T
    br = _mm(gt, (row <= col).astype(F32), 'l3')
    q = q_ref[0]
    k = k_ref[0]
    v = v_ref[0]
    outs = []
    for h in range(N_HEADS):
        sl = slice(h * HEAD_DIM, (h + 1) * HEAD_DIM)
        qh = q[:, sl] * (HEAD_DIM ** -0.5)
        kh = k[:, sl]
        vh = v[:, sl]
        b_col = bc[:, 4 + h:5 + h]
        li_col = g[:, h:h + 1]
        b_row = br[4 + h:5 + h, :]
        li_row = gt[h:h + 1, :]
        m_prev = m_scr[:, h:h + 1]
        log_d = jnp.where(causal, b_col - b_row + li_row, -jnp.inf)
        log_inter = b_col + m_prev
        m_t = jnp.maximum(jnp.max(log_d, -1, keepdims=True), log_inter)
        s = _mm_nt(qh, kh) * jnp.exp(log_d - m_t)
        w_inter = jnp.exp(log_inter - m_t)
        ch = c_scr[h]
        nh = n_scr[h:h + 1, :]
        num = _mm(s, vh) + w_inter * _mm(qh, ch)
        den = jnp.sum(s, -1, keepdims=True) + w_inter * jnp.sum(qh * nh, -1, keepdims=True)
        outs.append(num / jnp.maximum(jnp.abs(den), jnp.exp(-m_t)))
        b_end = b_col[L - 1:L, :]
        log_w = b_end - b_col + li_col
        m_new = jnp.maximum(b_end + m_prev, jnp.max(log_w, 0, keepdims=True))
        wk = jnp.exp(log_w - m_new)
        decay = jnp.exp(b_end + m_prev - m_new)
        kw = kh * wk
        c_scr[h] = decay * ch + _mm_tn(kw, vh)
        n_scr[h:h + 1, :] = decay * nh + jnp.sum(kw, 0, keepdims=True)
        m_scr[:, h:h + 1] = m_new
    h_ref[0] = jnp.concatenate(outs, axis=1)

    @pl.when(c == pl.num_programs(1) - 1)
    def _():
        c_ref[0] = c_scr[...]
        n_ref[0] = n_scr[...]
        m_ref[0] = m_scr[...]


def _mlstm(q, k, v, gates, c0, n0, m0):
    B, T, _ = q.shape
    L = MLSTM_L
    tok = pl.BlockSpec((1, L, GW), lambda b, c: (b, c, 0))
    sc = pl.BlockSpec((1, N_HEADS, HEAD_DIM, HEAD_DIM), lambda b, c: (b, 0, 0, 0))
    sn = pl.BlockSpec((1, N_HEADS, HEAD_DIM), lambda b, c: (b, 0, 0))
    sm = pl.BlockSpec((1, 1, LANES), lambda b, c: (b, 0, 0))
    return pl.pallas_call(
        _mlstm_body,
        out_shape=(jax.ShapeDtypeStruct((B, T, GW), F32), jax.ShapeDtypeStruct(c0.shape, F32),
                   jax.ShapeDtypeStruct(n0.shape, F32), jax.ShapeDtypeStruct(m0.shape, F32)),
        grid=(B, T // L),
        in_specs=[tok, tok, tok, pl.BlockSpec((1, L, LANES), lambda b, c: (b, c, 0)), sc, sn, sm],
        out_specs=(tok, sc, sn, sm),
        scratch_shapes=[pltpu.VMEM((N_HEADS, HEAD_DIM, HEAD_DIM), F32), pltpu.VMEM((N_HEADS, HEAD_DIM), F32),
                        pltpu.VMEM((1, LANES), F32)],
        compiler_params=_cparams(("parallel", "arbitrary")), name="mlstm")(q, k, v, gates, c0, n0, m0)


def _rwkv_body(r_ref, w_ref, k_ref, v_ref, kk_ref, a_ref, s0_ref, y_ref, s_ref, s_scr, *, prec):
    c_id = pl.program_id(1)

    @pl.when(c_id == 0)
    def _():
        s_scr[...] = s0_ref[0]

    L = r_ref.shape[1]
    D = HEAD_DIM
    row = lax.broadcasted_iota(jnp.int32, (L, L), 0)
    col = lax.broadcasted_iota(jnp.int32, (L, L), 1)
    lower = row >= col
    strict = row > col
    tril = lower.astype(F32)
    r_all, w_all, k_all, v_all, kk_all, a_all = (x[0] for x in (r_ref, w_ref, k_ref, v_ref, kk_ref, a_ref))
    n_sq = int(np.log2(L)) - 1
    pc, pa, prec = prec
    outs = []
    for h in range(N_HEADS):
        sl = slice(h * D, (h + 1) * D)
        r, w, k, v, kk, a = (x[:, sl] for x in (r_all, w_all, k_all, v_all, kk_all, a_all))
        cum = _mm(tril, w, pc)
        c_last = cum[L - 1:L, :]
        e_neg = jnp.exp(-cum)
        kh = kk * jnp.exp(cum - w)
        bt = kk * a * e_neg
        kt = k * e_neg
        rh = r * jnp.exp(cum)
        gram = _mm_nt(jnp.concatenate([kh, rh], 0), jnp.concatenate([bt, kt], 0), pa)
        A = jnp.where(strict, gram[:L, :L], 0.0)
        Bm = jnp.where(strict, gram[:L, L:], 0.0)
        Mb = jnp.where(lower, gram[L:, :L], 0.0)
        Mk = jnp.where(lower, gram[L:, L:], 0.0)
        X = jnp.concatenate([kh, _mm(Bm, v, pa)], 1)
        Pw = A
        X = X - _mm(Pw, X, pa)
        for _ in range(n_sq):
            Pw = _mm(Pw, Pw, pa)
            X = X + _mm(Pw, X, pa)
        W = X[:, :D]
        U0 = X[:, D:]
        e_end = jnp.exp(c_last - cum)
        bp = kk * a * e_end
        kp = k * e_end
        xtb = _mm_tn(X, bp, prec)
        wtb = xtb[:D]
        N = _mm_tn(v, kp, prec) - xtb[D:]
        mbx = _mm(Mb, X, prec)
        qp = rh - mbx[:, :D]
        y0 = _mm(Mk, v, prec) - mbx[:, D:]
        s0 = s_scr[h]
        outs.append(_mm_nt(qp, s0, prec) + y0)
        s_scr[h] = s0 * jnp.exp(c_last) - _mm(s0, wtb, prec) + N
    y_ref[0] = jnp.concatenate(outs, axis=1)

    @pl.when(c_id == pl.num_programs(1) - 1)
    def _():
        s_ref[0] = s_scr[...]


def _rwkv(r, w, k, v, kk, a, s0, prec):
    B, T, _ = r.shape
    L = RWKV_L
    tok = pl.BlockSpec((1, L, GW), lambda b, c: (b, c, 0))
    st = pl.BlockSpec((1, N_HEADS, HEAD_DIM, HEAD_DIM), lambda b, c: (b, 0, 0, 0))
    return pl.pallas_call(
        functools.partial(_rwkv_body, prec=prec),
        out_shape=(jax.ShapeDtypeStruct((B, T, GW), F32), jax.ShapeDtypeStruct(s0.shape, F32)),
        grid=(B, T // L), in_specs=[tok] * 6 + [st], out_specs=(tok, st),
        scratch_shapes=[pltpu.VMEM((N_HEADS, HEAD_DIM, HEAD_DIM), F32)],
        compiler_params=_cparams(("parallel", "arbitrary")), name="rwkv")(r, w, k, v, kk, a, s0)


def _subproj_accumulate(load_rows, w_ref, o_ref):
    n = o_ref.shape[-2]
    accs = []
    for h in range(NSA_KV_HEADS):
        acc = jnp.zeros((n, 2 * LANES), F32)
        for s in range(CMP_STRIDE):
            acc = acc + jnp.dot(load_rows(h, s, n).astype(BF16), w_ref[s], preferred_element_type=F32)
        accs.append(acc)
    out = jnp.concatenate([accs[0][:, :LANES], accs[1][:, :LANES], accs[0][:, LANES:], accs[1][:, LANES:]], 1)
    o_ref[...] = out.reshape(o_ref.shape)


def _subproj_body(x0_ref, x1_ref, w_ref, o_ref):
    xs = (x0_ref, x1_ref)
    _subproj_accumulate(lambda h, s, n: xs[h][pl.ds(s, n, stride=CMP_STRIDE), :], w_ref, o_ref)


def _subproj(rows2d, col0, w):
    n = rows2d.shape[0]
    tm = min(2048, n)
    return pl.pallas_call(
        _subproj_body, out_shape=jax.ShapeDtypeStruct((n // CMP_STRIDE, 4 * LANES), F32), grid=(n // tm,),
        in_specs=[pl.BlockSpec((tm, LANES), lambda i: (i, col0)), pl.BlockSpec((tm, LANES), lambda i: (i, col0 + 1)),
                  pl.BlockSpec((CMP_STRIDE, LANES, 2 * LANES), lambda i: (0, 0, 0))],
        out_specs=pl.BlockSpec((tm // CMP_STRIDE, 4 * LANES), lambda i: (i, 0)),
        compiler_params=_cparams(("parallel",)), name="cmp_subproj")(rows2d, rows2d, w)


_PAGES_PER_STEP = 32


def _subproj_pages_body(pt_ref, *refs):
    del pt_ref
    npg = len(refs) - 3
    w_ref, o_ref, rows_scr = refs[npg], refs[npg + 1], refs[npg + 2]
    for p in range(npg):
        for h in range(NSA_KV_HEADS):
            rows_scr[h, p * PAGE_SIZE:(p + 1) * PAGE_SIZE, :] = refs[p][0, 0, h].reshape(2 * HEAD_DIM, PAGE_SIZE).T
    _subproj_accumulate(lambda h, s, n: rows_scr[h, pl.ds(s, n, stride=CMP_STRIDE), :], w_ref, o_ref)


def _subproj_pages(cache_t, page_table, layer, w):
    B, n_pages = page_table.shape
    npg = min(_PAGES_PER_STEP, n_pages)
    spp = PAGE_SIZE // CMP_STRIDE

    def page_spec(p):
        return pl.BlockSpec((1, 1, NSA_KV_HEADS, 2, HEAD_DIM, PAGE_SIZE),
                            lambda b, g, pt: (layer, pt[b, g * npg + p], 0, 0, 0, 0))

    gs = pltpu.PrefetchScalarGridSpec(
        num_scalar_prefetch=1, grid=(B, n_pages // npg),
        in_specs=[page_spec(p) for p in range(npg)]
        + [pl.BlockSpec((CMP_STRIDE, LANES, 2 * LANES), lambda b, g, pt: (0, 0, 0))],
        out_specs=pl.BlockSpec((1, npg * spp, 4 * LANES), lambda b, g, pt: (b, g, 0)),
        scratch_shapes=[pltpu.VMEM((NSA_KV_HEADS, npg * PAGE_SIZE, LANES), F32)])
    return pl.pallas_call(
        _subproj_pages_body, out_shape=jax.ShapeDtypeStruct((B, n_pages * spp, 4 * LANES), F32), grid_spec=gs,
        compiler_params=_cparams(("parallel", "arbitrary"), 48), name="cmp_subproj_pages")(
            page_table, *([cache_t] * npg), w)


def _cmp_mlp_body(h_ref, w2_ref, g_ref, o_ref):
    x = h_ref[0]
    kv = _mm(jax.nn.gelu(x), w2_ref[...])
    g = g_ref[...]
    segs = []
    for j in range(4):
        seg = kv[:, j * HEAD_DIM:(j + 1) * HEAD_DIM]
        if j % 2 == 0:
            seg = seg * lax.rsqrt(jnp.mean(seg * seg, -1, keepdims=True) + NORM_EPS) * g
        segs.append(seg)
    o_ref[0] = jnp.concatenate(segs, 1)


def _cmp_mlp(hid, w2bd, g):
    B, n, _ = hid.shape
    tn = min(512, n)
    return pl.pallas_call(
        _cmp_mlp_body, out_shape=jax.ShapeDtypeStruct((B, n, 256), F32), grid=(B, n // tn),
        in_specs=[pl.BlockSpec((1, tn, 256), lambda b, i: (b, i, 0)), pl.BlockSpec((256, 256), lambda b, i: (0, 0)),
                  pl.BlockSpec((1, HEAD_DIM), lambda b, i: (0, 0))],
        out_specs=pl.BlockSpec((1, tn, 256), lambda b, i: (b, i, 0)),
        compiler_params=_cparams(("parallel", "parallel")), name="cmp_mlp")(hid, w2bd, g.reshape(1, HEAD_DIM))


def _cmp_attn_body(q_ref, kv_ref, cov_ref, o_ref, imp_ref, *, pos0):
    qi = pl.program_id(2)
    tq = q_ref.shape[1]
    n = kv_ref.shape[1]
    kv = kv_ref[0]
    k = kv[:, :HEAD_DIM]
    v = kv[:, HEAD_DIM:]
    q = q_ref[0]
    pos = pos0 + qi * tq + lax.broadcasted_iota(jnp.int32, (tq, 1), 0)
    end = lax.broadcasted_iota(jnp.int32, (1, n), 1) * CMP_STRIDE + (CMP_BLOCK - 1)
    mask = end <= pos
    psum = jnp.zeros((tq, n), F32)
    outs = []
    for g in range(2):
        s = _mm_nt(q[:, g * HEAD_DIM:(g + 1) * HEAD_DIM], k) * SCALE
        s = jnp.where(mask, s, -jnp.inf)
        m = jnp.max(s, -1, keepdims=True)
        e = jnp.exp(s - jnp.where(m == -jnp.inf, 0.0, m))
        p = e / jnp.maximum(jnp.sum(e, -1, keepdims=True), 1e-30)
        outs.append(_mm(p, v))
        psum = psum + p
    o_ref[0] = jnp.concatenate(outs, 1)
    hi = psum.astype(BF16)
    lo = (psum - hi.astype(F32)).astype(BF16)
    cov = cov_ref[...]
    imp_ref[0, 0] = _mm(hi, cov) + _mm(lo, cov)


def _cmp_attn(qn, kv_cmp, cover, pos0, tq):
    B, T, _ = qn.shape
    n = kv_cmp.shape[1]
    nbp = cover.shape[1]
    return pl.pallas_call(
        functools.partial(_cmp_attn_body, pos0=pos0),
        out_shape=(jax.ShapeDtypeStruct((B, T, GW), F32), jax.ShapeDtypeStruct((B, NSA_KV_HEADS, T, nbp), F32)),
        grid=(B, NSA_KV_HEADS, T // tq),
        in_specs=[pl.BlockSpec((1, tq, LANES), lambda b, h, i: (b, i, h)),
                  pl.BlockSpec((1, n, LANES), lambda b, h, i: (b, 0, h)),
                  pl.BlockSpec((n, nbp), lambda b, h, i: (0, 0))],
        out_specs=(pl.BlockSpec((1, tq, LANES), lambda b, h, i: (b, i, h)),
                   pl.BlockSpec((1, 1, tq, nbp), lambda b, h, i: (b, h, i, 0))),
        compiler_params=_cparams(("parallel", "parallel", "arbitrary")), name="cmp_attn")(qn, kv_cmp, cover)


def _topk_body(imp_ref, sel_ref, idx_ref, *, pos0, t_rows, n_blk):
    ti = pl.program_id(0)
    x = imp_ref[...]
    nbp = x.shape[1]
    xt = jnp.concatenate([x[:, j * LANES:(j + 1) * LANES].T for j in range(nbp // LANES)], 0)
    r = ti * LANES + lax.broadcasted_iota(jnp.int32, (1, LANES), 1)
    cur = (pos0 + r % t_rows) // SLC_BLOCK
    blk = lax.broadcasted_iota(jnp.int32, (nbp, 1), 0)
    forced = (blk == 0) | (blk == cur) | (blk == cur - 1)
    val = jnp.where(forced, jnp.inf, jnp.where(blk <= cur, xt, -jnp.inf))
    val = jnp.where(blk < n_blk, val, -jnp.inf)
    cnt = jnp.zeros((nbp, LANES), F32)
    for i in range(n_blk):
        vi = val[i:i + 1, :]
        ahead = (vi > val) | ((vi == val) & (blk > i))
        cnt = cnt + jnp.where(ahead, 1.0, 0.0)
    chosen = (cnt < float(N_SELECT)) & (val > -jnp.inf)
    self32 = jnp.where(chosen, 1.0, 0.0)
    sel_ref[...] = jnp.concatenate([self32[j * LANES:(j + 1) * LANES, :].T for j in range(nbp // LANES)], 1)
    blk_f = blk.astype(F32)
    rows = []
    for j in range(N_SELECT):
        hit = chosen & (cnt == float(j))
        rows.append(jnp.sum(jnp.where(hit, blk_f + 1.0, 0.0), 0, keepdims=True) - 1.0)
    idx_ref[...] = jnp.concatenate(rows, 0).astype(jnp.int32)


def _topk(imp2d, pos0, t_rows, n_blk):
    R, nbp = imp2d.shape
    return pl.pallas_call(
        functools.partial(_topk_body, pos0=pos0, t_rows=t_rows, n_blk=n_blk),
        out_shape=(jax.ShapeDtypeStruct((R, nbp), F32), jax.ShapeDtypeStruct((N_SELECT, R), jnp.int32)),
        grid=(R // LANES,),
        in_specs=[pl.BlockSpec((LANES, nbp), lambda i: (i, 0))],
        out_specs=(pl.BlockSpec((LANES, nbp), lambda i: (i, 0)), pl.BlockSpec((N_SELECT, LANES), lambda i: (0, i))),
        compiler_params=_cparams(("parallel",)), name="topk")(imp2d)


def _attn_body(*refs, mode, pairs, tk, pos_q0, pos_k0, scale):
    if mode == 'slc':
        q_ref, kv_ref, sel_ref, o_ref = refs
    else:
        q_ref, kv_ref, o_ref = refs
    qi = pl.program_id(2)
    tq = q_ref.shape[1]
    n_k = kv_ref.shape[1]
    n_tiles = n_k // tk
    q = q_ref[0]
    pq0 = pos_q0 + qi * tq
    qpos = pq0 + lax.broadcasted_iota(jnp.int32, (tq, 1), 0)
    if mode == 'none':
        lo, hi = 0, n_tiles
    else:
        r_hi = jnp.minimum(n_k - 1, pq0 + tq - 1 - pos_k0)
        hi = r_hi // tk + 1
        lo = jnp.maximum(0, pq0 - (WINDOW - 1) - pos_k0) // tk if mode == 'win' else 0
    if mode == 'slc':
        sel = sel_ref[0, 0].astype(BF16)
        nbp = sel.shape[1]

    def step(kt, carry):
        r0 = pl.multiple_of(kt * tk, tk)
        kvt = kv_ref[0, pl.ds(r0, tk), :]
        krow = r0 + lax.broadcasted_iota(jnp.int32, (1, tk), 1)
        kpos = pos_k0 + krow
        if mode == 'win':
            d = qpos - kpos
            mask = (d >= 0) & (d < WINDOW) & (kpos >= 0)
        elif mode == 'slc':
            expand = (lax.broadcasted_iota(jnp.int32, (nbp, 1), 0) == kpos // SLC_BLOCK)
            selx = _mm(sel, jnp.where(expand, 1.0, 0.0).astype(BF16))
            mask = (selx > 0.5) & (kpos <= qpos)
        else:
            mask = None
        new = []
        for p, (q_lo, k_lo, v_lo) in enumerate(pairs):
            m, l, acc = carry[3 * p:3 * p + 3]
            s = _mm_nt(q[:, q_lo:q_lo + HEAD_DIM], kvt[:, k_lo:k_lo + HEAD_DIM]) * scale
            if mask is not None:
                s = jnp.where(mask, s, -jnp.inf)
            m_new = jnp.maximum(m, jnp.max(s, -1, keepdims=True))
            alpha = jnp.exp(m - m_new)
            pr = jnp.exp(s - m_new)
            l = alpha * l + jnp.sum(pr, -1, keepdims=True)
            acc = alpha * acc + _mm(pr, kvt[:, v_lo:v_lo + HEAD_DIM])
            new += [m_new, l, acc]
        return tuple(new)

    init = []
    for _ in pairs:
        init += [jnp.full((tq, 1), NEG_BIG, F32), jnp.zeros((tq, 1), F32), jnp.zeros((tq, HEAD_DIM), F32)]
    res = lax.fori_loop(lo, hi, step, tuple(init))
    o_ref[0] = jnp.concatenate([res[3 * p + 2] / jnp.maximum(res[3 * p + 1], 1e-30) for p in range(len(pairs))], 1)


def _attn(q, kv, mode, pairs, kv_width, pos_q0, pos_k0, tq, tk, sel=None):
    B, T, _ = q.shape
    n_k = kv.shape[1]
    in_specs = [pl.BlockSpec((1, tq, LANES), lambda b, h, i: (b, i, h)),
                pl.BlockSpec((1, n_k, kv_width), lambda b, h, i: (b, 0, h))]
    args = [q, kv]
    if mode == 'slc':
        in_specs.append(pl.BlockSpec((1, 1, tq, sel.shape[-1]), lambda b, h, i: (b, h, i, 0)))
        args.append(sel)
    return pl.pallas_call(
        functools.partial(_attn_body, mode=mode, pairs=pairs, tk=tk, pos_q0=pos_q0, pos_k0=pos_k0, scale=SCALE),
        out_shape=jax.ShapeDtypeStruct((B, T, GW), F32), grid=(B, 2, T // tq),
        in_specs=in_specs, out_specs=pl.BlockSpec((1, tq, LANES), lambda b, h, i: (b, i, h)),
        compiler_params=_cparams(("parallel", "parallel", "arbitrary")), name="attn_" + mode)(*args)


_GQA_PAIRS = ((0, 0, HEAD_DIM), (HEAD_DIM, 0, HEAD_DIM))
_MHA_PAIRS = ((0, 0, HEAD_DIM), (HEAD_DIM, 2 * HEAD_DIM, 3 * HEAD_DIM))


def _slc_paged_body(idx_ref, phys_ref, *refs, pos0, blk0, t_real):
    del phys_ref
    q_ref = refs[0]
    blk_refs = refs[1:1 + N_SELECT]
    new_ref = refs[1 + N_SELECT]
    o_ref = refs[2 + N_SELECT]
    b, h, t = pl.program_id(0), pl.program_id(1), pl.program_id(2)
    base = ((b * NSA_KV_HEADS + h) * t_real + t) * N_SELECT
    qrow = q_ref[0, pl.ds(t, 1), :]
    q2 = jnp.concatenate([qrow[:, :HEAD_DIM], qrow[:, HEAD_DIM:], jnp.zeros((6, HEAD_DIM), F32)], 0)
    pos = pos0 + t
    bpp = PAGE_SIZE // SLC_BLOCK
    tok = lax.broadcasted_iota(jnp.int32, (1, PAGE_SIZE), 1)
    scores, vts = [], []
    n_new = jnp.int32(0)
    for j in range(N_SELECT):
        idx = idx_ref[base + j]
        idc = jnp.maximum(idx, 0)
        kv_t = blk_refs[j][0, 0, 0]
        ok = (idx >= 0) & (idx < blk0) & (tok // SLC_BLOCK == idc % bpp) & ((idc // bpp) * PAGE_SIZE + tok <= pos)
        scores.append(_mm(q2, kv_t[0]) * SCALE + jnp.where(ok, 0.0, -jnp.inf))
        vts.append(kv_t[1])
        n_new = n_new + jnp.where(idx >= blk0, 1, 0)
    newblk = new_ref[0]
    lane = lax.broadcasted_iota(jnp.int32, (1, SLC_BLOCK), 1)
    ok_new = (n_new > 0) & (blk0 * SLC_BLOCK + lane <= pos)
    s_new = _mm_nt(q2, newblk[:, :HEAD_DIM]) * SCALE + jnp.where(ok_new, 0.0, -jnp.inf)
    m = jnp.max(s_new, -1, keepdims=True)
    for s in scores:
        m = jnp.maximum(m, jnp.max(s, -1, keepdims=True))
    m = jnp.where(m == -jnp.inf, 0.0, m)
    e_new = jnp.exp(s_new - m)
    den = jnp.sum(e_new, -1, keepdims=True)
    o = _mm(e_new, newblk[:, HEAD_DIM:])
    for s, vt in zip(scores, vts):
        e = jnp.exp(s - m)
        den = den + jnp.sum(e, -1, keepdims=True)
        o = o + _mm_nt(e, vt)
    o = o / jnp.maximum(den, 1e-30)
    orow = jnp.concatenate([o[0:1], o[1:2]], 1)
    o_ref[0, 0, 0] = jnp.broadcast_to(orow, (8, LANES))


def _slc_paged(q_rot, cache_t, new_rows, idx_flat, page_flat, layer, pos0, blk0, t_real):
    B = q_rot.shape[0]
    tp = q_rot.shape[1]

    def blk_spec(j):
        def imap(b, h, t, idx, page):
            return (layer, page[((b * NSA_KV_HEADS + h) * t_real + t) * N_SELECT + j], h, 0, 0, 0)
        return pl.BlockSpec((1, 1, 1, 2, HEAD_DIM, PAGE_SIZE), imap)

    gs = pltpu.PrefetchScalarGridSpec(
        num_scalar_prefetch=2, grid=(B, NSA_KV_HEADS, t_real),
        in_specs=[pl.BlockSpec((1, tp, LANES), lambda b, h, t, idx, page: (b, 0, h))]
        + [blk_spec(j) for j in range(N_SELECT)]
        + [pl.BlockSpec((1, SLC_BLOCK, LANES), lambda b, h, t, idx, page: (b, 0, h))],
        out_specs=pl.BlockSpec((1, 1, 1, 8, LANES), lambda b, h, t, idx, page: (b, h, t, 0, 0)))
    out = pl.pallas_call(
        functools.partial(_slc_paged_body, pos0=pos0, blk0=blk0, t_real=t_real),
        out_shape=jax.ShapeDtypeStruct((B, NSA_KV_HEADS, t_real, 8, LANES), F32), grid_spec=gs,
        compiler_params=_cparams(("parallel", "parallel", "arbitrary")), name="slc_paged")(
            idx_flat, page_flat, q_rot, *([cache_t] * N_SELECT), new_rows)
    return jnp.transpose(out[:, :, :, 0, :], (0, 2, 1, 3)).reshape(B, t_real, GW)


def _rms(x, g):
    return x * lax.rsqrt(jnp.mean(x * x, -1, keepdims=True) + NORM_EPS) * g


def _rope(x, pos):
    half = HEAD_DIM // 2
    inv = ROPE_THETA ** (-jnp.arange(half, dtype=F32) / half)
    ang = pos.astype(F32)[:, None] * inv
    cos, sin = jnp.cos(ang)[:, None, :], jnp.sin(ang)[:, None, :]
    x1, x2 = x[..., :half], x[..., half:]
    return jnp.concatenate([x1 * cos - x2 * sin, x1 * sin + x2 * cos], -1)


def _pad_t(x, tp, value=0.0):
    t = x.shape[1]
    if t == tp:
        return x
    return jnp.pad(x, ((0, 0), (0, tp - t)) + ((0, 0),) * (x.ndim - 2), constant_values=value)


def _round_up(n, m):
    return -(-n // m) * m


def _cover_matrix(n_cmp, n_cmp_pad, n_slc, nbp):
    start = np.arange(n_cmp_pad)[:, None] * CMP_STRIDE
    blk = np.arange(nbp)[None, :]
    cov = (start < (blk + 1) * SLC_BLOCK) & (start + CMP_BLOCK > blk * SLC_BLOCK)
    cov &= (np.arange(n_cmp_pad)[:, None] < n_cmp) & (blk < n_slc)
    return jnp.asarray(cov.astype(np.float32), dtype=BF16)


def _mlstm_mixer(P, conv_buf, c0, n0, m0, prm):
    B, T, _ = P.shape
    qk_raw = jnp.concatenate([P[..., C_MQ:C_MQ + GW], P[..., C_MK:C_MK + GW]], -1)
    ext = jnp.concatenate([conv_buf, qk_raw], 1)
    w = prm['mlstm_conv_w']
    qk = jax.nn.silu(prm['mlstm_conv_b'] + sum(ext[:, j:j + T] * w[j] for j in range(MLSTM_CONV)))
    gates = P[..., C_MG:C_MG + LANES]
    li = gates[..., 0:4] + prm['mlstm_gate_b'][0]
    lf = jax.nn.log_sigmoid(gates[..., 4:8] + prm['mlstm_gate_b'][1])
    tp = _round_up(T, MLSTM_L)
    g128 = jnp.concatenate([_pad_t(li, tp, NEG_BIG), _pad_t(lf, tp), jnp.zeros((B, tp, LANES - 8), F32)], -1)
    m0p = jnp.pad(m0, ((0, 0), (0, LANES - N_HEADS))).reshape(B, 1, LANES)
    h, C, n, m = _mlstm(_pad_t(qk[..., :GW], tp), _pad_t(qk[..., GW:], tp), _pad_t(P[..., C_MV:C_MV + GW], tp),
                        g128, c0, n0, m0p)
    h = h[:, :T].reshape(B, T, N_HEADS, HEAD_DIM)
    h = _rms(h, prm['mlstm_norm_g'].reshape(N_HEADS, HEAD_DIM)).reshape(B, T, GW)
    out = jax.nn.sigmoid(P[..., C_MO:C_MO + GW]) * h * jax.nn.silu(P[..., C_MZ:C_MZ + GW])
    return out, C, n, m[:, 0, :N_HEADS], ext[:, -(MLSTM_CONV - 1):]


def _rwkv_mixer(P, shift_buf, s0, prm, prec):
    B, T, _ = P.shape
    u = jnp.concatenate([P[..., C_RR:C_RR + 3 * GW], P[..., C_RL:C_RL + LANES]], -1)
    prev = jnp.concatenate([shift_buf, u[:, :-1]], 1)
    um = u + (prev - u) * prm['rwkv_mu']
    r, k, v = um[..., :GW], um[..., GW:2 * GW], um[..., 2 * GW:3 * GW]
    wl, al = um[..., 3 * GW:3 * GW + DECAY_LORA], um[..., 3 * GW + DECAY_LORA:]
    w_pre = prm['rwkv_w0'] + jnp.dot(jnp.tanh(wl), prm['rwkv_w2'])
    log_w = -jnp.exp(-jax.nn.softplus(-w_pre) - 0.5)
    a = jax.nn.sigmoid(prm['rwkv_a0'] + jnp.dot(al, prm['rwkv_a2']))
    k_k, k_a, r_k = prm['rwkv_kk'][0], prm['rwkv_kk'][1], prm['rwkv_kk'][2]
    heads = lambda t: t.reshape(B, T, N_HEADS, HEAD_DIM)
    kk = heads(k * k_k)
    kk = (kk / jnp.maximum(jnp.sqrt(jnp.sum(kk * kk, -1, keepdims=True)), 1e-12)).reshape(B, T, GW)
    k_eff = k * (1.0 + (a - 1.0) * k_a)
    tp = _round_up(T, RWKV_L)
    y, S = _rwkv(*(_pad_t(t, tp) for t in (r, log_w, k_eff, v, kk, a)), s0, prec=prec)
    y = heads(y[:, :T])
    mu = jnp.mean(y, -1, keepdims=True)
    var = jnp.mean(jnp.square(y - mu), -1, keepdims=True)
    ln = prm['rwkv_ln'].reshape(2, N_HEADS, HEAD_DIM)
    y = (y - mu) * lax.rsqrt(var + RWKV_LN_EPS) * ln[0] + ln[1]
    rh, vh = heads(r), heads(v)
    y = y + jnp.sum(rh * heads(k_eff) * r_k.reshape(N_HEADS, HEAD_DIM), -1, keepdims=True) * vh
    out = y.reshape(B, T, GW) * jax.nn.silu(P[..., C_RZ:C_RZ + GW])
    return out, S, u[:, -1:]


def _memory_kv(mem, prm):
    B = mem.shape[0]
    kv = _proj_in(mem.reshape(B * N_MEM, D_MODEL), prm['mem_norm_g'], prm['w_mem_kv_bf16'], 2 * GW)
    kv = kv.reshape(B, N_MEM, MEM_HEADS, 2, HEAD_DIM)
    return jnp.stack([_rms(kv[:, :, :, 0], prm['mem_qk_g'][1]), kv[:, :, :, 1]], 3)


def _memory_mixer(P, mem_kv, prm):
    B, T, _ = P.shape
    tp = _round_up(T, 8)
    qn = _rms(P[..., C_CQ:C_CQ + GW].reshape(B, T, MEM_HEADS, HEAD_DIM), prm['mem_qk_g'][0]).reshape(B, T, GW)
    tq = min(256, tp)
    o = _attn(_pad_t(qn, tp), mem_kv.reshape(B, N_MEM, 2 * GW), 'none', _MHA_PAIRS, 2 * LANES, 0, 0, tq, N_MEM)
    return o[:, :T] * jax.nn.silu(P[..., C_CZ:C_CZ + GW])


def _nsa_mixer(P2d, B, T, pos0, win_prefix, prm, past):
    P = P2d.reshape(B, T, DP)
    pos = pos0 + jnp.arange(T, dtype=jnp.int32)
    g = prm['nsa_qk_g']
    tp = _round_up(T, 8)
    qn = _rms(P[..., C_NQ:C_NQ + GW].reshape(B, T, N_HEADS, HEAD_DIM), g[0])
    q_rot = _pad_t(_rope(qn, pos).reshape(B, T, GW), tp)
    qn = _pad_t(qn.reshape(B, T, GW), tp)
    kvrows = lambda c0: P[..., c0:c0 + 2 * KVW].reshape(B, T, NSA_KV_HEADS, 2, HEAD_DIM)
    cmp_new = kvrows(C_NCMP)
    slc_raw, win_raw = kvrows(C_NSLC), kvrows(C_NWIN)
    slc_new = jnp.stack([_rope(_rms(slc_raw[:, :, :, 0], g[2]), pos), slc_raw[:, :, :, 1]], 3)
    win_new = jnp.stack([_rope(_rms(win_raw[:, :, :, 0], g[3]), pos), win_raw[:, :, :, 1]], 3)

    L_all = pos0 + T
    n_sub = max(-(-L_all // CMP_STRIDE), CMP_BLOCK // CMP_STRIDE)
    n_cmp = n_sub - 1
    bd, w2bd, pe_hid = prm['cmp_bd'], prm['cmp_w2bd'], prm['cmp_pe_hid']
    if past is None:
        G = _subproj(P2d, C_NCMP // LANES, bd).reshape(B, T // CMP_STRIDE, 4 * LANES)
    else:
        cache_cmp_t, cache_slc_t, page_table, layer = past
        g_pages = _subproj_pages(cache_cmp_t, page_table, layer, bd)
        new_rows = _pad_t(cmp_new.reshape(B, T, 2 * KVW), CMP_STRIDE).reshape(B * CMP_STRIDE, 2 * KVW)
        g_new = _subproj(new_rows, 0, bd).reshape(B, 1, 4 * LANES)
        G = jnp.concatenate([g_pages, g_new], 1)
    n_cmp_pad = _round_up(n_cmp, LANES)
    gb = G[:, 1:, 256:]
    ga = G[:, :, :256]
    fit = lambda t: _pad_t(t, max(n_cmp_pad, t.shape[1]))[:, :n_cmp_pad]
    hid = fit(ga) + fit(gb) + pe_hid
    kv_cmp = _cmp_mlp(hid, w2bd, g[1])
    n_slc = -(-L_all // SLC_BLOCK)
    nbp = _round_up(n_slc, LANES)
    cover = _cover_matrix(n_cmp, n_cmp_pad, n_slc, nbp)
    tq = min(256, tp)
    o_cmp, imp = _cmp_attn(qn, kv_cmp, cover, pos0, tq)

    R = B * NSA_KV_HEADS * tp
    rp = _round_up(R, LANES)
    imp2d = jnp.pad(imp.reshape(R, nbp), ((0, rp - R), (0, 0)))
    sel, idx_t = _topk(imp2d, pos0, tp, n_slc)

    if past is None:
        sel4 = sel[:R].reshape(B, NSA_KV_HEADS, tp, nbp)
        o_slc = _attn(q_rot, slc_new.reshape(B, T, 2 * KVW), 'slc', _GQA_PAIRS, LANES, pos0, pos0, tq,
                      min(256, T), sel=sel4)
    else:
        bpp = PAGE_SIZE // SLC_BLOCK
        idx = idx_t[:, :R].T.reshape(B, NSA_KV_HEADS, tp, N_SELECT)[:, :, :T]
        idc = jnp.clip(idx, 0, page_table.shape[1] * bpp - 1)
        page = page_table[jnp.arange(B)[:, None, None, None], idc // bpp]
        new_rows = _pad_t(slc_new.reshape(B, T, 2 * KVW), SLC_BLOCK)
        o_slc = _slc_paged(q_rot, cache_slc_t, new_rows, idx.reshape(-1), page.reshape(-1).astype(jnp.int32),
                           layer, pos0, pos0 // SLC_BLOCK, T)

    if win_prefix.shape[1] == 0:
        win_ctx = win_new
        pos_k0 = pos0
    else:
        win_ctx = jnp.concatenate([win_prefix, win_new], 1)
        pos_k0 = pos0 - win_prefix.shape[1]
    n_k = win_ctx.shape[1]
    tkw = min(256, _round_up(n_k, LANES))
    kv_win = _pad_t(win_ctx.reshape(B, n_k, 2 * KVW), _round_up(n_k, tkw))
    o_win = _attn(q_rot, kv_win, 'win', _GQA_PAIRS, LANES, pos0, pos_k0, tq, tkw)

    gt = jax.nn.sigmoid(P[..., C_NG:C_NG + 3 * N_HEADS]).reshape(B, T, 3, N_HEADS, 1)
    hd = lambda o: o[:, :T].reshape(B, T, N_HEADS, HEAD_DIM)
    o = gt[:, :, 0] * hd(o_cmp) + gt[:, :, 1] * hd(o_slc) + gt[:, :, 2] * hd(o_win)
    keep = win_prefix.shape[1] if past is not None else min(WINDOW, T)
    out = o.reshape(B, T, GW) * jax.nn.silu(P[..., C_NZ:C_NZ + GW])
    return out, cmp_new, slc_new, win_ctx[:, -keep:]


RWKV_PREC = ('r3', 'bf16', 'bf16')


def _layer(x, pos0, st, mem_kv, prm, past, rwkv_prec=RWKV_PREC):
    conv_buf, c0, n0, m0, s0, shift_buf, win_prefix = st
    B, T, _ = x.shape
    x2d = x.reshape(B * T, D_MODEL)
    P2d = _proj_in(x2d, prm['norm_g'], prm['w_in_bf16'], 640)
    P = P2d.reshape(B, T, DP)
    y_m, C, n, m, conv_new = _mlstm_mixer(P, conv_buf, c0, n0, m0, prm)
    y_n, cmp_new, slc_new, win_new = _nsa_mixer(P2d, B, T, pos0, win_prefix, prm, past)
    y_r, S, shift_new = _rwkv_mixer(P, shift_buf, s0, prm, rwkv_prec)
    y_c = _memory_mixer(P, mem_kv, prm)
    ys = [t.reshape(B * T, GW) for t in (y_m, y_n, y_r, y_c)]
    out = _proj_out(x2d, ys, prm['w_out_bf16']).reshape(B, T, D_MODEL)
    return out, (cmp_new, slc_new, win_new, C, n, m, conv_new, S, shift_new)


def _prep_params(l, p):
    prm = {k: v[l] for k, v in p.items()}
    src = jnp.asarray(np.maximum(_SRC, 0), jnp.int32)
    keep = jnp.asarray((_SRC >= 0).astype(np.float32))
    prm['w_in_bf16'] = (jnp.take(prm['w_in'], src, axis=1) * keep).astype(BF16)
    prm['w_out_bf16'] = prm['w_out'].astype(BF16)
    wm = prm['w_mem_kv'].reshape(D_MODEL, 2, MEM_HEADS, HEAD_DIM)
    prm['w_mem_kv_bf16'] = jnp.transpose(wm, (0, 2, 1, 3)).reshape(D_MODEL, 2 * GW).astype(BF16)
    w1 = prm['nsa_cmp_w1']
    eye_h = jnp.eye(NSA_KV_HEADS, dtype=F32)
    eye_c = jnp.eye(2, dtype=F32)
    w1r = w1.reshape(2, 2, CMP_STRIDE, HEAD_DIM, HEAD_DIM)
    bd = jnp.einsum('cC,crsde->scdrCe', eye_c, w1r)
    prm['cmp_bd'] = bd.reshape(CMP_STRIDE, LANES, 2 * LANES).astype(BF16)
    w2 = prm['nsa_cmp_w2']
    prm['cmp_w2bd'] = jnp.einsum('hH,cC,ced->hceHCd', eye_h, eye_c, w2).reshape(2 * KVW, 2 * KVW)
    pe_hid = jnp.einsum('csd,csde->ce', prm['nsa_pe'], w1, precision=HI)
    prm['cmp_pe_hid'] = jnp.tile(pe_hid.reshape(1, 2 * HEAD_DIM), (1, NSA_KV_HEADS)).reshape(2 * KVW)
    return prm


def kernel(x_prompt, x_sample, cache_cmp_kv, cache_slc_kv, cache_win_kv, cache_mem_kv, state_mlstm_C, state_mlstm_n, state_mlstm_m, state_mlstm_conv, state_rwkv_S, state_rwkv_shift, page_table, mem_prompt, norm_g, w_in, w_out, mlstm_conv_w, mlstm_conv_b, mlstm_gate_b, mlstm_norm_g, nsa_qk_g, nsa_pe, nsa_cmp_w1, nsa_cmp_w2, rwkv_mu, rwkv_w0, rwkv_w2, rwkv_a0, rwkv_a2, rwkv_kk, rwkv_ln, mem_norm_g, w_mem_kv, mem_qk_g):
    params = dict(norm_g=norm_g, w_in=w_in, w_out=w_out, mlstm_conv_w=mlstm_conv_w, mlstm_conv_b=mlstm_conv_b,
                  mlstm_gate_b=mlstm_gate_b, mlstm_norm_g=mlstm_norm_g, nsa_qk_g=nsa_qk_g, nsa_pe=nsa_pe,
                  nsa_cmp_w1=nsa_cmp_w1, nsa_cmp_w2=nsa_cmp_w2, rwkv_mu=rwkv_mu, rwkv_w0=rwkv_w0, rwkv_w2=rwkv_w2,
                  rwkv_a0=rwkv_a0, rwkv_a2=rwkv_a2, rwkv_kk=rwkv_kk, rwkv_ln=rwkv_ln, mem_norm_g=mem_norm_g,
                  w_mem_kv=w_mem_kv, mem_qk_g=mem_qk_g)
    depth = norm_g.shape[0]
    B = x_prompt.shape[0]
    past_len = page_table.shape[1] * PAGE_SIZE
    cache_cmp_t = jnp.transpose(cache_cmp_kv, (0, 1, 3, 4, 5, 2))
    cache_slc_t = jnp.transpose(cache_slc_kv, (0, 1, 3, 4, 5, 2))
    xp, xs = x_prompt, x_sample
    new_p, new_s, new_mem = [], [], []
    for l in range(depth):
        prm = _prep_params(l, params)
        mem_kv_p = _memory_kv(mem_prompt, prm)
        st_p = (jnp.zeros((B, MLSTM_CONV - 1, 2 * GW), F32),
                jnp.zeros((B, N_HEADS, HEAD_DIM, HEAD_DIM), F32),
                jnp.zeros((B, N_HEADS, HEAD_DIM), F32),
                jnp.full((B, N_HEADS), M_INIT, F32),
                jnp.zeros((B, N_HEADS, HEAD_DIM, HEAD_DIM), F32),
                jnp.zeros((B, 1, RWKV_SHIFT), F32),
                jnp.zeros((B, 0, NSA_KV_HEADS, 2, HEAD_DIM), F32))
        xp, sp = _layer(xp, 0, st_p, mem_kv_p, prm, None)
        st_s = (state_mlstm_conv[l], state_mlstm_C[l], state_mlstm_n[l], state_mlstm_m[l],
                state_rwkv_S[l], state_rwkv_shift[l], cache_win_kv[l])
        xs, ss = _layer(xs, past_len, st_s, cache_mem_kv[l], prm, (cache_cmp_t, cache_slc_t, page_table, l))
        new_p.append(sp)
        new_s.append(ss)
        new_mem.append(mem_kv_p)
    stack = lambda states, i: jnp.stack([s[i] for s in states])
    outs = [xp, xs]
    for i in range(3):
        outs += [stack(new_p, i), stack(new_s, i)]
    outs.append(jnp.stack(new_mem))
    for i in range(3, 9):
        outs += [stack(new_p, i), stack(new_s, i)]
    return tuple(outs)
```

```python
import functools

import numpy as np
import jax
import jax.numpy as jnp
from jax import lax
from jax.experimental import pallas as pl
from jax.experimental.pallas import tpu as pltpu

F32 = jnp.float32
BF16 = jnp.bfloat16
HI = lax.Precision.HIGHEST

D_MODEL = 1024
PAGE_SIZE = 128
HEAD_DIM = 64
GW = D_MODEL // 4
N_HEADS = GW // HEAD_DIM
SCALE = HEAD_DIM ** -0.5
MLSTM_CONV = 4
M_INIT = -1e30
NSA_KV_HEADS = 2
KVW = NSA_KV_HEADS * HEAD_DIM
CMP_BLOCK = 32
CMP_STRIDE = 16
SLC_BLOCK = 64
N_SELECT = 16
WINDOW = 512
DECAY_LORA = 64
AAA_LORA = 64
RWKV_SHIFT = 3 * GW + DECAY_LORA + AAA_LORA
RWKV_LN_EPS = HEAD_DIM * 1e-5
N_MEM = 256
MEM_HEADS = 4
ROPE_THETA = 10000.0
NORM_EPS = 1e-6

LANES = 128
MLSTM_L = 128
RWKV_L = 64
NEG_BIG = -1e30

_M0, _N0, _R0, _C0 = 0, 1288, 2580, 3732
C_MQ, C_MK, C_MV, C_MO, C_MZ = 0, 256, 512, 768, 1024
C_NQ, C_NCMP, C_NSLC, C_NWIN, C_NZ = 1280, 1536, 1792, 2048, 2304
C_RR, C_RK, C_RV, C_RZ = 2560, 2816, 3072, 3328
C_CQ, C_CZ = 3584, 3840
C_RL = 4096
C_MG = 4224
C_NG = 4352
DP = 4480


def _packed_src():
    src = -np.ones((DP,), np.int64)

    def put(dst, lo, n):
        src[dst:dst + n] = np.arange(lo, lo + n)

    put(C_MQ, _M0, 256); put(C_MK, _M0 + 256, 256); put(C_MV, _M0 + 512, 256)
    put(C_MG, _M0 + 768, 8); put(C_MO, _M0 + 776, 256); put(C_MZ, _M0 + 1032, 256)
    put(C_NQ, _N0, 256)
    for i, base in enumerate((C_NCMP, C_NSLC, C_NWIN)):
        ksrc = _N0 + 256 + 256 * i
        vsrc = ksrc + 128
        for h in range(2):
            put(base + 128 * h, ksrc + 64 * h, 64)
            put(base + 128 * h + 64, vsrc + 64 * h, 64)
    put(C_NG, _N0 + 1024, 12); put(C_NZ, _N0 + 1036, 256)
    put(C_RR, _R0, 256); put(C_RK, _R0 + 256, 256); put(C_RV, _R0 + 512, 256)
    put(C_RL, _R0 + 768, 128); put(C_RZ, _R0 + 896, 256)
    put(C_CQ, _C0, 256); put(C_CZ, _C0 + 256, 256)
    return src


_SRC = _packed_src()


def _split2(a):
    hi = a.astype(BF16)
    return hi, (a - hi.astype(F32)).astype(BF16)


def _dg(a, b, dims, prec):
    dn = (dims, ((), ()))
    if prec == 'bf16':
        return lax.dot_general(a.astype(BF16), b.astype(BF16), dn, preferred_element_type=F32)
    d = lambda x, y: lax.dot_general(x, y, dn, preferred_element_type=F32)
    if prec == 'x3':
        ah, al = _split2(a)
        bh, bl = _split2(b)
        return d(ah, bh) + (d(ah, bl) + d(al, bh))
    if prec in ('r3', 'l3'):
        exact, other = (a, b) if prec == 'r3' else (b, a)
        o1, rest = other.astype(BF16), None
        rest = other - o1.astype(F32)
        o2 = rest.astype(BF16)
        o3 = (rest - o2.astype(F32)).astype(BF16)
        e = exact.astype(BF16)
        if prec == 'r3':
            return d(e, o1) + (d(e, o2) + d(e, o3))
        return d(o1, e) + (d(o2, e) + d(o3, e))
    return lax.dot_general(a, b, dn, preferred_element_type=F32, precision=prec)


def _mm(a, b, prec=None):
    return _dg(a, b, ((1,), (0,)), prec)


def _mm_nt(a, b, prec=None):
    return _dg(a, b, ((1,), (1,)), prec)


def _mm_tn(a, b, prec=None):
    return _dg(a, b, ((0,), (0,)), prec)


def _run_interleaved(chains):
    chains = list(chains)
    while chains:
        alive = []
        for ch in chains:
            try:
                next(ch)
                alive.append(ch)
            except StopIteration:
                pass
        chains = alive


def _cparams(sem, vmem_mb=None):
    kw = dict(dimension_semantics=sem)
    if vmem_mb is not None:
        kw['vmem_limit_bytes'] = vmem_mb * 1024 * 1024
    return pltpu.CompilerParams(**kw)


def _proj_in_body(x_ref, g_ref, w_ref, o_ref):
    x = x_ref[...]
    h = x * lax.rsqrt(jnp.mean(x * x, -1, keepdims=True) + NORM_EPS) * g_ref[...]
    o_ref[...] = jnp.dot(h.astype(BF16), w_ref[...], preferred_element_type=F32)


def _proj_in(x2d, g, w_bf16, tn):
    n, d = x2d.shape
    dn = w_bf16.shape[1]
    tm = min(1024, n)
    return pl.pallas_call(
        _proj_in_body, out_shape=jax.ShapeDtypeStruct((n, dn), F32), grid=(n // tm, dn // tn),
        in_specs=[pl.BlockSpec((tm, d), lambda i, j: (i, 0)),
                  pl.BlockSpec((1, d), lambda i, j: (0, 0)),
                  pl.BlockSpec((d, tn), lambda i, j: (0, j))],
        out_specs=pl.BlockSpec((tm, tn), lambda i, j: (i, j)),
        compiler_params=_cparams(("parallel", "arbitrary"), 48), name="proj_in")(x2d, g.reshape(1, d), w_bf16)


def _proj_out_body(x_ref, y0, y1, y2, y3, w_ref, o_ref):
    acc = x_ref[...]
    for i, y in enumerate((y0, y1, y2, y3)):
        acc = acc + jnp.dot(y[...].astype(BF16), w_ref[i * GW:(i + 1) * GW, :], preferred_element_type=F32)
    o_ref[...] = acc


def _proj_out(x2d, ys, w_bf16):
    n, d = x2d.shape
    tm = min(512, n)
    yspec = pl.BlockSpec((tm, GW), lambda i: (i, 0))
    return pl.pallas_call(
        _proj_out_body, out_shape=jax.ShapeDtypeStruct((n, d), F32), grid=(n // tm,),
        in_specs=[pl.BlockSpec((tm, d), lambda i: (i, 0)), yspec, yspec, yspec, yspec,
                  pl.BlockSpec((d, d), lambda i: (0, 0))],
        out_specs=pl.BlockSpec((tm, d), lambda i: (i, 0)),
        compiler_params=_cparams(("parallel",), 48), name="proj_out")(x2d, *ys, w_bf16)


_MP_CONV_Q, _MP_CONV_K, _MP_BIAS_Q, _MP_BIAS_K, _MP_NORM, _MP_GATE_B = 0, 4, 8, 9, 10, 11


def _silu(x):
    return x * jax.nn.sigmoid(x)


def _mlstm_body(q_ref, k_ref, v_ref, o_ref, z_ref, g_ref, tq_ref, tk_ref, prm_ref, c0_ref, n0_ref, m0_ref,
                h_ref, c_ref, n_ref, m_ref, c_scr, n_scr, m_scr, pq_scr, pk_scr, *, t_valid):
    c = pl.program_id(1)
    nb, L = q_ref.shape[0], q_ref.shape[1]

    @pl.when(c == 0)
    def _():
        c_scr[...] = c0_ref[...]
        n_scr[...] = n0_ref[...]
        m_scr[...] = m0_ref[...]
        pq_scr[...] = jnp.zeros(pq_scr.shape, F32)
        pk_scr[...] = jnp.zeros(pk_scr.shape, F32)
        pq_scr[:, L - 8:L, :] = tq_ref[...]
        pk_scr[:, L - 8:L, :] = tk_ref[...]

    row = lax.broadcasted_iota(jnp.int32, (L, L), 0)
    col = lax.broadcasted_iota(jnp.int32, (L, L), 1)
    causal = row >= col
    tril = causal.astype(F32)
    triu = (row <= col).astype(F32)
    trow = lax.broadcasted_iota(jnp.int32, (L, 1), 0)
    valid = (c * L + trow) < t_valid
    lane = lax.broadcasted_iota(jnp.int32, (1, LANES), 1)
    prm = prm_ref[...]

    def conv(x, prev, w0, b):
        acc = prm[b:b + 1, :] + prm[w0 + MLSTM_CONV - 1:w0 + MLSTM_CONV, :] * x
        for s in range(1, MLSTM_CONV):
            shifted = jnp.where(trow >= s, pltpu.roll(x, s, 0), pltpu.roll(prev, s, 0))
            acc = acc + prm[w0 + MLSTM_CONV - 1 - s:w0 + MLSTM_CONV - s, :] * shifted
        return _silu(acc)

    cums, qs, ks = [], [], []
    for gi in range(nb):
        q_raw, k_raw = q_ref[gi], k_ref[gi]
        qs.append(conv(q_raw, pq_scr[gi], _MP_CONV_Q, _MP_BIAS_Q) * (HEAD_DIM ** -0.5))
        ks.append(conv(k_raw, pk_scr[gi], _MP_CONV_K, _MP_BIAS_K))
        pq_scr[gi] = q_raw
        pk_scr[gi] = k_raw
        x = g_ref[gi] + prm[_MP_GATE_B:_MP_GATE_B + 1, :LANES]
        log_f = jnp.minimum(x, 0.0) - jnp.log1p(jnp.exp(-jnp.abs(x)))
        g = jnp.where(lane < N_HEADS, x, jnp.where(lane < 2 * N_HEADS, log_f, 0.0))
        g = jnp.where(valid, g, jnp.where(lane < N_HEADS, NEG_BIG, 0.0))
        gt = g.T
        cums.append((g, gt, _mm(tril, g, 'r3'), _mm(gt, triu, 'l3')))

    def chain(gi, h):
        g, gt, bc, br = cums[gi]
        sl = slice(h * HEAD_DIM, (h + 1) * HEAD_DIM)
        qh = qs[gi][:, sl]
        kh = ks[gi][:, sl]
        vh = v_ref[gi, :, sl]
        qk = _mm_nt(qh, kh)
        ch = c_scr[gi, h]
        qc = _mm(qh, ch)
        yield
        b_col = bc[:, 4 + h:5 + h]
        li_col = g[:, h:h + 1]
        b_row = br[4 + h:5 + h, :]
        li_row = gt[h:h + 1, :]
        m_prev = m_scr[gi, :, h:h + 1]
        log_d = jnp.where(causal, b_col - b_row + li_row, -jnp.inf)
        log_inter = b_col + m_prev
        m_t = jnp.maximum(jnp.max(log_d, -1, keepdims=True), log_inter)
        s = qk * jnp.exp(log_d - m_t)
        w_inter = jnp.exp(log_inter - m_t)
        nh = n_scr[gi, h:h + 1, :]
        sv = _mm(s, vh)
        yield
        num = sv + w_inter * qc
        den = jnp.sum(s, -1, keepdims=True) + w_inter * jnp.sum(qh * nh, -1, keepdims=True)
        hh = num / jnp.maximum(jnp.abs(den), jnp.exp(-m_t))
        hn = hh * lax.rsqrt(jnp.mean(hh * hh, -1, keepdims=True) + NORM_EPS) * prm[_MP_NORM:_MP_NORM + 1, sl]
        h_ref[gi, :, sl] = jax.nn.sigmoid(o_ref[gi, :, sl]) * hn * _silu(z_ref[gi, :, sl])
        b_end = b_col[L - 1:L, :]
        log_w = b_end - b_col + li_col
        m_new = jnp.maximum(b_end + m_prev, jnp.max(log_w, 0, keepdims=True))
        wk = jnp.exp(log_w - m_new)
        decay = jnp.exp(b_end + m_prev - m_new)
        kw = kh * wk
        c_scr[gi, h] = decay * ch + _mm_tn(kw, vh)
        n_scr[gi, h:h + 1, :] = decay * nh + jnp.sum(kw, 0, keepdims=True)
        m_scr[gi, :, h:h + 1] = m_new

    _run_interleaved([chain(gi, h) for gi in range(nb) for h in range(N_HEADS)])

    @pl.when(c == pl.num_programs(1) - 1)
    def _():
        c_ref[...] = c_scr[...]
        n_ref[...] = n_scr[...]
        m_ref[...] = m_scr[...]


BATCH_ROWS = 4


def _mlstm(P, t_valid, tail_q, tail_k, prm_blk, c0, n0, m0):
    B, tp, _ = P.shape
    L = MLSTM_L
    nb = BATCH_ROWS if B % BATCH_ROWS == 0 else 1
    col = lambda c0_: pl.BlockSpec((nb, L, GW), lambda b, c: (b, c, c0_ // GW))
    tok = pl.BlockSpec((nb, L, GW), lambda b, c: (b, c, 0))
    tail = pl.BlockSpec((nb, 8, GW), lambda b, c: (b, 0, 0))
    sc = pl.BlockSpec((nb, N_HEADS, HEAD_DIM, HEAD_DIM), lambda b, c: (b, 0, 0, 0))
    sn = pl.BlockSpec((nb, N_HEADS, HEAD_DIM), lambda b, c: (b, 0, 0))
    sm = pl.BlockSpec((nb, 1, LANES), lambda b, c: (b, 0, 0))
    return pl.pallas_call(
        functools.partial(_mlstm_body, t_valid=t_valid),
        out_shape=(jax.ShapeDtypeStruct((B, tp, GW), F32), jax.ShapeDtypeStruct(c0.shape, F32),
                   jax.ShapeDtypeStruct(n0.shape, F32), jax.ShapeDtypeStruct(m0.shape, F32)),
        grid=(B // nb, -(-t_valid // L)),
        in_specs=[col(C_MQ), col(C_MK), col(C_MV), col(C_MO), col(C_MZ),
                  pl.BlockSpec((nb, L, LANES), lambda b, c: (b, c, C_MG // LANES)), tail, tail,
                  pl.BlockSpec((16, GW), lambda b, c: (0, 0)), sc, sn, sm],
        out_specs=(tok, sc, sn, sm),
        scratch_shapes=[pltpu.VMEM((nb, N_HEADS, HEAD_DIM, HEAD_DIM), F32), pltpu.VMEM((nb, N_HEADS, HEAD_DIM), F32),
                        pltpu.VMEM((nb, 1, LANES), F32), pltpu.VMEM((nb, L, GW), F32), pltpu.VMEM((nb, L, GW), F32)],
        compiler_params=_cparams(("parallel", "arbitrary")), name="mlstm")(
            P, P, P, P, P, P, tail_q, tail_k, prm_blk, c0, n0, m0)


(_RP_MU_R, _RP_MU_K, _RP_MU_V, _RP_MU_L, _RP_W0, _RP_A0, _RP_KK, _RP_KA, _RP_RK, _RP_LN_G, _RP_LN_B) = range(11)


def _rwkv_body(r_ref, k_ref, v_ref, z_ref, l_ref, sr_ref, sk_ref, sv_ref, sl_ref, prm_ref, lora_ref, s0_ref,
               y_ref, s_ref, s_scr, cr_scr, ck_scr, cv_scr, cl_scr, *, prec, t_valid):
    c_id = pl.program_id(1)

    @pl.when(c_id == 0)
    def _():
        s_scr[...] = s0_ref[...]
        cr_scr[...] = sr_ref[...]
        ck_scr[...] = sk_ref[...]
        cv_scr[...] = sv_ref[...]
        cl_scr[...] = sl_ref[...]

    nb, L = r_ref.shape[0], r_ref.shape[1]
    D = HEAD_DIM
    row = lax.broadcasted_iota(jnp.int32, (L, L), 0)
    col = lax.broadcasted_iota(jnp.int32, (L, L), 1)
    lower = row >= col
    strict = row > col
    tril = lower.astype(F32)
    n_sq = int(np.log2(L)) - 1
    pc, pa, prec = prec
    trow = lax.broadcasted_iota(jnp.int32, (L, 1), 0)
    valid = (c_id * L + trow) < t_valid
    prm = prm_ref[...]
    prow = lambda i, n=GW: prm[i:i + 1, :n]

    def shifted_mix(x_ref, carry_scr, g, mu):
        x = x_ref[g]
        prev = jnp.where(trow >= 1, pltpu.roll(x, 1, 0), carry_scr[g])
        carry_scr[g] = x[L - 1:L, :]
        return x + (prev - x) * mu

    prep = []
    for g in range(nb):
        r = shifted_mix(r_ref, cr_scr, g, prow(_RP_MU_R))
        k = shifted_mix(k_ref, ck_scr, g, prow(_RP_MU_K))
        v = shifted_mix(v_ref, cv_scr, g, prow(_RP_MU_V))
        lo = shifted_mix(l_ref, cl_scr, g, prow(_RP_MU_L, LANES))
        lo_in = jnp.where(lax.broadcasted_iota(jnp.int32, (1, LANES), 1) < DECAY_LORA, jnp.tanh(lo), lo)
        lora = _mm(lo_in, lora_ref[...])
        w_pre = prow(_RP_W0) + lora[:, :GW]
        log_w = -(float(np.exp(-0.5)) * jax.nn.sigmoid(w_pre))
        a = jax.nn.sigmoid(prow(_RP_A0) + lora[:, GW:])
        k_eff = k * (1.0 + (a - 1.0) * prow(_RP_KA))
        kk_raw = k * prow(_RP_KK)
        log_w = jnp.where(valid, log_w, 0.0)
        k_eff = jnp.where(valid, k_eff, 0.0)
        kk_raw = jnp.where(valid, kk_raw, 0.0)
        v = jnp.where(valid, v, 0.0)
        prep.append((r, log_w, k_eff, v, kk_raw, a))

    def chain(g, h):
        sl = slice(h * D, (h + 1) * D)
        r, w, k, v, kk, a = (x[:, sl] for x in prep[g])
        kk = kk / jnp.maximum(jnp.sqrt(jnp.sum(kk * kk, -1, keepdims=True)), 1e-12)
        cum = _mm(tril, w, pc)
        yield
        c_last = cum[L - 1:L, :]
        e_neg = jnp.exp(-cum)
        kh = kk * jnp.exp(cum - w)
        bt = kk * a * e_neg
        kt = k * e_neg
        rh = r * jnp.exp(cum)
        gram = _mm_nt(jnp.concatenate([kh, rh], 0), jnp.concatenate([bt, kt], 0), pa)
        yield
        A = jnp.where(strict, gram[:L, :L], 0.0)
        Bm = jnp.where(strict, gram[:L, L:], 0.0)
        Mb = jnp.where(lower, gram[L:, :L], 0.0)
        Mk = jnp.where(lower, gram[L:, L:], 0.0)
        X = jnp.concatenate([kh, _mm(Bm, v, pa)], 1)
        yield
        Pw = A
        X = X - _mm(Pw, X, pa)
        yield
        for _ in range(n_sq):
            Pw = _mm(Pw, Pw, pa)
            yield
            X = X + _mm(Pw, X, pa)
            yield
        e_end = jnp.exp(c_last - cum)
        bp = kk * a * e_end
        kp = k * e_end
        xtb = _mm_tn(X, bp, prec)
        yield
        wtb = xtb[:D]
        N = _mm_tn(v, kp, prec) - xtb[D:]
        yield
        mbx = _mm(Mb, X, prec)
        yield
        qp = rh - mbx[:, :D]
        y0 = _mm(Mk, v, prec) - mbx[:, D:]
        yield
        s0 = s_scr[g, h]
        y = _mm_nt(qp, s0, prec) + y0
        yield
        s_scr[g, h] = s0 * jnp.exp(c_last) - _mm(s0, wtb, prec) + N
        mu = jnp.mean(y, -1, keepdims=True)
        var = jnp.mean(jnp.square(y - mu), -1, keepdims=True)
        y = (y - mu) * lax.rsqrt(var + RWKV_LN_EPS) * prm[_RP_LN_G:_RP_LN_G + 1, sl] + prm[_RP_LN_B:_RP_LN_B + 1, sl]
        y = y + jnp.sum(r * k * prm[_RP_RK:_RP_RK + 1, sl], -1, keepdims=True) * v
        y_ref[g, :, sl] = y * _silu(z_ref[g, :, sl])

    _run_interleaved([chain(g, h) for g in range(nb) for h in range(N_HEADS)])

    @pl.when(c_id == pl.num_programs(1) - 1)
    def _():
        s_ref[...] = s_scr[...]


def _rwkv(P, t_valid, shift_buf, prm_blk, lora_w, s0, prec):
    B, tp, _ = P.shape
    L = RWKV_L
    nb = BATCH_ROWS if B % BATCH_ROWS == 0 else 1
    col = lambda c0_: pl.BlockSpec((nb, L, GW), lambda b, c: (b, c, c0_ // GW))
    tok = pl.BlockSpec((nb, L, GW), lambda b, c: (b, c, 0))
    car = lambda w: pl.BlockSpec((nb, 1, w), lambda b, c: (b, 0, 0))
    st = pl.BlockSpec((nb, N_HEADS, HEAD_DIM, HEAD_DIM), lambda b, c: (b, 0, 0, 0))
    shifts = [shift_buf[..., i * GW:(i + 1) * GW] for i in range(3)] + [shift_buf[..., 3 * GW:]]
    return pl.pallas_call(
        functools.partial(_rwkv_body, prec=prec, t_valid=t_valid),
        out_shape=(jax.ShapeDtypeStruct((B, tp, GW), F32), jax.ShapeDtypeStruct(s0.shape, F32)),
        grid=(B // nb, -(-t_valid // L)),
        in_specs=[col(C_RR), col(C_RK), col(C_RV), col(C_RZ),
                  pl.BlockSpec((nb, L, LANES), lambda b, c: (b, c, C_RL // LANES)),
                  car(GW), car(GW), car(GW), car(LANES),
                  pl.BlockSpec((16, GW), lambda b, c: (0, 0)), pl.BlockSpec((LANES, 2 * GW), lambda b, c: (0, 0)), st],
        out_specs=(tok, st),
        scratch_shapes=[pltpu.VMEM((nb, N_HEADS, HEAD_DIM, HEAD_DIM), F32), pltpu.VMEM((nb, 1, GW), F32),
                        pltpu.VMEM((nb, 1, GW), F32), pltpu.VMEM((nb, 1, GW), F32), pltpu.VMEM((nb, 1, LANES), F32)],
        compiler_params=_cparams(("parallel", "arbitrary")), name="rwkv")(
            P, P, P, P, P, *shifts, prm_blk, lora_w, s0)


def _subproj_accumulate(load_rows, w_ref, o_ref):
    n = o_ref.shape[-2]
    accs = []
    for h in range(NSA_KV_HEADS):
        acc = jnp.zeros((n, 2 * LANES), F32)
        for s in range(CMP_STRIDE):
            acc = acc + jnp.dot(load_rows(h, s, n).astype(BF16), w_ref[s], preferred_element_type=F32)
        accs.append(acc)
    out = jnp.concatenate([accs[0][:, :LANES], accs[1][:, :LANES], accs[0][:, LANES:], accs[1][:, LANES:]], 1)
    o_ref[...] = out.reshape(o_ref.shape)


def _subproj_body(x0_ref, x1_ref, w_ref, o_ref):
    xs = (x0_ref, x1_ref)
    _subproj_accumulate(lambda h, s, n: xs[h][pl.ds(s, n, stride=CMP_STRIDE), :], w_ref, o_ref)


def _subproj(rows2d, col0, w):
    n = rows2d.shape[0]
    tm = min(2048, n)
    return pl.pallas_call(
        _subproj_body, out_shape=jax.ShapeDtypeStruct((n // CMP_STRIDE, 4 * LANES), F32), grid=(n // tm,),
        in_specs=[pl.BlockSpec((tm, LANES), lambda i: (i, col0)), pl.BlockSpec((tm, LANES), lambda i: (i, col0 + 1)),
                  pl.BlockSpec((CMP_STRIDE, LANES, 2 * LANES), lambda i: (0, 0, 0))],
        out_specs=pl.BlockSpec((tm // CMP_STRIDE, 4 * LANES), lambda i: (i, 0)),
        compiler_params=_cparams(("parallel",)), name="cmp_subproj")(rows2d, rows2d, w)


_PAGES_PER_STEP = 32


def _subproj_pages_body(pt_ref, *refs):
    del pt_ref
    npg = len(refs) - 3
    w_ref, o_ref, rows_scr = refs[npg], refs[npg + 1], refs[npg + 2]
    for p in range(npg):
        for h in range(NSA_KV_HEADS):
            rows_scr[h, p * PAGE_SIZE:(p + 1) * PAGE_SIZE, :] = refs[p][0, 0, h].reshape(2 * HEAD_DIM, PAGE_SIZE).T
    _subproj_accumulate(lambda h, s, n: rows_scr[h, pl.ds(s, n, stride=CMP_STRIDE), :], w_ref, o_ref)


def _subproj_pages(cache_t, page_table, layer, w):
    B, n_pages = page_table.shape
    npg = min(_PAGES_PER_STEP, n_pages)
    spp = PAGE_SIZE // CMP_STRIDE

    def page_spec(p):
        return pl.BlockSpec((1, 1, NSA_KV_HEADS, 2, HEAD_DIM, PAGE_SIZE),
                            lambda b, g, pt: (layer, pt[b, g * npg + p], 0, 0, 0, 0))

    gs = pltpu.PrefetchScalarGridSpec(
        num_scalar_prefetch=1, grid=(B, n_pages // npg),
        in_specs=[page_spec(p) for p in range(npg)]
        + [pl.BlockSpec((CMP_STRIDE, LANES, 2 * LANES), lambda b, g, pt: (0, 0, 0))],
        out_specs=pl.BlockSpec((1, npg * spp, 4 * LANES), lambda b, g, pt: (b, g, 0)),
        scratch_shapes=[pltpu.VMEM((NSA_KV_HEADS, npg * PAGE_SIZE, LANES), F32)])
    return pl.pallas_call(
        _subproj_pages_body, out_shape=jax.ShapeDtypeStruct((B, n_pages * spp, 4 * LANES), F32), grid_spec=gs,
        compiler_params=_cparams(("parallel", "arbitrary"), 48), name="cmp_subproj_pages")(
            page_table, *([cache_t] * npg), w)


def _cmp_mlp_body(h_ref, w2_ref, g_ref, o_ref):
    x = h_ref[0]
    kv = _mm(jax.nn.gelu(x), w2_ref[...])
    g = g_ref[...]
    segs = []
    for j in range(4):
        seg = kv[:, j * HEAD_DIM:(j + 1) * HEAD_DIM]
        if j % 2 == 0:
            seg = seg * lax.rsqrt(jnp.mean(seg * seg, -1, keepdims=True) + NORM_EPS) * g
        segs.append(seg)
    o_ref[0] = jnp.concatenate(segs, 1)


def _cmp_mlp(hid, w2bd, g):
    B, n, _ = hid.shape
    tn = min(512, n)
    return pl.pallas_call(
        _cmp_mlp_body, out_shape=jax.ShapeDtypeStruct((B, n, 256), F32), grid=(B, n // tn),
        in_specs=[pl.BlockSpec((1, tn, 256), lambda b, i: (b, i, 0)), pl.BlockSpec((256, 256), lambda b, i: (0, 0)),
                  pl.BlockSpec((1, HEAD_DIM), lambda b, i: (0, 0))],
        out_specs=pl.BlockSpec((1, tn, 256), lambda b, i: (b, i, 0)),
        compiler_params=_cparams(("parallel", "parallel")), name="cmp_mlp")(hid, w2bd, g.reshape(1, HEAD_DIM))


def _cmp_attn_body(q_ref, kv_ref, cov_ref, o_ref, imp_ref, *, pos0):
    qi = pl.program_id(2)
    tq = q_ref.shape[1]
    n = kv_ref.shape[1]
    kv = kv_ref[0]
    k = kv[:, :HEAD_DIM]
    v = kv[:, HEAD_DIM:]
    q = q_ref[0]
    pos = pos0 + qi * tq + lax.broadcasted_iota(jnp.int32, (tq, 1), 0)
    end = lax.broadcasted_iota(jnp.int32, (1, n), 1) * CMP_STRIDE + (CMP_BLOCK - 1)
    mask = end <= pos
    psum = jnp.zeros((tq, n), F32)
    outs = []
    for g in range(2):
        s = _mm_nt(q[:, g * HEAD_DIM:(g + 1) * HEAD_DIM], k) * SCALE
        s = jnp.where(mask, s, -jnp.inf)
        m = jnp.max(s, -1, keepdims=True)
        e = jnp.exp(s - jnp.where(m == -jnp.inf, 0.0, m))
        p = e / jnp.maximum(jnp.sum(e, -1, keepdims=True), 1e-30)
        outs.append(_mm(p, v))
        psum = psum + p
    o_ref[0] = jnp.concatenate(outs, 1)
    hi = psum.astype(BF16)
    lo = (psum - hi.astype(F32)).astype(BF16)
    cov = cov_ref[...]
    imp_ref[0, 0] = _mm(hi, cov) + _mm(lo, cov)


def _cmp_attn(qn, kv_cmp, cover, pos0, tq):
    B, T, _ = qn.shape
    n = kv_cmp.shape[1]
    nbp = cover.shape[1]
    return pl.pallas_call(
        functools.partial(_cmp_attn_body, pos0=pos0),
        out_shape=(jax.ShapeDtypeStruct((B, T, GW), F32), jax.ShapeDtypeStruct((B, NSA_KV_HEADS, T, nbp), F32)),
        grid=(B, NSA_KV_HEADS, T // tq),
        in_specs=[pl.BlockSpec((1, tq, LANES), lambda b, h, i: (b, i, h)),
                  pl.BlockSpec((1, n, LANES), lambda b, h, i: (b, 0, h)),
                  pl.BlockSpec((n, nbp), lambda b, h, i: (0, 0))],
        out_specs=(pl.BlockSpec((1, tq, LANES), lambda b, h, i: (b, i, h)),
                   pl.BlockSpec((1, 1, tq, nbp), lambda b, h, i: (b, h, i, 0))),
        compiler_params=_cparams(("parallel", "parallel", "arbitrary")), name="cmp_attn")(qn, kv_cmp, cover)


def _topk_body(imp_ref, sel_ref, idx_ref, *, pos0, t_rows, n_blk):
    ti = pl.program_id(0)
    x = imp_ref[...]
    nbp = x.shape[1]
    xt = jnp.concatenate([x[:, j * LANES:(j + 1) * LANES].T for j in range(nbp // LANES)], 0)
    r = ti * LANES + lax.broadcasted_iota(jnp.int32, (1, LANES), 1)
    cur = (pos0 + r % t_rows) // SLC_BLOCK
    blk = lax.broadcasted_iota(jnp.int32, (nbp, 1), 0)
    forced = (blk == 0) | (blk == cur) | (blk == cur - 1)
    val = jnp.where(forced, jnp.inf, jnp.where(blk <= cur, xt, -jnp.inf))
    val = jnp.where(blk < n_blk, val, -jnp.inf)
    cnt = jnp.zeros((nbp, LANES), F32)
    for i in range(n_blk):
        vi = val[i:i + 1, :]
        ahead = (vi > val) | ((vi == val) & (blk > i))
        cnt = cnt + jnp.where(ahead, 1.0, 0.0)
    chosen = (cnt < float(N_SELECT)) & (val > -jnp.inf)
    self32 = jnp.where(chosen, 1.0, 0.0)
    sel_ref[...] = jnp.concatenate([self32[j * LANES:(j + 1) * LANES, :].T for j in range(nbp // LANES)], 1)
    blk_f = blk.astype(F32)
    rows = []
    for j in range(N_SELECT):
        hit = chosen & (cnt == float(j))
        rows.append(jnp.sum(jnp.where(hit, blk_f + 1.0, 0.0), 0, keepdims=True) - 1.0)
    idx_ref[...] = jnp.concatenate(rows, 0).astype(jnp.int32)


def _topk(imp2d, pos0, t_rows, n_blk):
    R, nbp = imp2d.shape
    return pl.pallas_call(
        functools.partial(_topk_body, pos0=pos0, t_rows=t_rows, n_blk=n_blk),
        out_shape=(jax.ShapeDtypeStruct((R, nbp), F32), jax.ShapeDtypeStruct((N_SELECT, R), jnp.int32)),
        grid=(R // LANES,),
        in_specs=[pl.BlockSpec((LANES, nbp), lambda i: (i, 0))],
        out_specs=(pl.BlockSpec((LANES, nbp), lambda i: (i, 0)), pl.BlockSpec((N_SELECT, LANES), lambda i: (0, i))),
        compiler_params=_cparams(("parallel",)), name="topk")(imp2d)


def _attn_body(*refs, mode, pairs, tk, pos_q0, pos_k0, scale):
    if mode == 'slc':
        q_ref, kv_ref, sel_ref, o_ref = refs
    else:
        q_ref, kv_ref, o_ref = refs
    qi = pl.program_id(2)
    tq = q_ref.shape[1]
    n_k = kv_ref.shape[1]
    n_tiles = n_k // tk
    q = q_ref[0]
    pq0 = pos_q0 + qi * tq
    qpos = pq0 + lax.broadcasted_iota(jnp.int32, (tq, 1), 0)
    if mode == 'none':
        lo, hi = 0, n_tiles
    else:
        r_hi = jnp.minimum(n_k - 1, pq0 + tq - 1 - pos_k0)
        hi = r_hi // tk + 1
        lo = jnp.maximum(0, pq0 - (WINDOW - 1) - pos_k0) // tk if mode == 'win' else 0
    if mode == 'slc':
        sel = sel_ref[0, 0].astype(BF16)
        nbp = sel.shape[1]

    def step(kt, carry):
        r0 = pl.multiple_of(kt * tk, tk)
        kvt = kv_ref[0, pl.ds(r0, tk), :]
        krow = r0 + lax.broadcasted_iota(jnp.int32, (1, tk), 1)
        kpos = pos_k0 + krow
        if mode == 'win':
            d = qpos - kpos
            mask = (d >= 0) & (d < WINDOW) & (kpos >= 0)
        elif mode == 'slc':
            expand = (lax.broadcasted_iota(jnp.int32, (nbp, 1), 0) == kpos // SLC_BLOCK)
            selx = _mm(sel, jnp.where(expand, 1.0, 0.0).astype(BF16))
            mask = (selx > 0.5) & (kpos <= qpos)
        else:
            mask = None
        new = []
        for p, (q_lo, k_lo, v_lo) in enumerate(pairs):
            m, l, acc = carry[3 * p:3 * p + 3]
            s = _mm_nt(q[:, q_lo:q_lo + HEAD_DIM], kvt[:, k_lo:k_lo + HEAD_DIM]) * scale
            if mask is not None:
                s = jnp.where(mask, s, -jnp.inf)
            m_new = jnp.maximum(m, jnp.max(s, -1, keepdims=True))
            alpha = jnp.exp(m - m_new)
            pr = jnp.exp(s - m_new)
            l = alpha * l + jnp.sum(pr, -1, keepdims=True)
            acc = alpha * acc + _mm(pr, kvt[:, v_lo:v_lo + HEAD_DIM])
            new += [m_new, l, acc]
        return tuple(new)

    init = []
    for _ in pairs:
        init += [jnp.full((tq, 1), NEG_BIG, F32), jnp.zeros((tq, 1), F32), jnp.zeros((tq, HEAD_DIM), F32)]
    res = lax.fori_loop(lo, hi, step, tuple(init))
    o_ref[0] = jnp.concatenate([res[3 * p + 2] / jnp.maximum(res[3 * p + 1], 1e-30) for p in range(len(pairs))], 1)


def _attn(q, kv, mode, pairs, kv_width, pos_q0, pos_k0, tq, tk, sel=None):
    B, T, _ = q.shape
    n_k = kv.shape[1]
    in_specs = [pl.BlockSpec((1, tq, LANES), lambda b, h, i: (b, i, h)),
                pl.BlockSpec((1, n_k, kv_width), lambda b, h, i: (b, 0, h))]
    args = [q, kv]
    if mode == 'slc':
        in_specs.append(pl.BlockSpec((1, 1, tq, sel.shape[-1]), lambda b, h, i: (b, h, i, 0)))
        args.append(sel)
    return pl.pallas_call(
        functools.partial(_attn_body, mode=mode, pairs=pairs, tk=tk, pos_q0=pos_q0, pos_k0=pos_k0, scale=SCALE),
        out_shape=jax.ShapeDtypeStruct((B, T, GW), F32), grid=(B, 2, T // tq),
        in_specs=in_specs, out_specs=pl.BlockSpec((1, tq, LANES), lambda b, h, i: (b, i, h)),
        compiler_params=_cparams(("parallel", "parallel", "arbitrary")), name="attn_" + mode)(*args)


_GQA_PAIRS = ((0, 0, HEAD_DIM), (HEAD_DIM, 0, HEAD_DIM))
_MHA_PAIRS = ((0, 0, HEAD_DIM), (HEAD_DIM, 2 * HEAD_DIM, 3 * HEAD_DIM))


def _slc_paged_body(idx_ref, phys_ref, *refs, pos0, blk0, t_real):
    del phys_ref
    q_ref = refs[0]
    blk_refs = refs[1:1 + N_SELECT]
    new_ref = refs[1 + N_SELECT]
    o_ref = refs[2 + N_SELECT]
    b, h, t = pl.program_id(0), pl.program_id(1), pl.program_id(2)
    base = ((b * NSA_KV_HEADS + h) * t_real + t) * N_SELECT
    qrow = q_ref[0, pl.ds(t, 1), :]
    q2 = jnp.concatenate([qrow[:, :HEAD_DIM], qrow[:, HEAD_DIM:], jnp.zeros((6, HEAD_DIM), F32)], 0)
    pos = pos0 + t
    bpp = PAGE_SIZE // SLC_BLOCK
    tok = lax.broadcasted_iota(jnp.int32, (1, PAGE_SIZE), 1)
    scores, vts = [], []
    n_new = jnp.int32(0)
    for j in range(N_SELECT):
        idx = idx_ref[base + j]
        idc = jnp.maximum(idx, 0)
        kv_t = blk_refs[j][0, 0, 0]
        ok = (idx >= 0) & (idx < blk0) & (tok // SLC_BLOCK == idc % bpp) & ((idc // bpp) * PAGE_SIZE + tok <= pos)
        scores.append(_mm(q2, kv_t[0]) * SCALE + jnp.where(ok, 0.0, -jnp.inf))
        vts.append(kv_t[1])
        n_new = n_new + jnp.where(idx >= blk0, 1, 0)
    newblk = new_ref[0]
    lane = lax.broadcasted_iota(jnp.int32, (1, SLC_BLOCK), 1)
    ok_new = (n_new > 0) & (blk0 * SLC_BLOCK + lane <= pos)
    s_new = _mm_nt(q2, newblk[:, :HEAD_DIM]) * SCALE + jnp.where(ok_new, 0.0, -jnp.inf)
    m = jnp.max(s_new, -1, keepdims=True)
    for s in scores:
        m = jnp.maximum(m, jnp.max(s, -1, keepdims=True))
    m = jnp.where(m == -jnp.inf, 0.0, m)
    e_new = jnp.exp(s_new - m)
    den = jnp.sum(e_new, -1, keepdims=True)
    o = _mm(e_new, newblk[:, HEAD_DIM:])
    for s, vt in zip(scores, vts):
        e = jnp.exp(s - m)
        den = den + jnp.sum(e, -1, keepdims=True)
        o = o + _mm_nt(e, vt)
    o = o / jnp.maximum(den, 1e-30)
    orow = jnp.concatenate([o[0:1], o[1:2]], 1)
    o_ref[0, 0, 0] = jnp.broadcast_to(orow, (8, LANES))


def _slc_paged(q_rot, cache_t, new_rows, idx_flat, page_flat, layer, pos0, blk0, t_real):
    B = q_rot.shape[0]
    tp = q_rot.shape[1]

    def blk_spec(j):
        def imap(b, h, t, idx, page):
            return (layer, page[((b * NSA_KV_HEADS + h) * t_real + t) * N_SELECT + j], h, 0, 0, 0)
        return pl.BlockSpec((1, 1, 1, 2, HEAD_DIM, PAGE_SIZE), imap)

    gs = pltpu.PrefetchScalarGridSpec(
        num_scalar_prefetch=2, grid=(B, NSA_KV_HEADS, t_real),
        in_specs=[pl.BlockSpec((1, tp, LANES), lambda b, h, t, idx, page: (b, 0, h))]
        + [blk_spec(j) for j in range(N_SELECT)]
        + [pl.BlockSpec((1, SLC_BLOCK, LANES), lambda b, h, t, idx, page: (b, 0, h))],
        out_specs=pl.BlockSpec((1, 1, 1, 8, LANES), lambda b, h, t, idx, page: (b, h, t, 0, 0)))
    out = pl.pallas_call(
        functools.partial(_slc_paged_body, pos0=pos0, blk0=blk0, t_real=t_real),
        out_shape=jax.ShapeDtypeStruct((B, NSA_KV_HEADS, t_real, 8, LANES), F32), grid_spec=gs,
        compiler_params=_cparams(("parallel", "parallel", "arbitrary")), name="slc_paged")(
            idx_flat, page_flat, q_rot, *([cache_t] * N_SELECT), new_rows)
    return jnp.transpose(out[:, :, :, 0, :], (0, 2, 1, 3)).reshape(B, t_real, GW)


def _rms(x, g):
    return x * lax.rsqrt(jnp.mean(x * x, -1, keepdims=True) + NORM_EPS) * g


def _rope(x, pos):
    half = HEAD_DIM // 2
    inv = ROPE_THETA ** (-jnp.arange(half, dtype=F32) / half)
    ang = pos.astype(F32)[:, None] * inv
    cos, sin = jnp.cos(ang)[:, None, :], jnp.sin(ang)[:, None, :]
    x1, x2 = x[..., :half], x[..., half:]
    return jnp.concatenate([x1 * cos - x2 * sin, x1 * sin + x2 * cos], -1)


def _pad_t(x, tp, value=0.0):
    t = x.shape[1]
    if t == tp:
        return x
    return jnp.pad(x, ((0, 0), (0, tp - t)) + ((0, 0),) * (x.ndim - 2), constant_values=value)


def _round_up(n, m):
    return -(-n // m) * m


def _cover_matrix(n_cmp, n_cmp_pad, n_slc, nbp):
    start = np.arange(n_cmp_pad)[:, None] * CMP_STRIDE
    blk = np.arange(nbp)[None, :]
    cov = (start < (blk + 1) * SLC_BLOCK) & (start + CMP_BLOCK > blk * SLC_BLOCK)
    cov &= (np.arange(n_cmp_pad)[:, None] < n_cmp) & (blk < n_slc)
    return jnp.asarray(cov.astype(np.float32), dtype=BF16)


def _mlstm_mixer(P, Pp, conv_buf, c0, n0, m0, prm):
    B, T, _ = P.shape
    keep = MLSTM_CONV - 1
    tail = jnp.pad(conv_buf, ((0, 0), (8 - keep, 0), (0, 0)))
    m0p = jnp.pad(m0, ((0, 0), (0, LANES - N_HEADS))).reshape(B, 1, LANES)
    out, C, n, m = _mlstm(Pp, T, tail[..., :GW], tail[..., GW:], prm['mlstm_blk'], c0, n0, m0p)
    qk_raw = jnp.concatenate([P[:, -keep:, C_MQ:C_MQ + GW], P[:, -keep:, C_MK:C_MK + GW]], -1)
    conv_new = jnp.concatenate([conv_buf, qk_raw], 1)[:, -keep:]
    return out[:, :T], C, n, m[:, 0, :N_HEADS], conv_new


def _rwkv_mixer(P, Pp, shift_buf, s0, prm, prec):
    B, T, _ = P.shape
    out, S = _rwkv(Pp, T, shift_buf, prm['rwkv_blk'], prm['rwkv_lora'], s0, prec)
    shift_new = jnp.concatenate([P[:, -1:, C_RR:C_RR + 3 * GW], P[:, -1:, C_RL:C_RL + LANES]], -1)
    return out[:, :T], S, shift_new


def _memory_kv(mem, prm):
    B = mem.shape[0]
    kv = _proj_in(mem.reshape(B * N_MEM, D_MODEL), prm['mem_norm_g'], prm['w_mem_kv_bf16'], 2 * GW)
    kv = kv.reshape(B, N_MEM, MEM_HEADS, 2, HEAD_DIM)
    return jnp.stack([_rms(kv[:, :, :, 0], prm['mem_qk_g'][1]), kv[:, :, :, 1]], 3)


def _memory_mixer(P, mem_kv, prm):
    B, T, _ = P.shape
    tp = _round_up(T, 8)
    qn = _rms(P[..., C_CQ:C_CQ + GW].reshape(B, T, MEM_HEADS, HEAD_DIM), prm['mem_qk_g'][0]).reshape(B, T, GW)
    tq = min(256, tp)
    o = _attn(_pad_t(qn, tp), mem_kv.reshape(B, N_MEM, 2 * GW), 'none', _MHA_PAIRS, 2 * LANES, 0, 0, tq, N_MEM)
    return o[:, :T] * jax.nn.silu(P[..., C_CZ:C_CZ + GW])


def _nsa_mixer(P2d, B, T, pos0, win_prefix, prm, past):
    P = P2d.reshape(B, T, DP)
    pos = pos0 + jnp.arange(T, dtype=jnp.int32)
    g = prm['nsa_qk_g']
    tp = _round_up(T, 8)
    qn = _rms(P[..., C_NQ:C_NQ + GW].reshape(B, T, N_HEADS, HEAD_DIM), g[0])
    q_rot = _pad_t(_rope(qn, pos).reshape(B, T, GW), tp)
    qn = _pad_t(qn.reshape(B, T, GW), tp)
    kvrows = lambda c0: P[..., c0:c0 + 2 * KVW].reshape(B, T, NSA_KV_HEADS, 2, HEAD_DIM)
    cmp_new = kvrows(C_NCMP)
    slc_raw, win_raw = kvrows(C_NSLC), kvrows(C_NWIN)
    slc_new = jnp.stack([_rope(_rms(slc_raw[:, :, :, 0], g[2]), pos), slc_raw[:, :, :, 1]], 3)
    win_new = jnp.stack([_rope(_rms(win_raw[:, :, :, 0], g[3]), pos), win_raw[:, :, :, 1]], 3)

    L_all = pos0 + T
    n_sub = max(-(-L_all // CMP_STRIDE), CMP_BLOCK // CMP_STRIDE)
    n_cmp = n_sub - 1
    bd, w2bd, pe_hid = prm['cmp_bd'], prm['cmp_w2bd'], prm['cmp_pe_hid']
    if past is None:
        G = _subproj(P2d, C_NCMP // LANES, bd).reshape(B, T // CMP_STRIDE, 4 * LANES)
    else:
        cache_cmp_t, cache_slc_t, page_table, layer = past
        g_pages = _subproj_pages(cache_cmp_t, page_table, layer, bd)
        new_rows = _pad_t(cmp_new.reshape(B, T, 2 * KVW), CMP_STRIDE).reshape(B * CMP_STRIDE, 2 * KVW)
        g_new = _subproj(new_rows, 0, bd).reshape(B, 1, 4 * LANES)
        G = jnp.concatenate([g_pages, g_new], 1)
    n_cmp_pad = _round_up(n_cmp, LANES)
    gb = G[:, 1:, 256:]
    ga = G[:, :, :256]
    fit = lambda t: _pad_t(t, max(n_cmp_pad, t.shape[1]))[:, :n_cmp_pad]
    hid = fit(ga) + fit(gb) + pe_hid
    kv_cmp = _cmp_mlp(hid, w2bd, g[1])
    n_slc = -(-L_all // SLC_BLOCK)
    nbp = _round_up(n_slc, LANES)
    cover = _cover_matrix(n_cmp, n_cmp_pad, n_slc, nbp)
    tq = min(256, tp)
    o_cmp, imp = _cmp_attn(qn, kv_cmp, cover, pos0, tq)

    R = B * NSA_KV_HEADS * tp
    rp = _round_up(R, LANES)
    imp2d = jnp.pad(imp.reshape(R, nbp), ((0, rp - R), (0, 0)))
    sel, idx_t = _topk(imp2d, pos0, tp, n_slc)

    if past is None:
        sel4 = sel[:R].reshape(B, NSA_KV_HEADS, tp, nbp)
        o_slc = _attn(q_rot, slc_new.reshape(B, T, 2 * KVW), 'slc', _GQA_PAIRS, LANES, pos0, pos0, tq,
                      min(256, T), sel=sel4)
    else:
        bpp = PAGE_SIZE // SLC_BLOCK
        idx = idx_t[:, :R].T.reshape(B, NSA_KV_HEADS, tp, N_SELECT)[:, :, :T]
        idc = jnp.clip(idx, 0, page_table.shape[1] * bpp - 1)
        page = page_table[jnp.arange(B)[:, None, None, None], idc // bpp]
        new_rows = _pad_t(slc_new.reshape(B, T, 2 * KVW), SLC_BLOCK)
        o_slc = _slc_paged(q_rot, cache_slc_t, new_rows, idx.reshape(-1), page.reshape(-1).astype(jnp.int32),
                           layer, pos0, pos0 // SLC_BLOCK, T)

    if win_prefix.shape[1] == 0:
        win_ctx = win_new
        pos_k0 = pos0
    else:
        win_ctx = jnp.concatenate([win_prefix, win_new], 1)
        pos_k0 = pos0 - win_prefix.shape[1]
    n_k = win_ctx.shape[1]
    tkw = min(256, _round_up(n_k, LANES))
    kv_win = _pad_t(win_ctx.reshape(B, n_k, 2 * KVW), _round_up(n_k, tkw))
    o_win = _attn(q_rot, kv_win, 'win', _GQA_PAIRS, LANES, pos0, pos_k0, tq, tkw)

    gt = jax.nn.sigmoid(P[..., C_NG:C_NG + 3 * N_HEADS]).reshape(B, T, 3, N_HEADS, 1)
    hd = lambda o: o[:, :T].reshape(B, T, N_HEADS, HEAD_DIM)
    o = gt[:, :, 0] * hd(o_cmp) + gt[:, :, 1] * hd(o_slc) + gt[:, :, 2] * hd(o_win)
    keep = win_prefix.shape[1] if past is not None else min(WINDOW, T)
    out = o.reshape(B, T, GW) * jax.nn.silu(P[..., C_NZ:C_NZ + GW])
    return out, cmp_new, slc_new, win_ctx[:, -keep:]


RWKV_PREC = ('r3', 'bf16', 'bf16')


def _layer(x, pos0, st, mem_kv, prm, past, rwkv_prec=RWKV_PREC):
    conv_buf, c0, n0, m0, s0, shift_buf, win_prefix = st
    B, T, _ = x.shape
    x2d = x.reshape(B * T, D_MODEL)
    P2d = _proj_in(x2d, prm['norm_g'], prm['w_in_bf16'], 640)
    P = P2d.reshape(B, T, DP)
    Pp = _pad_t(P, _round_up(T, max(MLSTM_L, RWKV_L)))
    y_m, C, n, m, conv_new = _mlstm_mixer(P, Pp, conv_buf, c0, n0, m0, prm)
    y_n, cmp_new, slc_new, win_new = _nsa_mixer(P2d, B, T, pos0, win_prefix, prm, past)
    y_r, S, shift_new = _rwkv_mixer(P, Pp, shift_buf, s0, prm, rwkv_prec)
    y_c = _memory_mixer(P, mem_kv, prm)
    ys = [t.reshape(B * T, GW) for t in (y_m, y_n, y_r, y_c)]
    out = _proj_out(x2d, ys, prm['w_out_bf16']).reshape(B, T, D_MODEL)
    return out, (cmp_new, slc_new, win_new, C, n, m, conv_new, S, shift_new)


def _prep_params(l, p):
    prm = {k: v[l] for k, v in p.items()}
    src = jnp.asarray(np.maximum(_SRC, 0), jnp.int32)
    keep = jnp.asarray((_SRC >= 0).astype(np.float32))
    prm['w_in_bf16'] = (jnp.take(prm['w_in'], src, axis=1) * keep).astype(BF16)
    prm['w_out_bf16'] = prm['w_out'].astype(BF16)
    wm = prm['w_mem_kv'].reshape(D_MODEL, 2, MEM_HEADS, HEAD_DIM)
    prm['w_mem_kv_bf16'] = jnp.transpose(wm, (0, 2, 1, 3)).reshape(D_MODEL, 2 * GW).astype(BF16)
    rows = lambda *vs: jnp.concatenate([jnp.pad(v.reshape(-1, v.shape[-1]), ((0, 0), (0, GW - v.shape[-1])))
                                        for v in vs], 0)
    pad16 = lambda blk: jnp.pad(blk, ((0, 16 - blk.shape[0]), (0, 0)))
    cw, cb = prm['mlstm_conv_w'], prm['mlstm_conv_b']
    prm['mlstm_blk'] = pad16(rows(cw[:, :GW], cw[:, GW:], cb[:GW], cb[GW:], prm['mlstm_norm_g'],
                                  prm['mlstm_gate_b'].reshape(1, 2 * N_HEADS)))
    mu = prm['rwkv_mu']
    prm['rwkv_blk'] = pad16(rows(mu[:GW], mu[GW:2 * GW], mu[2 * GW:3 * GW], mu[3 * GW:], prm['rwkv_w0'],
                                 prm['rwkv_a0'], prm['rwkv_kk'], prm['rwkv_ln']))
    zl = jnp.zeros((DECAY_LORA, GW), F32)
    prm['rwkv_lora'] = jnp.concatenate([jnp.concatenate([prm['rwkv_w2'], zl], 1),
                                        jnp.concatenate([zl, prm['rwkv_a2']], 1)], 0)
    w1 = prm['nsa_cmp_w1']
    eye_h = jnp.eye(NSA_KV_HEADS, dtype=F32)
    eye_c = jnp.eye(2, dtype=F32)
    w1r = w1.reshape(2, 2, CMP_STRIDE, HEAD_DIM, HEAD_DIM)
    bd = jnp.einsum('cC,crsde->scdrCe', eye_c, w1r)
    prm['cmp_bd'] = bd.reshape(CMP_STRIDE, LANES, 2 * LANES).astype(BF16)
    w2 = prm['nsa_cmp_w2']
    prm['cmp_w2bd'] = jnp.einsum('hH,cC,ced->hceHCd', eye_h, eye_c, w2).reshape(2 * KVW, 2 * KVW)
    pe_hid = jnp.einsum('csd,csde->ce', prm['nsa_pe'], w1, precision=HI)
    prm['cmp_pe_hid'] = jnp.tile(pe_hid.reshape(1, 2 * HEAD_DIM), (1, NSA_KV_HEADS)).reshape(2 * KVW)
    return prm


def kernel(x_prompt, x_sample, cache_cmp_kv, cache_slc_kv, cache_win_kv, cache_mem_kv, state_mlstm_C, state_mlstm_n, state_mlstm_m, state_mlstm_conv, state_rwkv_S, state_rwkv_shift, page_table, mem_prompt, norm_g, w_in, w_out, mlstm_conv_w, mlstm_conv_b, mlstm_gate_b, mlstm_norm_g, nsa_qk_g, nsa_pe, nsa_cmp_w1, nsa_cmp_w2, rwkv_mu, rwkv_w0, rwkv_w2, rwkv_a0, rwkv_a2, rwkv_kk, rwkv_ln, mem_norm_g, w_mem_kv, mem_qk_g):
    params = dict(norm_g=norm_g, w_in=w_in, w_out=w_out, mlstm_conv_w=mlstm_conv_w, mlstm_conv_b=mlstm_conv_b,
                  mlstm_gate_b=mlstm_gate_b, mlstm_norm_g=mlstm_norm_g, nsa_qk_g=nsa_qk_g, nsa_pe=nsa_pe,
                  nsa_cmp_w1=nsa_cmp_w1, nsa_cmp_w2=nsa_cmp_w2, rwkv_mu=rwkv_mu, rwkv_w0=rwkv_w0, rwkv_w2=rwkv_w2,
                  rwkv_a0=rwkv_a0, rwkv_a2=rwkv_a2, rwkv_kk=rwkv_kk, rwkv_ln=rwkv_ln, mem_norm_g=mem_norm_g,
                  w_mem_kv=w_mem_kv, mem_qk_g=mem_qk_g)
    depth = norm_g.shape[0]
    B = x_prompt.shape[0]
    past_len = page_table.shape[1] * PAGE_SIZE
    cache_cmp_t = jnp.transpose(cache_cmp_kv, (0, 1, 3, 4, 5, 2))
    cache_slc_t = jnp.transpose(cache_slc_kv, (0, 1, 3, 4, 5, 2))
    xp, xs = x_prompt, x_sample
    new_p, new_s, new_mem = [], [], []
    for l in range(depth):
        prm = _prep_params(l, params)
        mem_kv_p = _memory_kv(mem_prompt, prm)
        st_p = (jnp.zeros((B, MLSTM_CONV - 1, 2 * GW), F32),
                jnp.zeros((B, N_HEADS, HEAD_DIM, HEAD_DIM), F32),
                jnp.zeros((B, N_HEADS, HEAD_DIM), F32),
                jnp.full((B, N_HEADS), M_INIT, F32),
                jnp.zeros((B, N_HEADS, HEAD_DIM, HEAD_DIM), F32),
                jnp.zeros((B, 1, RWKV_SHIFT), F32),
                jnp.zeros((B, 0, NSA_KV_HEADS, 2, HEAD_DIM), F32))
        xp, sp = _layer(xp, 0, st_p, mem_kv_p, prm, None)
        st_s = (state_mlstm_conv[l], state_mlstm_C[l], state_mlstm_n[l], state_mlstm_m[l],
                state_rwkv_S[l], state_rwkv_shift[l], cache_win_kv[l])
        xs, ss = _layer(xs, past_len, st_s, cache_mem_kv[l], prm, (cache_cmp_t, cache_slc_t, page_table, l))
        new_p.append(sp)
        new_s.append(ss)
        new_mem.append(mem_kv_p)
    stack = lambda states, i: jnp.stack([s[i] for s in states])
    outs = [xp, xs]
    for i in range(3):
        outs += [stack(new_p, i), stack(new_s, i)]
    outs.append(jnp.stack(new_mem))
    for i in range(3, 9):
        outs += [stack(new_p, i), stack(new_s, i)]
    return tuple(outs)
```

```python
import functools

import numpy as np
import jax
import jax.numpy as jnp
from jax import lax
from jax.experimental import pallas as pl
from jax.experimental.pallas import tpu as pltpu

F32 = jnp.float32
BF16 = jnp.bfloat16
HI = lax.Precision.HIGHEST

D_MODEL = 1024
PAGE_SIZE = 128
HEAD_DIM = 64
GW = D_MODEL // 4
N_HEADS = GW // HEAD_DIM
SCALE = HEAD_DIM ** -0.5
MLSTM_CONV = 4
M_INIT = -1e30
NSA_KV_HEADS = 2
KVW = NSA_KV_HEADS * HEAD_DIM
CMP_BLOCK = 32
CMP_STRIDE = 16
SLC_BLOCK = 64
N_SELECT = 16
WINDOW = 512
DECAY_LORA = 64
AAA_LORA = 64
RWKV_SHIFT = 3 * GW + DECAY_LORA + AAA_LORA
RWKV_LN_EPS = HEAD_DIM * 1e-5
N_MEM = 256
MEM_HEADS = 4
ROPE_THETA = 10000.0
NORM_EPS = 1e-6

LANES = 128
MLSTM_L = 128
RWKV_L = 64
NEG_BIG = -1e30

_M0, _N0, _R0, _C0 = 0, 1288, 2580, 3732
C_MQ, C_MK, C_MV, C_MO, C_MZ = 0, 256, 512, 768, 1024
C_NQ, C_NCMP, C_NSLC, C_NWIN, C_NZ = 1280, 1536, 1792, 2048, 2304
C_RR, C_RK, C_RV, C_RZ = 2560, 2816, 3072, 3328
C_CQ, C_CZ = 3584, 3840
C_RL = 4096
C_MG = 4224
C_NG = 4352
DP = 4480


def _packed_src():
    src = -np.ones((DP,), np.int64)

    def put(dst, lo, n):
        src[dst:dst + n] = np.arange(lo, lo + n)

    put(C_MQ, _M0, 256); put(C_MK, _M0 + 256, 256); put(C_MV, _M0 + 512, 256)
    put(C_MG, _M0 + 768, 8); put(C_MO, _M0 + 776, 256); put(C_MZ, _M0 + 1032, 256)
    put(C_NQ, _N0, 256)
    for i, base in enumerate((C_NCMP, C_NSLC, C_NWIN)):
        ksrc = _N0 + 256 + 256 * i
        vsrc = ksrc + 128
        for h in range(2):
            put(base + 128 * h, ksrc + 64 * h, 64)
            put(base + 128 * h + 64, vsrc + 64 * h, 64)
    put(C_NG, _N0 + 1024, 12); put(C_NZ, _N0 + 1036, 256)
    put(C_RR, _R0, 256); put(C_RK, _R0 + 256, 256); put(C_RV, _R0 + 512, 256)
    put(C_RL, _R0 + 768, 128); put(C_RZ, _R0 + 896, 256)
    put(C_CQ, _C0, 256); put(C_CZ, _C0 + 256, 256)
    return src


_SRC = _packed_src()


def _split2(a):
    hi = a.astype(BF16)
    return hi, (a - hi.astype(F32)).astype(BF16)


def _dg(a, b, dims, prec):
    dn = (dims, ((), ()))
    if prec == 'bf16':
        return lax.dot_general(a.astype(BF16), b.astype(BF16), dn, preferred_element_type=F32)
    d = lambda x, y: lax.dot_general(x, y, dn, preferred_element_type=F32)
    if prec == 'x3':
        ah, al = _split2(a)
        bh, bl = _split2(b)
        return d(ah, bh) + (d(ah, bl) + d(al, bh))
    if prec in ('r3', 'l3'):
        exact, other = (a, b) if prec == 'r3' else (b, a)
        o1, rest = other.astype(BF16), None
        rest = other - o1.astype(F32)
        o2 = rest.astype(BF16)
        o3 = (rest - o2.astype(F32)).astype(BF16)
        e = exact.astype(BF16)
        if prec == 'r3':
            return d(e, o1) + (d(e, o2) + d(e, o3))
        return d(o1, e) + (d(o2, e) + d(o3, e))
    return lax.dot_general(a, b, dn, preferred_element_type=F32, precision=prec)


def _mm(a, b, prec=None):
    return _dg(a, b, ((1,), (0,)), prec)


def _mm_nt(a, b, prec=None):
    return _dg(a, b, ((1,), (1,)), prec)


def _mm_tn(a, b, prec=None):
    return _dg(a, b, ((0,), (0,)), prec)


def _run_interleaved(chains):
    chains = list(chains)
    while chains:
        alive = []
        for ch in chains:
            try:
                next(ch)
                alive.append(ch)
            except StopIteration:
                pass
        chains = alive


def _cparams(sem, vmem_mb=None):
    kw = dict(dimension_semantics=sem)
    if vmem_mb is not None:
        kw['vmem_limit_bytes'] = vmem_mb * 1024 * 1024
    return pltpu.CompilerParams(**kw)


def _proj_in_body(x_ref, g_ref, w_ref, o_ref):
    x = x_ref[...]
    h = x * lax.rsqrt(jnp.mean(x * x, -1, keepdims=True) + NORM_EPS) * g_ref[...]
    o_ref[...] = jnp.dot(h.astype(BF16), w_ref[...], preferred_element_type=F32)


def _proj_in(x2d, g, w_bf16, tn):
    n, d = x2d.shape
    dn = w_bf16.shape[1]
    tm = min(1024, n)
    return pl.pallas_call(
        _proj_in_body, out_shape=jax.ShapeDtypeStruct((n, dn), F32), grid=(n // tm, dn // tn),
        in_specs=[pl.BlockSpec((tm, d), lambda i, j: (i, 0)),
                  pl.BlockSpec((1, d), lambda i, j: (0, 0)),
                  pl.BlockSpec((d, tn), lambda i, j: (0, j))],
        out_specs=pl.BlockSpec((tm, tn), lambda i, j: (i, j)),
        compiler_params=_cparams(("parallel", "arbitrary"), 48), name="proj_in")(x2d, g.reshape(1, d), w_bf16)


def _silu(x):
    return x * jax.nn.sigmoid(x)


def _proj_out_body(x_ref, ym_ref, yr_ref, oc_ref, os_ref, ow_ref, om_ref, g_ref, zn_ref, zc_ref, e_ref, w_ref, o_ref):
    ge = _mm(jax.nn.sigmoid(g_ref[...]), e_ref[...], 'l3')
    y_n = (ge[:, :GW] * oc_ref[...] + ge[:, GW:2 * GW] * os_ref[...] + ge[:, 2 * GW:] * ow_ref[...]) * _silu(zn_ref[...])
    y_c = om_ref[...] * _silu(zc_ref[...])
    acc = x_ref[...]
    for i, y in enumerate((ym_ref[...], y_n, yr_ref[...], y_c)):
        acc = acc + jnp.dot(y.astype(BF16), w_ref[i * GW:(i + 1) * GW, :], preferred_element_type=F32)
    o_ref[...] = acc


def _proj_out(x2d, P2d, y_m, y_r, o_cmp, o_slc, o_win, o_mem, gate_expand, w_bf16):
    n, d = x2d.shape
    tm = min(512, n)
    yspec = pl.BlockSpec((tm, GW), lambda i: (i, 0))
    return pl.pallas_call(
        _proj_out_body, out_shape=jax.ShapeDtypeStruct((n, d), F32), grid=(n // tm,),
        in_specs=[pl.BlockSpec((tm, d), lambda i: (i, 0)), yspec, yspec, yspec, yspec, yspec, yspec,
                  pl.BlockSpec((tm, LANES), lambda i: (i, C_NG // LANES)),
                  pl.BlockSpec((tm, GW), lambda i: (i, C_NZ // GW)), pl.BlockSpec((tm, GW), lambda i: (i, C_CZ // GW)),
                  pl.BlockSpec((LANES, 3 * GW), lambda i: (0, 0)), pl.BlockSpec((d, d), lambda i: (0, 0))],
        out_specs=pl.BlockSpec((tm, d), lambda i: (i, 0)),
        compiler_params=_cparams(("parallel",), 48), name="proj_out")(
            x2d, y_m, y_r, o_cmp, o_slc, o_win, o_mem, P2d, P2d, P2d, gate_expand, w_bf16)


_MP_CONV_Q, _MP_CONV_K, _MP_BIAS_Q, _MP_BIAS_K, _MP_NORM, _MP_GATE_B = 0, 4, 8, 9, 10, 11


def _mlstm_body(q_ref, k_ref, v_ref, o_ref, z_ref, g_ref, tq_ref, tk_ref, prm_ref, c0_ref, n0_ref, m0_ref,
                h_ref, c_ref, n_ref, m_ref, c_scr, n_scr, m_scr, pq_scr, pk_scr, *, t_valid):
    c = pl.program_id(1)
    nb, L = q_ref.shape[0], q_ref.shape[1]

    @pl.when(c == 0)
    def _():
        c_scr[...] = c0_ref[...]
        n_scr[...] = n0_ref[...]
        m_scr[...] = m0_ref[...]
        pq_scr[...] = jnp.zeros(pq_scr.shape, F32)
        pk_scr[...] = jnp.zeros(pk_scr.shape, F32)
        pq_scr[:, L - 8:L, :] = tq_ref[...]
        pk_scr[:, L - 8:L, :] = tk_ref[...]

    row = lax.broadcasted_iota(jnp.int32, (L, L), 0)
    col = lax.broadcasted_iota(jnp.int32, (L, L), 1)
    causal = row >= col
    tril = causal.astype(F32)
    triu = (row <= col).astype(F32)
    trow = lax.broadcasted_iota(jnp.int32, (L, 1), 0)
    valid = (c * L + trow) < t_valid
    lane = lax.broadcasted_iota(jnp.int32, (1, LANES), 1)
    prm = prm_ref[...]

    def conv(x, prev, w0, b):
        acc = prm[b:b + 1, :] + prm[w0 + MLSTM_CONV - 1:w0 + MLSTM_CONV, :] * x
        for s in range(1, MLSTM_CONV):
            shifted = jnp.where(trow >= s, pltpu.roll(x, s, 0), pltpu.roll(prev, s, 0))
            acc = acc + prm[w0 + MLSTM_CONV - 1 - s:w0 + MLSTM_CONV - s, :] * shifted
        return _silu(acc)

    cums, qs, ks = [], [], []
    for gi in range(nb):
        q_raw, k_raw = q_ref[gi], k_ref[gi]
        qs.append(conv(q_raw, pq_scr[gi], _MP_CONV_Q, _MP_BIAS_Q) * (HEAD_DIM ** -0.5))
        ks.append(conv(k_raw, pk_scr[gi], _MP_CONV_K, _MP_BIAS_K))
        pq_scr[gi] = q_raw
        pk_scr[gi] = k_raw
        x = g_ref[gi] + prm[_MP_GATE_B:_MP_GATE_B + 1, :LANES]
        log_f = jnp.minimum(x, 0.0) - jnp.log1p(jnp.exp(-jnp.abs(x)))
        g = jnp.where(lane < N_HEADS, x, jnp.where(lane < 2 * N_HEADS, log_f, 0.0))
        g = jnp.where(valid, g, jnp.where(lane < N_HEADS, NEG_BIG, 0.0))
        gt = g.T
        cums.append((g, gt, _mm(tril, g, 'r3'), _mm(gt, triu, 'l3')))

    def chain(gi, h):
        g, gt, bc, br = cums[gi]
        sl = slice(h * HEAD_DIM, (h + 1) * HEAD_DIM)
        qh = qs[gi][:, sl]
        kh = ks[gi][:, sl]
        vh = v_ref[gi, :, sl]
        qk = _mm_nt(qh, kh)
        ch = c_scr[gi, h]
        qc = _mm(qh, ch)
        yield
        b_col = bc[:, 4 + h:5 + h]
        li_col = g[:, h:h + 1]
        b_row = br[4 + h:5 + h, :]
        li_row = gt[h:h + 1, :]
        m_prev = m_scr[gi, :, h:h + 1]
        log_d = jnp.where(causal, b_col - b_row + li_row, -jnp.inf)
        log_inter = b_col + m_prev
        m_t = jnp.maximum(jnp.max(log_d, -1, keepdims=True), log_inter)
        s = qk * jnp.exp(log_d - m_t)
        w_inter = jnp.exp(log_inter - m_t)
        nh = n_scr[gi, h:h + 1, :]
        sv = _mm(s, vh)
        yield
        num = sv + w_inter * qc
        den = jnp.sum(s, -1, keepdims=True) + w_inter * jnp.sum(qh * nh, -1, keepdims=True)
        hh = num / jnp.maximum(jnp.abs(den), jnp.exp(-m_t))
        hn = hh * lax.rsqrt(jnp.mean(hh * hh, -1, keepdims=True) + NORM_EPS) * prm[_MP_NORM:_MP_NORM + 1, sl]
        h_ref[gi, :, sl] = jax.nn.sigmoid(o_ref[gi, :, sl]) * hn * _silu(z_ref[gi, :, sl])
        b_end = b_col[L - 1:L, :]
        log_w = b_end - b_col + li_col
        m_new = jnp.maximum(b_end + m_prev, jnp.max(log_w, 0, keepdims=True))
        wk = jnp.exp(log_w - m_new)
        decay = jnp.exp(b_end + m_prev - m_new)
        kw = kh * wk
        c_scr[gi, h] = decay * ch + _mm_tn(kw, vh)
        n_scr[gi, h:h + 1, :] = decay * nh + jnp.sum(kw, 0, keepdims=True)
        m_scr[gi, :, h:h + 1] = m_new

    _run_interleaved([chain(gi, h) for gi in range(nb) for h in range(N_HEADS)])

    @pl.when(c == pl.num_programs(1) - 1)
    def _():
        c_ref[...] = c_scr[...]
        n_ref[...] = n_scr[...]
        m_ref[...] = m_scr[...]


BATCH_ROWS = 4


def _mlstm(P, t_valid, tail_q, tail_k, prm_blk, c0, n0, m0):
    B, tp, _ = P.shape
    L = MLSTM_L
    nb = BATCH_ROWS if B % BATCH_ROWS == 0 else 1
    col = lambda c0_: pl.BlockSpec((nb, L, GW), lambda b, c: (b, c, c0_ // GW))
    tok = pl.BlockSpec((nb, L, GW), lambda b, c: (b, c, 0))
    tail = pl.BlockSpec((nb, 8, GW), lambda b, c: (b, 0, 0))
    sc = pl.BlockSpec((nb, N_HEADS, HEAD_DIM, HEAD_DIM), lambda b, c: (b, 0, 0, 0))
    sn = pl.BlockSpec((nb, N_HEADS, HEAD_DIM), lambda b, c: (b, 0, 0))
    sm = pl.BlockSpec((nb, 1, LANES), lambda b, c: (b, 0, 0))
    return pl.pallas_call(
        functools.partial(_mlstm_body, t_valid=t_valid),
        out_shape=(jax.ShapeDtypeStruct((B, tp, GW), F32), jax.ShapeDtypeStruct(c0.shape, F32),
                   jax.ShapeDtypeStruct(n0.shape, F32), jax.ShapeDtypeStruct(m0.shape, F32)),
        grid=(B // nb, -(-t_valid // L)),
        in_specs=[col(C_MQ), col(C_MK), col(C_MV), col(C_MO), col(C_MZ),
                  pl.BlockSpec((nb, L, LANES), lambda b, c: (b, c, C_MG // LANES)), tail, tail,
                  pl.BlockSpec((16, GW), lambda b, c: (0, 0)), sc, sn, sm],
        out_specs=(tok, sc, sn, sm),
        scratch_shapes=[pltpu.VMEM((nb, N_HEADS, HEAD_DIM, HEAD_DIM), F32), pltpu.VMEM((nb, N_HEADS, HEAD_DIM), F32),
                        pltpu.VMEM((nb, 1, LANES), F32), pltpu.VMEM((nb, L, GW), F32), pltpu.VMEM((nb, L, GW), F32)],
        compiler_params=_cparams(("parallel", "arbitrary")), name="mlstm")(
            P, P, P, P, P, P, tail_q, tail_k, prm_blk, c0, n0, m0)


(_RP_MU_R, _RP_MU_K, _RP_MU_V, _RP_MU_L, _RP_W0, _RP_A0, _RP_KK, _RP_KA, _RP_RK, _RP_LN_G, _RP_LN_B) = range(11)


def _rwkv_body(r_ref, k_ref, v_ref, z_ref, l_ref, sr_ref, sk_ref, sv_ref, sl_ref, prm_ref, lora_ref, s0_ref,
               y_ref, s_ref, s_scr, cr_scr, ck_scr, cv_scr, cl_scr, *, prec, t_valid):
    c_id = pl.program_id(1)

    @pl.when(c_id == 0)
    def _():
        s_scr[...] = s0_ref[...]
        cr_scr[...] = sr_ref[...]
        ck_scr[...] = sk_ref[...]
        cv_scr[...] = sv_ref[...]
        cl_scr[...] = sl_ref[...]

    nb, L = r_ref.shape[0], r_ref.shape[1]
    D = HEAD_DIM
    row = lax.broadcasted_iota(jnp.int32, (L, L), 0)
    col = lax.broadcasted_iota(jnp.int32, (L, L), 1)
    lower = row >= col
    strict = row > col
    tril = lower.astype(F32)
    n_sq = int(np.log2(L)) - 1
    pc, pa, prec = prec
    trow = lax.broadcasted_iota(jnp.int32, (L, 1), 0)
    valid = (c_id * L + trow) < t_valid
    prm = prm_ref[...]
    prow = lambda i, n=GW: prm[i:i + 1, :n]

    def shifted_mix(x_ref, carry_scr, g, mu):
        x = x_ref[g]
        prev = jnp.where(trow >= 1, pltpu.roll(x, 1, 0), carry_scr[g])
        carry_scr[g] = x[L - 1:L, :]
        return x + (prev - x) * mu

    prep = []
    for g in range(nb):
        r = shifted_mix(r_ref, cr_scr, g, prow(_RP_MU_R))
        k = shifted_mix(k_ref, ck_scr, g, prow(_RP_MU_K))
        v = shifted_mix(v_ref, cv_scr, g, prow(_RP_MU_V))
        lo = shifted_mix(l_ref, cl_scr, g, prow(_RP_MU_L, LANES))
        lo_in = jnp.where(lax.broadcasted_iota(jnp.int32, (1, LANES), 1) < DECAY_LORA, jnp.tanh(lo), lo)
        lora = _mm(lo_in, lora_ref[...])
        w_pre = prow(_RP_W0) + lora[:, :GW]
        log_w = -(float(np.exp(-0.5)) * jax.nn.sigmoid(w_pre))
        a = jax.nn.sigmoid(prow(_RP_A0) + lora[:, GW:])
        k_eff = k * (1.0 + (a - 1.0) * prow(_RP_KA))
        kk_raw = k * prow(_RP_KK)
        log_w = jnp.where(valid, log_w, 0.0)
        k_eff = jnp.where(valid, k_eff, 0.0)
        kk_raw = jnp.where(valid, kk_raw, 0.0)
        v = jnp.where(valid, v, 0.0)
        prep.append((r, log_w, k_eff, v, kk_raw, a))

    def chain(g, h):
        sl = slice(h * D, (h + 1) * D)
        r, w, k, v, kk, a = (x[:, sl] for x in prep[g])
        kk = kk / jnp.maximum(jnp.sqrt(jnp.sum(kk * kk, -1, keepdims=True)), 1e-12)
        cum = _mm(tril, w, pc)
        yield
        c_last = cum[L - 1:L, :]
        e_neg = jnp.exp(-cum)
        kh = kk * jnp.exp(cum - w)
        bt = kk * a * e_neg
        kt = k * e_neg
        rh = r * jnp.exp(cum)
        gram = _mm_nt(jnp.concatenate([kh, rh], 0), jnp.concatenate([bt, kt], 0), pa)
        yield
        A = jnp.where(strict, gram[:L, :L], 0.0)
        Bm = jnp.where(strict, gram[:L, L:], 0.0)
        Mb = jnp.where(lower, gram[L:, :L], 0.0)
        Mk = jnp.where(lower, gram[L:, L:], 0.0)
        X = jnp.concatenate([kh, _mm(Bm, v, pa)], 1)
        yield
        Pw = A
        X = X - _mm(Pw, X, pa)
        yield
        for _ in range(n_sq):
            Pw = _mm(Pw, Pw, pa)
            yield
            X = X + _mm(Pw, X, pa)
            yield
        e_end = jnp.exp(c_last - cum)
        bp = kk * a * e_end
        kp = k * e_end
        xtb = _mm_tn(X, bp, prec)
        yield
        wtb = xtb[:D]
        N = _mm_tn(v, kp, prec) - xtb[D:]
        yield
        mbx = _mm(Mb, X, prec)
        yield
        qp = rh - mbx[:, :D]
        y0 = _mm(Mk, v, prec) - mbx[:, D:]
        yield
        s0 = s_scr[g, h]
        y = _mm_nt(qp, s0, prec) + y0
        yield
        s_scr[g, h] = s0 * jnp.exp(c_last) - _mm(s0, wtb, prec) + N
        mu = jnp.mean(y, -1, keepdims=True)
        var = jnp.mean(jnp.square(y - mu), -1, keepdims=True)
        y = (y - mu) * lax.rsqrt(var + RWKV_LN_EPS) * prm[_RP_LN_G:_RP_LN_G + 1, sl] + prm[_RP_LN_B:_RP_LN_B + 1, sl]
        y = y + jnp.sum(r * k * prm[_RP_RK:_RP_RK + 1, sl], -1, keepdims=True) * v
        y_ref[g, :, sl] = y * _silu(z_ref[g, :, sl])

    _run_interleaved([chain(g, h) for g in range(nb) for h in range(N_HEADS)])

    @pl.when(c_id == pl.num_programs(1) - 1)
    def _():
        s_ref[...] = s_scr[...]


def _rwkv(P, t_valid, shift_buf, prm_blk, lora_w, s0, prec):
    B, tp, _ = P.shape
    L = RWKV_L
    nb = BATCH_ROWS if B % BATCH_ROWS == 0 else 1
    col = lambda c0_: pl.BlockSpec((nb, L, GW), lambda b, c: (b, c, c0_ // GW))
    tok = pl.BlockSpec((nb, L, GW), lambda b, c: (b, c, 0))
    car = lambda w: pl.BlockSpec((nb, 1, w), lambda b, c: (b, 0, 0))
    st = pl.BlockSpec((nb, N_HEADS, HEAD_DIM, HEAD_DIM), lambda b, c: (b, 0, 0, 0))
    shifts = [shift_buf[..., i * GW:(i + 1) * GW] for i in range(3)] + [shift_buf[..., 3 * GW:]]
    return pl.pallas_call(
        functools.partial(_rwkv_body, prec=prec, t_valid=t_valid),
        out_shape=(jax.ShapeDtypeStruct((B, tp, GW), F32), jax.ShapeDtypeStruct(s0.shape, F32)),
        grid=(B // nb, -(-t_valid // L)),
        in_specs=[col(C_RR), col(C_RK), col(C_RV), col(C_RZ),
                  pl.BlockSpec((nb, L, LANES), lambda b, c: (b, c, C_RL // LANES)),
                  car(GW), car(GW), car(GW), car(LANES),
                  pl.BlockSpec((16, GW), lambda b, c: (0, 0)), pl.BlockSpec((LANES, 2 * GW), lambda b, c: (0, 0)), st],
        out_specs=(tok, st),
        scratch_shapes=[pltpu.VMEM((nb, N_HEADS, HEAD_DIM, HEAD_DIM), F32), pltpu.VMEM((nb, 1, GW), F32),
                        pltpu.VMEM((nb, 1, GW), F32), pltpu.VMEM((nb, 1, GW), F32), pltpu.VMEM((nb, 1, LANES), F32)],
        compiler_params=_cparams(("parallel", "arbitrary")), name="rwkv")(
            P, P, P, P, P, *shifts, prm_blk, lora_w, s0)


def _subproj_accumulate(load_rows, w_ref, o_ref):
    n = o_ref.shape[-2]
    accs = []
    for h in range(NSA_KV_HEADS):
        acc = jnp.zeros((n, 2 * LANES), F32)
        for s in range(0, CMP_STRIDE, 2):
            xs = jnp.concatenate([load_rows(h, s, n), load_rows(h, s + 1, n)], 1)
            acc = acc + jnp.dot(xs.astype(BF16), w_ref[s // 2], preferred_element_type=F32)
        accs.append(acc)
    out = jnp.concatenate([accs[0][:, :LANES], accs[1][:, :LANES], accs[0][:, LANES:], accs[1][:, LANES:]], 1)
    o_ref[...] = out.reshape(o_ref.shape)


def _subproj_body(x0_ref, x1_ref, w_ref, o_ref):
    xs = (x0_ref, x1_ref)
    _subproj_accumulate(lambda h, s, n: xs[h][pl.ds(s, n, stride=CMP_STRIDE), :], w_ref, o_ref)


def _subproj(rows2d, col0, w):
    n = rows2d.shape[0]
    tm = min(2048, n)
    return pl.pallas_call(
        _subproj_body, out_shape=jax.ShapeDtypeStruct((n // CMP_STRIDE, 4 * LANES), F32), grid=(n // tm,),
        in_specs=[pl.BlockSpec((tm, LANES), lambda i: (i, col0)), pl.BlockSpec((tm, LANES), lambda i: (i, col0 + 1)),
                  pl.BlockSpec((CMP_STRIDE // 2, 2 * LANES, 2 * LANES), lambda i: (0, 0, 0))],
        out_specs=pl.BlockSpec((tm // CMP_STRIDE, 4 * LANES), lambda i: (i, 0)),
        compiler_params=_cparams(("parallel",)), name="cmp_subproj")(rows2d, rows2d, w)


_PAGES_PER_STEP = 32


def _subproj_pages_body(pt_ref, *refs):
    del pt_ref
    npg = len(refs) - 3
    w_ref, o_ref, rows_scr = refs[npg], refs[npg + 1], refs[npg + 2]
    for p in range(npg):
        for h in range(NSA_KV_HEADS):
            rows_scr[h, p * PAGE_SIZE:(p + 1) * PAGE_SIZE, :] = refs[p][0, 0, h].reshape(2 * HEAD_DIM, PAGE_SIZE).T
    _subproj_accumulate(lambda h, s, n: rows_scr[h, pl.ds(s, n, stride=CMP_STRIDE), :], w_ref, o_ref)


def _subproj_pages(cache_t, page_table, layer, w):
    B, n_pages = page_table.shape
    npg = min(_PAGES_PER_STEP, n_pages)
    spp = PAGE_SIZE // CMP_STRIDE

    def page_spec(p):
        return pl.BlockSpec((1, 1, NSA_KV_HEADS, 2, HEAD_DIM, PAGE_SIZE),
                            lambda b, g, pt: (layer, pt[b, g * npg + p], 0, 0, 0, 0))

    gs = pltpu.PrefetchScalarGridSpec(
        num_scalar_prefetch=1, grid=(B, n_pages // npg),
        in_specs=[page_spec(p) for p in range(npg)]
        + [pl.BlockSpec((CMP_STRIDE // 2, 2 * LANES, 2 * LANES), lambda b, g, pt: (0, 0, 0))],
        out_specs=pl.BlockSpec((1, npg * spp, 4 * LANES), lambda b, g, pt: (b, g, 0)),
        scratch_shapes=[pltpu.VMEM((NSA_KV_HEADS, npg * PAGE_SIZE, LANES), F32)])
    return pl.pallas_call(
        _subproj_pages_body, out_shape=jax.ShapeDtypeStruct((B, n_pages * spp, 4 * LANES), F32), grid_spec=gs,
        compiler_params=_cparams(("parallel", "arbitrary"), 48), name="cmp_subproj_pages")(
            page_table, *([cache_t] * npg), w)


def _cmp_mlp_body(h_ref, w2_ref, g_ref, o_ref):
    x = h_ref[0]
    kv = _mm(jax.nn.gelu(x), w2_ref[...])
    g = g_ref[...]
    segs = []
    for j in range(4):
        seg = kv[:, j * HEAD_DIM:(j + 1) * HEAD_DIM]
        if j % 2 == 0:
            seg = seg * lax.rsqrt(jnp.mean(seg * seg, -1, keepdims=True) + NORM_EPS) * g
        segs.append(seg)
    o_ref[0] = jnp.concatenate(segs, 1)


def _cmp_mlp(hid, w2bd, g):
    B, n, _ = hid.shape
    tn = min(512, n)
    return pl.pallas_call(
        _cmp_mlp_body, out_shape=jax.ShapeDtypeStruct((B, n, 256), F32), grid=(B, n // tn),
        in_specs=[pl.BlockSpec((1, tn, 256), lambda b, i: (b, i, 0)), pl.BlockSpec((256, 256), lambda b, i: (0, 0)),
                  pl.BlockSpec((1, HEAD_DIM), lambda b, i: (0, 0))],
        out_specs=pl.BlockSpec((1, tn, 256), lambda b, i: (b, i, 0)),
        compiler_params=_cparams(("parallel", "parallel")), name="cmp_mlp")(hid, w2bd, g.reshape(1, HEAD_DIM))


def _cmp_attn_body(q_ref, kv_ref, cov_ref, o_ref, imp_ref, *, pos0):
    qi = pl.program_id(2)
    tq = q_ref.shape[1]
    n = kv_ref.shape[1]
    kv = kv_ref[0]
    k = kv[:, :HEAD_DIM]
    v = kv[:, HEAD_DIM:]
    q = q_ref[0]
    pos = pos0 + qi * tq + lax.broadcasted_iota(jnp.int32, (tq, 1), 0)
    end = lax.broadcasted_iota(jnp.int32, (1, n), 1) * CMP_STRIDE + (CMP_BLOCK - 1)
    mask = end <= pos
    psum = jnp.zeros((tq, n), F32)
    outs = []
    for g in range(2):
        s = _mm_nt(q[:, g * HEAD_DIM:(g + 1) * HEAD_DIM], k) * SCALE
        s = jnp.where(mask, s, -jnp.inf)
        m = jnp.max(s, -1, keepdims=True)
        e = jnp.exp(s - jnp.where(m == -jnp.inf, 0.0, m))
        p = e / jnp.maximum(jnp.sum(e, -1, keepdims=True), 1e-30)
        outs.append(_mm(p, v))
        psum = psum + p
    o_ref[0] = jnp.concatenate(outs, 1)
    hi = psum.astype(BF16)
    lo = (psum - hi.astype(F32)).astype(BF16)
    cov = cov_ref[...]
    imp_ref[0, 0] = _mm(hi, cov) + _mm(lo, cov)


def _cmp_attn(qn, kv_cmp, cover, pos0, tq):
    B, T, _ = qn.shape
    n = kv_cmp.shape[1]
    nbp = cover.shape[1]
    return pl.pallas_call(
        functools.partial(_cmp_attn_body, pos0=pos0),
        out_shape=(jax.ShapeDtypeStruct((B, T, GW), F32), jax.ShapeDtypeStruct((B, NSA_KV_HEADS, T, nbp), F32)),
        grid=(B, NSA_KV_HEADS, T // tq),
        in_specs=[pl.BlockSpec((1, tq, LANES), lambda b, h, i: (b, i, h)),
                  pl.BlockSpec((1, n, LANES), lambda b, h, i: (b, 0, h)),
                  pl.BlockSpec((n, nbp), lambda b, h, i: (0, 0))],
        out_specs=(pl.BlockSpec((1, tq, LANES), lambda b, h, i: (b, i, h)),
                   pl.BlockSpec((1, 1, tq, nbp), lambda b, h, i: (b, h, i, 0))),
        compiler_params=_cparams(("parallel", "parallel", "arbitrary")), name="cmp_attn")(qn, kv_cmp, cover)


def _topk_body(imp_ref, sel_ref, idx_ref, *, pos0, t_rows, n_blk):
    ti = pl.program_id(0)
    x = imp_ref[...]
    nbp = x.shape[1]
    xt = jnp.concatenate([x[:, j * LANES:(j + 1) * LANES].T for j in range(nbp // LANES)], 0)
    r = ti * LANES + lax.broadcasted_iota(jnp.int32, (1, LANES), 1)
    cur = (pos0 + r % t_rows) // SLC_BLOCK
    blk = lax.broadcasted_iota(jnp.int32, (nbp, 1), 0)
    forced = (blk == 0) | (blk == cur) | (blk == cur - 1)
    val = jnp.where(forced, jnp.inf, jnp.where(blk <= cur, xt, -jnp.inf))
    val = jnp.where(blk < n_blk, val, -jnp.inf)
    cnt = jnp.zeros((nbp, LANES), F32)
    for i in range(n_blk):
        vi = val[i:i + 1, :]
        ahead = (vi > val) | ((vi == val) & (blk > i))
        cnt = cnt + jnp.where(ahead, 1.0, 0.0)
    chosen = (cnt < float(N_SELECT)) & (val > -jnp.inf)
    self32 = jnp.where(chosen, 1.0, 0.0)
    sel_ref[...] = jnp.concatenate([self32[j * LANES:(j + 1) * LANES, :].T for j in range(nbp // LANES)], 1)
    blk_f = blk.astype(F32)
    rows = []
    for j in range(N_SELECT):
        hit = chosen & (cnt == float(j))
        rows.append(jnp.sum(jnp.where(hit, blk_f + 1.0, 0.0), 0, keepdims=True) - 1.0)
    idx_ref[...] = jnp.concatenate(rows, 0).astype(jnp.int32)


def _topk(imp2d, pos0, t_rows, n_blk):
    R, nbp = imp2d.shape
    return pl.pallas_call(
        functools.partial(_topk_body, pos0=pos0, t_rows=t_rows, n_blk=n_blk),
        out_shape=(jax.ShapeDtypeStruct((R, nbp), F32), jax.ShapeDtypeStruct((N_SELECT, R), jnp.int32)),
        grid=(R // LANES,),
        in_specs=[pl.BlockSpec((LANES, nbp), lambda i: (i, 0))],
        out_specs=(pl.BlockSpec((LANES, nbp), lambda i: (i, 0)), pl.BlockSpec((N_SELECT, LANES), lambda i: (0, i))),
        compiler_params=_cparams(("parallel",)), name="topk")(imp2d)


def _attn_body(*refs, mode, pairs, tk, pos_q0, pos_k0, scale):
    if mode == 'slc':
        q_ref, kv_ref, sel_ref, blk_ref, o_ref = refs
    else:
        q_ref, kv_ref, o_ref = refs
    qi = pl.program_id(2)
    tq = q_ref.shape[1]
    n_k = kv_ref.shape[1]
    n_tiles = n_k // tk
    q = q_ref[0]
    pq0 = pos_q0 + qi * tq
    qpos = pq0 + lax.broadcasted_iota(jnp.int32, (tq, 1), 0)
    unroll = 2 if n_tiles % 2 == 0 else 1
    if mode == 'none':
        lo, hi = 0, n_tiles // unroll
    else:
        r_hi = jnp.minimum(n_k - 1, pq0 + tq - 1 - pos_k0)
        hi = r_hi // (tk * unroll) + 1
        lo = jnp.maximum(0, pq0 - (WINDOW - 1) - pos_k0) // (tk * unroll) if mode == 'win' else 0
    qs = [q[:, q_lo:q_lo + HEAD_DIM] * scale for q_lo, _, _ in pairs]
    if mode == 'slc':
        nbp = sel_ref.shape[3]
        sel_bias = ((sel_ref[0, 0] - 1.0) * (-NEG_BIG)).astype(BF16)
        qs = [jnp.concatenate([sel_bias, qh.astype(BF16)], 1) for qh in qs]

    def step(it, carry, causal=True):
        tiles = []
        for u in range(unroll):
            r0 = pl.multiple_of((it * unroll + u) * tk, tk)
            kvt = kv_ref[0, pl.ds(r0, tk), :]
            kpos = pos_k0 + r0 + lax.broadcasted_iota(jnp.int32, (1, tk), 1)
            onehot = None
            if mode == 'win':
                d = qpos - kpos
                mask = (d >= 0) & (d < WINDOW) & (kpos >= 0)
            elif mode == 'slc':
                onehot = blk_ref[pl.ds(r0, tk), :]
                mask = (kpos <= qpos) if causal else None
            else:
                mask = None
            tiles.append((kvt, mask, onehot))
        new = [None] * (3 * len(pairs))

        def chain(p):
            _, k_lo, v_lo = pairs[p]
            m, l, acc = carry[3 * p:3 * p + 3]
            if mode == 'slc':
                ss = [_mm_nt(qs[p], jnp.concatenate([oh, kvt[:, k_lo:k_lo + HEAD_DIM].astype(BF16)], 1))
                      for kvt, _, oh in tiles]
            else:
                ss = [_mm_nt(qs[p], kvt[:, k_lo:k_lo + HEAD_DIM]) for kvt, _, _ in tiles]
            yield
            ss = [s if mask is None else jnp.where(mask, s, -jnp.inf) for s, (_, mask, _) in zip(ss, tiles)]
            m_new = m
            for s in ss:
                m_new = jnp.maximum(m_new, jnp.max(s, -1, keepdims=True))
            alpha = jnp.exp(m - m_new)
            prs = [jnp.exp(s - m_new) for s in ss]
            yield
            l = alpha * l
            acc = alpha * acc
            for pr, (kvt, _, _) in zip(prs, tiles):
                l = l + jnp.sum(pr, -1, keepdims=True)
                acc = acc + _mm(pr, kvt[:, v_lo:v_lo + HEAD_DIM])
            new[3 * p:3 * p + 3] = [m_new, l, acc]

        _run_interleaved([chain(p) for p in range(len(pairs))])
        return tuple(new)

    init = []
    for _ in pairs:
        init += [jnp.full((tq, 1), NEG_BIG, F32), jnp.zeros((tq, 1), F32), jnp.zeros((tq, HEAD_DIM), F32)]
    if mode == 'slc':
        hi_full = jnp.minimum(hi, (pq0 - pos_k0 + 1) // (tk * unroll))
        res = lax.fori_loop(lo, hi_full, functools.partial(step, causal=False), tuple(init))
        res = lax.fori_loop(hi_full, hi, step, res)
    else:
        res = lax.fori_loop(lo, hi, step, tuple(init))
    outs = []
    for p in range(len(pairs)):
        m, l, acc = res[3 * p:3 * p + 3]
        outs.append(jnp.where(m > 0.5 * NEG_BIG, acc / jnp.maximum(l, 1e-30), 0.0))
    o_ref[0] = jnp.concatenate(outs, 1)


def _attn(q, kv, mode, pairs, kv_width, pos_q0, pos_k0, tq, tk, sel=None):
    B, T, _ = q.shape
    n_k = kv.shape[1]
    in_specs = [pl.BlockSpec((1, tq, LANES), lambda b, h, i: (b, i, h)),
                pl.BlockSpec((1, n_k, kv_width), lambda b, h, i: (b, 0, h))]
    args = [q, kv]
    if mode == 'slc':
        nbp = sel.shape[-1]
        in_specs.append(pl.BlockSpec((1, 1, tq, nbp), lambda b, h, i: (b, h, i, 0)))
        args.append(sel)
        key_blk = (pos_k0 + np.arange(n_k)) // SLC_BLOCK
        in_specs.append(pl.BlockSpec((n_k, nbp), lambda b, h, i: (0, 0)))
        args.append(jnp.asarray(key_blk[:, None] == np.arange(nbp)[None, :], dtype=BF16))
    return pl.pallas_call(
        functools.partial(_attn_body, mode=mode, pairs=pairs, tk=tk, pos_q0=pos_q0, pos_k0=pos_k0, scale=SCALE),
        out_shape=jax.ShapeDtypeStruct((B, T, GW), F32), grid=(B, 2, T // tq),
        in_specs=in_specs, out_specs=pl.BlockSpec((1, tq, LANES), lambda b, h, i: (b, i, h)),
        compiler_params=_cparams(("parallel", "parallel", "arbitrary")), name="attn_" + mode)(*args)


_GQA_PAIRS = ((0, 0, HEAD_DIM), (HEAD_DIM, 0, HEAD_DIM))
_MHA_PAIRS = ((0, 0, HEAD_DIM), (HEAD_DIM, 2 * HEAD_DIM, 3 * HEAD_DIM))


def _slc_paged_body(idx_ref, phys_ref, *refs, pos0, blk0, t_real):
    del phys_ref
    q_ref = refs[0]
    blk_refs = refs[1:1 + N_SELECT]
    new_ref = refs[1 + N_SELECT]
    o_ref = refs[2 + N_SELECT]
    b, h, t = pl.program_id(0), pl.program_id(1), pl.program_id(2)
    base = ((b * NSA_KV_HEADS + h) * t_real + t) * N_SELECT
    qrow = q_ref[0, pl.ds(t, 1), :]
    q2 = jnp.concatenate([qrow[:, :HEAD_DIM], qrow[:, HEAD_DIM:], jnp.zeros((6, HEAD_DIM), F32)], 0)
    pos = pos0 + t
    bpp = PAGE_SIZE // SLC_BLOCK
    tok = lax.broadcasted_iota(jnp.int32, (1, PAGE_SIZE), 1)
    scores, vts = [], []
    n_new = jnp.int32(0)
    for j in range(N_SELECT):
        idx = idx_ref[base + j]
        idc = jnp.maximum(idx, 0)
        kv_t = blk_refs[j][0, 0, 0]
        ok = (idx >= 0) & (idx < blk0) & (tok // SLC_BLOCK == idc % bpp) & ((idc // bpp) * PAGE_SIZE + tok <= pos)
        scores.append(_mm(q2, kv_t[0]) * SCALE + jnp.where(ok, 0.0, -jnp.inf))
        vts.append(kv_t[1])
        n_new = n_new + jnp.where(idx >= blk0, 1, 0)
    newblk = new_ref[0]
    lane = lax.broadcasted_iota(jnp.int32, (1, SLC_BLOCK), 1)
    ok_new = (n_new > 0) & (blk0 * SLC_BLOCK + lane <= pos)
    s_new = _mm_nt(q2, newblk[:, :HEAD_DIM]) * SCALE + jnp.where(ok_new, 0.0, -jnp.inf)
    m = jnp.max(s_new, -1, keepdims=True)
    for s in scores:
        m = jnp.maximum(m, jnp.max(s, -1, keepdims=True))
    m = jnp.where(m == -jnp.inf, 0.0, m)
    e_new = jnp.exp(s_new - m)
    den = jnp.sum(e_new, -1, keepdims=True)
    o = _mm(e_new, newblk[:, HEAD_DIM:])
    for s, vt in zip(scores, vts):
        e = jnp.exp(s - m)
        den = den + jnp.sum(e, -1, keepdims=True)
        o = o + _mm_nt(e, vt)
    o = o / jnp.maximum(den, 1e-30)
    orow = jnp.concatenate([o[0:1], o[1:2]], 1)
    o_ref[0, 0, 0] = jnp.broadcast_to(orow, (8, LANES))


def _slc_paged(q_rot, cache_t, new_rows, idx_flat, page_flat, layer, pos0, blk0, t_real):
    B = q_rot.shape[0]
    tp = q_rot.shape[1]

    def blk_spec(j):
        def imap(b, h, t, idx, page):
            return (layer, page[((b * NSA_KV_HEADS + h) * t_real + t) * N_SELECT + j], h, 0, 0, 0)
        return pl.BlockSpec((1, 1, 1, 2, HEAD_DIM, PAGE_SIZE), imap)

    gs = pltpu.PrefetchScalarGridSpec(
        num_scalar_prefetch=2, grid=(B, NSA_KV_HEADS, t_real),
        in_specs=[pl.BlockSpec((1, tp, LANES), lambda b, h, t, idx, page: (b, 0, h))]
        + [blk_spec(j) for j in range(N_SELECT)]
        + [pl.BlockSpec((1, SLC_BLOCK, LANES), lambda b, h, t, idx, page: (b, 0, h))],
        out_specs=pl.BlockSpec((1, 1, 1, 8, LANES), lambda b, h, t, idx, page: (b, h, t, 0, 0)))
    out = pl.pallas_call(
        functools.partial(_slc_paged_body, pos0=pos0, blk0=blk0, t_real=t_real),
        out_shape=jax.ShapeDtypeStruct((B, NSA_KV_HEADS, t_real, 8, LANES), F32), grid_spec=gs,
        compiler_params=_cparams(("parallel", "parallel", "arbitrary")), name="slc_paged")(
            idx_flat, page_flat, q_rot, *([cache_t] * N_SELECT), new_rows)
    return jnp.transpose(out[:, :, :, 0, :], (0, 2, 1, 3)).reshape(B, t_real, GW)


def _nsa_prep_body(q_ref, s_ref, w_ref, cos_ref, sin_ref, g_ref, ones_ref, qn_ref, qr_ref, so_ref, wo_ref):
    lane = lax.broadcasted_iota(jnp.int32, (1, GW), 1)
    first_half = (lane % HEAD_DIM) < (HEAD_DIM // 2)
    is_k = (lane // HEAD_DIM) % 2 == 0
    cos, sin = cos_ref[...], sin_ref[...]
    ones = ones_ref[...]

    def norm(x, g):
        ms = _mm(x * x, ones, 'l3') * (1.0 / HEAD_DIM)
        return x * lax.rsqrt(ms + NORM_EPS) * g

    def rope(x):
        swapped = jnp.where(first_half, pltpu.roll(x, GW - HEAD_DIM // 2, 1), pltpu.roll(x, HEAD_DIM // 2, 1))
        return x * cos + swapped * sin

    qn = norm(q_ref[0], g_ref[0:1, :])
    qn_ref[0] = qn
    qr_ref[0] = rope(qn)
    for x_ref, o_ref, gi in ((s_ref, so_ref, 1), (w_ref, wo_ref, 2)):
        x = x_ref[0]
        o_ref[0] = jnp.where(is_k, rope(norm(x, g_ref[gi:gi + 1, :])), x)


def _nsa_prep(P3, n_rows, tr, cos, sin, g_blk, ones_bd):
    B = P3.shape[0]
    col = lambda c0: pl.BlockSpec((1, tr, GW), lambda b, i: (b, i, c0 // GW))
    tab = pl.BlockSpec((tr, GW), lambda b, i: (i, 0))
    out = pl.BlockSpec((1, tr, GW), lambda b, i: (b, i, 0))
    shp = jax.ShapeDtypeStruct((B, n_rows, GW), F32)
    return pl.pallas_call(
        _nsa_prep_body, out_shape=(shp, shp, shp, shp), grid=(B, n_rows // tr),
        in_specs=[col(C_NQ), col(C_NSLC), col(C_NWIN), tab, tab, pl.BlockSpec((8, GW), lambda b, i: (0, 0)),
                  pl.BlockSpec((GW, GW), lambda b, i: (0, 0))],
        out_specs=(out, out, out, out),
        compiler_params=_cparams(("parallel", "parallel")), name="nsa_prep")(P3, P3, P3, cos, sin, g_blk, ones_bd)


def _rms(x, g):
    return x * lax.rsqrt(jnp.mean(x * x, -1, keepdims=True) + NORM_EPS) * g


def _rope(x, pos):
    half = HEAD_DIM // 2
    inv = ROPE_THETA ** (-jnp.arange(half, dtype=F32) / half)
    ang = pos.astype(F32)[:, None] * inv
    cos, sin = jnp.cos(ang)[:, None, :], jnp.sin(ang)[:, None, :]
    x1, x2 = x[..., :half], x[..., half:]
    return jnp.concatenate([x1 * cos - x2 * sin, x1 * sin + x2 * cos], -1)


def _pad_t(x, tp, value=0.0):
    t = x.shape[1]
    if t == tp:
        return x
    return jnp.pad(x, ((0, 0), (0, tp - t)) + ((0, 0),) * (x.ndim - 2), constant_values=value)


def _round_up(n, m):
    return -(-n // m) * m


def _cover_matrix(n_cmp, n_cmp_pad, n_slc, nbp):
    start = np.arange(n_cmp_pad)[:, None] * CMP_STRIDE
    blk = np.arange(nbp)[None, :]
    cov = (start < (blk + 1) * SLC_BLOCK) & (start + CMP_BLOCK > blk * SLC_BLOCK)
    cov &= (np.arange(n_cmp_pad)[:, None] < n_cmp) & (blk < n_slc)
    return jnp.asarray(cov.astype(np.float32), dtype=BF16)


def _mlstm_mixer(P, Pp, conv_buf, c0, n0, m0, prm):
    B, T, _ = P.shape
    keep = MLSTM_CONV - 1
    tail = jnp.pad(conv_buf, ((0, 0), (8 - keep, 0), (0, 0)))
    m0p = jnp.pad(m0, ((0, 0), (0, LANES - N_HEADS))).reshape(B, 1, LANES)
    out, C, n, m = _mlstm(Pp, T, tail[..., :GW], tail[..., GW:], prm['mlstm_blk'], c0, n0, m0p)
    qk_raw = jnp.concatenate([P[:, -keep:, C_MQ:C_MQ + GW], P[:, -keep:, C_MK:C_MK + GW]], -1)
    conv_new = jnp.concatenate([conv_buf, qk_raw], 1)[:, -keep:]
    return out[:, :T], C, n, m[:, 0, :N_HEADS], conv_new


def _rwkv_mixer(P, Pp, shift_buf, s0, prm, prec):
    B, T, _ = P.shape
    out, S = _rwkv(Pp, T, shift_buf, prm['rwkv_blk'], prm['rwkv_lora'], s0, prec)
    shift_new = jnp.concatenate([P[:, -1:, C_RR:C_RR + 3 * GW], P[:, -1:, C_RL:C_RL + LANES]], -1)
    return out[:, :T], S, shift_new


def _memory_kv(mem, prm):
    B = mem.shape[0]
    kv = _proj_in(mem.reshape(B * N_MEM, D_MODEL), prm['mem_norm_g'], prm['w_mem_kv_bf16'], 2 * GW)
    kv = kv.reshape(B, N_MEM, MEM_HEADS, 2, HEAD_DIM)
    return jnp.stack([_rms(kv[:, :, :, 0], prm['mem_qk_g'][1]), kv[:, :, :, 1]], 3)


def _memory_mixer(P, mem_kv, prm):
    B, T, _ = P.shape
    tp = _round_up(T, 8)
    qn = _rms(P[..., C_CQ:C_CQ + GW].reshape(B, T, MEM_HEADS, HEAD_DIM), prm['mem_qk_g'][0]).reshape(B, T, GW)
    tq = min(256, tp)
    o = _attn(_pad_t(qn, tp), mem_kv.reshape(B, N_MEM, 2 * GW), 'none', _MHA_PAIRS, 2 * LANES, 0, 0, tq, N_MEM)
    return o[:, :T]


def _rope_tables(pos):
    half = HEAD_DIM // 2
    inv = ROPE_THETA ** (-jnp.arange(half, dtype=F32) / half)
    ang = pos.astype(F32)[:, None] * inv
    cos, sin = jnp.cos(ang), jnp.sin(ang)
    return (jnp.tile(jnp.concatenate([cos, cos], -1), (1, N_HEADS)),
            jnp.tile(jnp.concatenate([-sin, sin], -1), (1, N_HEADS)))


def _nsa_mixer(P2d, Pp, B, T, pos0, win_prefix, prm, past):
    P = P2d.reshape(B, T, DP)
    g = prm['nsa_qk_g']
    tp = _round_up(T, 8)
    tq = min(256, tp)
    cos, sin = _rope_tables(pos0 + jnp.arange(tp, dtype=jnp.int32))
    qn, q_rot, slc_rows, win_rows = _nsa_prep(Pp, tp, tq, cos, sin, prm['nsa_prep_blk'], prm['head_ones'])
    kvrows = lambda t: t[:, :T].reshape(B, T, NSA_KV_HEADS, 2, HEAD_DIM)
    cmp_new = kvrows(P[..., C_NCMP:C_NCMP + 2 * KVW])
    slc_new, win_new = kvrows(slc_rows), kvrows(win_rows)

    L_all = pos0 + T
    n_sub = max(-(-L_all // CMP_STRIDE), CMP_BLOCK // CMP_STRIDE)
    n_cmp = n_sub - 1
    bd, w2bd, pe_hid = prm['cmp_bd'], prm['cmp_w2bd'], prm['cmp_pe_hid']
    if past is None:
        G = _subproj(P2d, C_NCMP // LANES, bd).reshape(B, T // CMP_STRIDE, 4 * LANES)
    else:
        cache_cmp_t, cache_slc_t, page_table, layer = past
        g_pages = _subproj_pages(cache_cmp_t, page_table, layer, bd)
        new_rows = _pad_t(cmp_new.reshape(B, T, 2 * KVW), CMP_STRIDE).reshape(B * CMP_STRIDE, 2 * KVW)
        g_new = _subproj(new_rows, 0, bd).reshape(B, 1, 4 * LANES)
        G = jnp.concatenate([g_pages, g_new], 1)
    n_cmp_pad = _round_up(n_cmp, LANES)
    gb = G[:, 1:, 256:]
    ga = G[:, :, :256]
    fit = lambda t: _pad_t(t, max(n_cmp_pad, t.shape[1]))[:, :n_cmp_pad]
    hid = fit(ga) + fit(gb) + pe_hid
    kv_cmp = _cmp_mlp(hid, w2bd, g[1])
    n_slc = -(-L_all // SLC_BLOCK)
    nbp = _round_up(n_slc, LANES)
    cover = _cover_matrix(n_cmp, n_cmp_pad, n_slc, nbp)
    tq = min(256, tp)
    o_cmp, imp = _cmp_attn(qn, kv_cmp, cover, pos0, tq)

    R = B * NSA_KV_HEADS * tp
    rp = _round_up(R, LANES)
    imp2d = jnp.pad(imp.reshape(R, nbp), ((0, rp - R), (0, 0)))
    sel, idx_t = _topk(imp2d, pos0, tp, n_slc)

    if past is None:
        sel4 = sel[:R].reshape(B, NSA_KV_HEADS, tp, nbp)
        o_slc = _attn(q_rot, slc_new.reshape(B, T, 2 * KVW), 'slc', _GQA_PAIRS, LANES, pos0, pos0, tq,
                      min(256, T), sel=sel4)
    else:
        bpp = PAGE_SIZE // SLC_BLOCK
        idx = idx_t[:, :R].T.reshape(B, NSA_KV_HEADS, tp, N_SELECT)[:, :, :T]
        idc = jnp.clip(idx, 0, page_table.shape[1] * bpp - 1)
        page = page_table[jnp.arange(B)[:, None, None, None], idc // bpp]
        new_rows = _pad_t(slc_new.reshape(B, T, 2 * KVW), SLC_BLOCK)
        o_slc = _slc_paged(q_rot, cache_slc_t, new_rows, idx.reshape(-1), page.reshape(-1).astype(jnp.int32),
                           layer, pos0, pos0 // SLC_BLOCK, T)

    if win_prefix.shape[1] == 0:
        win_ctx = win_new
        pos_k0 = pos0
    else:
        win_ctx = jnp.concatenate([win_prefix, win_new], 1)
        pos_k0 = pos0 - win_prefix.shape[1]
    n_k = win_ctx.shape[1]
    tkw = min(256, _round_up(n_k, LANES))
    kv_win = _pad_t(win_ctx.reshape(B, n_k, 2 * KVW), _round_up(n_k, tkw))
    o_win = _attn(q_rot, kv_win, 'win', _GQA_PAIRS, LANES, pos0, pos_k0, tq, tkw)

    keep = win_prefix.shape[1] if past is not None else min(WINDOW, T)
    flat = lambda o: o[:, :T].reshape(B * T, GW)
    return (flat(o_cmp), flat(o_slc), flat(o_win)), cmp_new, slc_new, win_ctx[:, -keep:]


RWKV_PREC = ('r3', 'bf16', 'bf16')


def _layer(x, pos0, st, mem_kv, prm, past, rwkv_prec=RWKV_PREC):
    conv_buf, c0, n0, m0, s0, shift_buf, win_prefix = st
    B, T, _ = x.shape
    x2d = x.reshape(B * T, D_MODEL)
    P2d = _proj_in(x2d, prm['norm_g'], prm['w_in_bf16'], 640)
    P = P2d.reshape(B, T, DP)
    Pp = _pad_t(P, _round_up(T, max(MLSTM_L, RWKV_L)))
    y_m, C, n, m, conv_new = _mlstm_mixer(P, Pp, conv_buf, c0, n0, m0, prm)
    (o_cmp, o_slc, o_win), cmp_new, slc_new, win_new = _nsa_mixer(P2d, Pp, B, T, pos0, win_prefix, prm, past)
    y_r, S, shift_new = _rwkv_mixer(P, Pp, shift_buf, s0, prm, rwkv_prec)
    o_mem = _memory_mixer(P, mem_kv, prm)
    flat = lambda t: t.reshape(B * T, GW)
    out = _proj_out(x2d, P2d, flat(y_m), flat(y_r), o_cmp, o_slc, o_win, flat(o_mem), prm['gate_expand'],
                    prm['w_out_bf16']).reshape(B, T, D_MODEL)
    return out, (cmp_new, slc_new, win_new, C, n, m, conv_new, S, shift_new)


def _prep_params(l, p):
    prm = {k: v[l] for k, v in p.items()}
    src = jnp.asarray(np.maximum(_SRC, 0), jnp.int32)
    keep = jnp.asarray((_SRC >= 0).astype(np.float32))
    prm['w_in_bf16'] = (jnp.take(prm['w_in'], src, axis=1) * keep).astype(BF16)
    prm['w_out_bf16'] = prm['w_out'].astype(BF16)
    wm = prm['w_mem_kv'].reshape(D_MODEL, 2, MEM_HEADS, HEAD_DIM)
    prm['w_mem_kv_bf16'] = jnp.transpose(wm, (0, 2, 1, 3)).reshape(D_MODEL, 2 * GW).astype(BF16)
    rows = lambda *vs: jnp.concatenate([jnp.pad(v.reshape(-1, v.shape[-1]), ((0, 0), (0, GW - v.shape[-1])))
                                        for v in vs], 0)
    pad16 = lambda blk: jnp.pad(blk, ((0, 16 - blk.shape[0]), (0, 0)))
    cw, cb = prm['mlstm_conv_w'], prm['mlstm_conv_b']
    prm['mlstm_blk'] = pad16(rows(cw[:, :GW], cw[:, GW:], cb[:GW], cb[GW:], prm['mlstm_norm_g'],
                                  prm['mlstm_gate_b'].reshape(1, 2 * N_HEADS)))
    mu = prm['rwkv_mu']
    prm['rwkv_blk'] = pad16(rows(mu[:GW], mu[GW:2 * GW], mu[2 * GW:3 * GW], mu[3 * GW:], prm['rwkv_w0'],
                                 prm['rwkv_a0'], prm['rwkv_kk'], prm['rwkv_ln']))
    g = prm['nsa_qk_g']
    one = jnp.ones((HEAD_DIM,), F32)
    prm['nsa_prep_blk'] = jnp.pad(rows(jnp.tile(g[0], N_HEADS), jnp.concatenate([g[2], one, g[2], one]),
                                       jnp.concatenate([g[3], one, g[3], one])), ((0, 5), (0, 0)))
    head_of = np.arange(GW) // HEAD_DIM
    prm['head_ones'] = jnp.asarray(head_of[:, None] == head_of[None, :], dtype=BF16)
    gate_lane = np.arange(LANES)[:, None]
    out_lane = np.arange(3 * GW)[None, :]
    prm['gate_expand'] = jnp.asarray(gate_lane == (out_lane // GW) * N_HEADS + (out_lane % GW) // HEAD_DIM, dtype=BF16)
    zl = jnp.zeros((DECAY_LORA, GW), F32)
    prm['rwkv_lora'] = jnp.concatenate([jnp.concatenate([prm['rwkv_w2'], zl], 1),
                                        jnp.concatenate([zl, prm['rwkv_a2']], 1)], 0)
    w1 = prm['nsa_cmp_w1']
    eye_h = jnp.eye(NSA_KV_HEADS, dtype=F32)
    eye_c = jnp.eye(2, dtype=F32)
    w1r = w1.reshape(2, 2, CMP_STRIDE, HEAD_DIM, HEAD_DIM)
    bd = jnp.einsum('cC,crsde->scdrCe', eye_c, w1r)
    prm['cmp_bd'] = bd.reshape(CMP_STRIDE // 2, 2 * LANES, 2 * LANES).astype(BF16)
    w2 = prm['nsa_cmp_w2']
    prm['cmp_w2bd'] = jnp.einsum('hH,cC,ced->hceHCd', eye_h, eye_c, w2).reshape(2 * KVW, 2 * KVW)
    pe_hid = jnp.einsum('csd,csde->ce', prm['nsa_pe'], w1, precision=HI)
    prm['cmp_pe_hid'] = jnp.tile(pe_hid.reshape(1, 2 * HEAD_DIM), (1, NSA_KV_HEADS)).reshape(2 * KVW)
    return prm


def kernel(x_prompt, x_sample, cache_cmp_kv, cache_slc_kv, cache_win_kv, cache_mem_kv, state_mlstm_C, state_mlstm_n, state_mlstm_m, state_mlstm_conv, state_rwkv_S, state_rwkv_shift, page_table, mem_prompt, norm_g, w_in, w_out, mlstm_conv_w, mlstm_conv_b, mlstm_gate_b, mlstm_norm_g, nsa_qk_g, nsa_pe, nsa_cmp_w1, nsa_cmp_w2, rwkv_mu, rwkv_w0, rwkv_w2, rwkv_a0, rwkv_a2, rwkv_kk, rwkv_ln, mem_norm_g, w_mem_kv, mem_qk_g):
    params = dict(norm_g=norm_g, w_in=w_in, w_out=w_out, mlstm_conv_w=mlstm_conv_w, mlstm_conv_b=mlstm_conv_b,
                  mlstm_gate_b=mlstm_gate_b, mlstm_norm_g=mlstm_norm_g, nsa_qk_g=nsa_qk_g, nsa_pe=nsa_pe,
                  nsa_cmp_w1=nsa_cmp_w1, nsa_cmp_w2=nsa_cmp_w2, rwkv_mu=rwkv_mu, rwkv_w0=rwkv_w0, rwkv_w2=rwkv_w2,
                  rwkv_a0=rwkv_a0, rwkv_a2=rwkv_a2, rwkv_kk=rwkv_kk, rwkv_ln=rwkv_ln, mem_norm_g=mem_norm_g,
                  w_mem_kv=w_mem_kv, mem_qk_g=mem_qk_g)
    depth = norm_g.shape[0]
    B = x_prompt.shape[0]
    past_len = page_table.shape[1] * PAGE_SIZE
    cache_cmp_t = jnp.transpose(cache_cmp_kv, (0, 1, 3, 4, 5, 2))
    cache_slc_t = jnp.transpose(cache_slc_kv, (0, 1, 3, 4, 5, 2))
    xp, xs = x_prompt, x_sample
    new_p, new_s, new_mem = [], [], []
    for l in range(depth):
        prm = _prep_params(l, params)
        mem_kv_p = _memory_kv(mem_prompt, prm)
        st_p = (jnp.zeros((B, MLSTM_CONV - 1, 2 * GW), F32),
                jnp.zeros((B, N_HEADS, HEAD_DIM, HEAD_DIM), F32),
                jnp.zeros((B, N_HEADS, HEAD_DIM), F32),
                jnp.full((B, N_HEADS), M_INIT, F32),
                jnp.zeros((B, N_HEADS, HEAD_DIM, HEAD_DIM), F32),
                jnp.zeros((B, 1, RWKV_SHIFT), F32),
                jnp.zeros((B, 0, NSA_KV_HEADS, 2, HEAD_DIM), F32))
        xp, sp = _layer(xp, 0, st_p, mem_kv_p, prm, None)
        st_s = (state_mlstm_conv[l], state_mlstm_C[l], state_mlstm_n[l], state_mlstm_m[l],
                state_rwkv_S[l], state_rwkv_shift[l], cache_win_kv[l])
        xs, ss = _layer(xs, past_len, st_s, cache_mem_kv[l], prm, (cache_cmp_t, cache_slc_t, page_table, l))
        new_p.append(sp)
        new_s.append(ss)
        new_mem.append(mem_kv_p)
    stack = lambda states, i: jnp.stack([s[i] for s in states])
    outs = [xp, xs]
    for i in range(3):
        outs += [stack(new_p, i), stack(new_s, i)]
    outs.append(jnp.stack(new_mem))
    for i in range(3, 9):
        outs += [stack(new_p, i), stack(new_s, i)]
    return tuple(outs)
```

```python
import functools

import numpy as np
import jax
import jax.numpy as jnp
from jax import lax
from jax.experimental import pallas as pl
from jax.experimental.pallas import tpu as pltpu

F32 = jnp.float32
BF16 = jnp.bfloat16
HI = lax.Precision.HIGHEST

D_MODEL = 1024
PAGE_SIZE = 128
HEAD_DIM = 64
GW = D_MODEL // 4
N_HEADS = GW // HEAD_DIM
SCALE = HEAD_DIM ** -0.5
MLSTM_CONV = 4
M_INIT = -1e30
NSA_KV_HEADS = 2
KVW = NSA_KV_HEADS * HEAD_DIM
CMP_BLOCK = 32
CMP_STRIDE = 16
SLC_BLOCK = 64
N_SELECT = 16
WINDOW = 512
DECAY_LORA = 64
AAA_LORA = 64
RWKV_SHIFT = 3 * GW + DECAY_LORA + AAA_LORA
RWKV_LN_EPS = HEAD_DIM * 1e-5
N_MEM = 256
MEM_HEADS = 4
ROPE_THETA = 10000.0
NORM_EPS = 1e-6

LANES = 128
MLSTM_L = 128
RWKV_L = 64
NEG_BIG = -1e30

_M0, _N0, _R0, _C0 = 0, 1288, 2580, 3732
C_MQ, C_MK, C_MV, C_MO, C_MZ = 0, 256, 512, 768, 1024
C_NQ, C_NCMP, C_NSLC, C_NWIN, C_NZ = 1280, 1536, 1792, 2048, 2304
C_RR, C_RK, C_RV, C_RZ = 2560, 2816, 3072, 3328
C_CQ, C_CZ = 3584, 3840
C_RL = 4096
C_MG = 4224
C_NG = 4352
DP = 4480


def _packed_src():
    src = -np.ones((DP,), np.int64)

    def put(dst, lo, n):
        src[dst:dst + n] = np.arange(lo, lo + n)

    put(C_MQ, _M0, 256); put(C_MK, _M0 + 256, 256); put(C_MV, _M0 + 512, 256)
    put(C_MG, _M0 + 768, 8); put(C_MO, _M0 + 776, 256); put(C_MZ, _M0 + 1032, 256)
    put(C_NQ, _N0, 256)
    for i, base in enumerate((C_NCMP, C_NSLC, C_NWIN)):
        ksrc = _N0 + 256 + 256 * i
        vsrc = ksrc + 128
        for h in range(2):
            put(base + 128 * h, ksrc + 64 * h, 64)
            put(base + 128 * h + 64, vsrc + 64 * h, 64)
    put(C_NG, _N0 + 1024, 12); put(C_NZ, _N0 + 1036, 256)
    put(C_RR, _R0, 256); put(C_RK, _R0 + 256, 256); put(C_RV, _R0 + 512, 256)
    put(C_RL, _R0 + 768, 128); put(C_RZ, _R0 + 896, 256)
    put(C_CQ, _C0, 256); put(C_CZ, _C0 + 256, 256)
    return src


_SRC = _packed_src()


def _split2(a):
    hi = a.astype(BF16)
    return hi, (a - hi.astype(F32)).astype(BF16)


def _dg(a, b, dims, prec):
    dn = (dims, ((), ()))
    if prec == 'bf16':
        return lax.dot_general(a.astype(BF16), b.astype(BF16), dn, preferred_element_type=F32)
    d = lambda x, y: lax.dot_general(x, y, dn, preferred_element_type=F32)
    if prec == 'x3':
        ah, al = _split2(a)
        bh, bl = _split2(b)
        return d(ah, bh) + (d(ah, bl) + d(al, bh))
    if prec in ('r3', 'l3'):
        exact, other = (a, b) if prec == 'r3' else (b, a)
        o1, rest = other.astype(BF16), None
        rest = other - o1.astype(F32)
        o2 = rest.astype(BF16)
        o3 = (rest - o2.astype(F32)).astype(BF16)
        e = exact.astype(BF16)
        if prec == 'r3':
            return d(e, o1) + (d(e, o2) + d(e, o3))
        return d(o1, e) + (d(o2, e) + d(o3, e))
    return lax.dot_general(a, b, dn, preferred_element_type=F32, precision=prec)


def _mm(a, b, prec=None):
    return _dg(a, b, ((1,), (0,)), prec)


def _mm_nt(a, b, prec=None):
    return _dg(a, b, ((1,), (1,)), prec)


def _mm_tn(a, b, prec=None):
    return _dg(a, b, ((0,), (0,)), prec)


def _run_interleaved(chains):
    chains = list(chains)
    while chains:
        alive = []
        for ch in chains:
            try:
                next(ch)
                alive.append(ch)
            except StopIteration:
                pass
        chains = alive


def _cparams(sem, vmem_mb=None):
    kw = dict(dimension_semantics=sem)
    if vmem_mb is not None:
        kw['vmem_limit_bytes'] = vmem_mb * 1024 * 1024
    return pltpu.CompilerParams(**kw)


def _proj_in_body(x_ref, g_ref, w_ref, o_ref, *, tn):
    x = x_ref[...]
    h = (x * lax.rsqrt(jnp.mean(x * x, -1, keepdims=True) + NORM_EPS) * g_ref[...]).astype(BF16)
    for j in range(o_ref.shape[1] // tn):
        o_ref[:, j * tn:(j + 1) * tn] = jnp.dot(h, w_ref[:, j * tn:(j + 1) * tn], preferred_element_type=F32)


def _proj_in(x2d, g, w_bf16, tn):
    n, d = x2d.shape
    dn = w_bf16.shape[1]
    tm = min(512, n)
    return pl.pallas_call(
        functools.partial(_proj_in_body, tn=tn), out_shape=jax.ShapeDtypeStruct((n, dn), F32), grid=(n // tm,),
        in_specs=[pl.BlockSpec((tm, d), lambda i: (i, 0)),
                  pl.BlockSpec((1, d), lambda i: (0, 0)),
                  pl.BlockSpec((d, dn), lambda i: (0, 0))],
        out_specs=pl.BlockSpec((tm, dn), lambda i: (i, 0)),
        compiler_params=_cparams(("parallel",), 56), name="proj_in")(x2d, g.reshape(1, d), w_bf16)


def _silu(x):
    return x * jax.nn.sigmoid(x)


def _proj_out_body(x_ref, ym_ref, yr_ref, oc_ref, os_ref, ow_ref, om_ref, g_ref, zn_ref, zc_ref, e_ref, w_ref, o_ref):
    ge = _mm(jax.nn.sigmoid(g_ref[...]), e_ref[...], 'l3')
    y_n = (ge[:, :GW] * oc_ref[...] + ge[:, GW:2 * GW] * os_ref[...] + ge[:, 2 * GW:] * ow_ref[...]) * _silu(zn_ref[...])
    y_c = om_ref[...] * _silu(zc_ref[...])
    acc = x_ref[...]
    for i, y in enumerate((ym_ref[...], y_n, yr_ref[...], y_c)):
        acc = acc + jnp.dot(y.astype(BF16), w_ref[i * GW:(i + 1) * GW, :], preferred_element_type=F32)
    o_ref[...] = acc


def _proj_out(x2d, P2d, y_m, y_r, o_cmp, o_slc, o_win, o_mem, gate_expand, w_bf16):
    n, d = x2d.shape
    tm = min(512, n)
    yspec = pl.BlockSpec((tm, GW), lambda i: (i, 0))
    return pl.pallas_call(
        _proj_out_body, out_shape=jax.ShapeDtypeStruct((n, d), F32), grid=(n // tm,),
        in_specs=[pl.BlockSpec((tm, d), lambda i: (i, 0)), yspec, yspec, yspec, yspec, yspec, yspec,
                  pl.BlockSpec((tm, LANES), lambda i: (i, C_NG // LANES)),
                  pl.BlockSpec((tm, GW), lambda i: (i, C_NZ // GW)), pl.BlockSpec((tm, GW), lambda i: (i, C_CZ // GW)),
                  pl.BlockSpec((LANES, 3 * GW), lambda i: (0, 0)), pl.BlockSpec((d, d), lambda i: (0, 0))],
        out_specs=pl.BlockSpec((tm, d), lambda i: (i, 0)),
        compiler_params=_cparams(("parallel",), 48), name="proj_out")(
            x2d, y_m, y_r, o_cmp, o_slc, o_win, o_mem, P2d, P2d, P2d, gate_expand, w_bf16)


_MP_CONV_Q, _MP_CONV_K, _MP_BIAS_Q, _MP_BIAS_K, _MP_NORM, _MP_GATE_B = 0, 4, 8, 9, 10, 11


def _mlstm_body(q_ref, k_ref, v_ref, o_ref, z_ref, g_ref, tq_ref, tk_ref, prm_ref, c0_ref, n0_ref, m0_ref,
                h_ref, c_ref, n_ref, m_ref, c_scr, n_scr, m_scr, pq_scr, pk_scr, *, t_valid):
    c = pl.program_id(1)
    nb, L = q_ref.shape[0], q_ref.shape[1]

    @pl.when(c == 0)
    def _():
        c_scr[...] = c0_ref[...]
        n_scr[...] = n0_ref[...]
        m_scr[...] = m0_ref[...]
        pq_scr[...] = jnp.zeros(pq_scr.shape, F32)
        pk_scr[...] = jnp.zeros(pk_scr.shape, F32)
        pq_scr[:, L - 8:L, :] = tq_ref[...]
        pk_scr[:, L - 8:L, :] = tk_ref[...]

    row = lax.broadcasted_iota(jnp.int32, (L, L), 0)
    col = lax.broadcasted_iota(jnp.int32, (L, L), 1)
    causal = row >= col
    tril = causal.astype(F32)
    triu = (row <= col).astype(F32)
    trow = lax.broadcasted_iota(jnp.int32, (L, 1), 0)
    valid = (c * L + trow) < t_valid
    lane = lax.broadcasted_iota(jnp.int32, (1, LANES), 1)
    prm = prm_ref[...]

    def conv(x, prev, w0, b):
        acc = prm[b:b + 1, :] + prm[w0 + MLSTM_CONV - 1:w0 + MLSTM_CONV, :] * x
        for s in range(1, MLSTM_CONV):
            shifted = jnp.where(trow >= s, pltpu.roll(x, s, 0), pltpu.roll(prev, s, 0))
            acc = acc + prm[w0 + MLSTM_CONV - 1 - s:w0 + MLSTM_CONV - s, :] * shifted
        return _silu(acc)

    cums, qs, ks = [], [], []
    for gi in range(nb):
        q_raw, k_raw = q_ref[gi], k_ref[gi]
        qs.append(conv(q_raw, pq_scr[gi], _MP_CONV_Q, _MP_BIAS_Q) * (HEAD_DIM ** -0.5))
        ks.append(conv(k_raw, pk_scr[gi], _MP_CONV_K, _MP_BIAS_K))
        pq_scr[gi] = q_raw
        pk_scr[gi] = k_raw
        x = g_ref[gi] + prm[_MP_GATE_B:_MP_GATE_B + 1, :LANES]
        log_f = jnp.minimum(x, 0.0) - jnp.log1p(jnp.exp(-jnp.abs(x)))
        g = jnp.where(lane < N_HEADS, x, jnp.where(lane < 2 * N_HEADS, log_f, 0.0))
        g = jnp.where(valid, g, jnp.where(lane < N_HEADS, NEG_BIG, 0.0))
        gt = g.T
        cums.append((g, gt, _mm(tril, g, 'r3'), _mm(gt, triu, 'l3')))

    def chain(gi, h):
        g, gt, bc, br = cums[gi]
        sl = slice(h * HEAD_DIM, (h + 1) * HEAD_DIM)
        qh = qs[gi][:, sl]
        kh = ks[gi][:, sl]
        vh = v_ref[gi, :, sl]
        qk = _mm_nt(qh, kh)
        ch = c_scr[gi, h]
        qc = _mm(qh, ch)
        yield
        b_col = bc[:, 4 + h:5 + h]
        li_col = g[:, h:h + 1]
        b_row = br[4 + h:5 + h, :]
        li_row = gt[h:h + 1, :]
        m_prev = m_scr[gi, :, h:h + 1]
        log_d = jnp.where(causal, b_col - b_row + li_row, -jnp.inf)
        log_inter = b_col + m_prev
        m_t = jnp.maximum(jnp.max(log_d, -1, keepdims=True), log_inter)
        s = qk * jnp.exp(log_d - m_t)
        w_inter = jnp.exp(log_inter - m_t)
        nh = n_scr[gi, h:h + 1, :]
        sv = _mm(s, vh)
        yield
        num = sv + w_inter * qc
        den = jnp.sum(s, -1, keepdims=True) + w_inter * jnp.sum(qh * nh, -1, keepdims=True)
        hh = num / jnp.maximum(jnp.abs(den), jnp.exp(-m_t))
        hn = hh * lax.rsqrt(jnp.mean(hh * hh, -1, keepdims=True) + NORM_EPS) * prm[_MP_NORM:_MP_NORM + 1, sl]
        h_ref[gi, :, sl] = jax.nn.sigmoid(o_ref[gi, :, sl]) * hn * _silu(z_ref[gi, :, sl])
        b_end = b_col[L - 1:L, :]
        log_w = b_end - b_col + li_col
        m_new = jnp.maximum(b_end + m_prev, jnp.max(log_w, 0, keepdims=True))
        wk = jnp.exp(log_w - m_new)
        decay = jnp.exp(b_end + m_prev - m_new)
        kw = kh * wk
        c_scr[gi, h] = decay * ch + _mm_tn(kw, vh)
        n_scr[gi, h:h + 1, :] = decay * nh + jnp.sum(kw, 0, keepdims=True)
        m_scr[gi, :, h:h + 1] = m_new

    _run_interleaved([chain(gi, h) for gi in range(nb) for h in range(N_HEADS)])

    @pl.when(c == pl.num_programs(1) - 1)
    def _():
        c_ref[...] = c_scr[...]
        n_ref[...] = n_scr[...]
        m_ref[...] = m_scr[...]


BATCH_ROWS = 4


def _mlstm(P, t_valid, tail_q, tail_k, prm_blk, c0, n0, m0):
    B, tp, _ = P.shape
    L = MLSTM_L
    nb = BATCH_ROWS if B % BATCH_ROWS == 0 else 1
    col = lambda c0_: pl.BlockSpec((nb, L, GW), lambda b, c: (b, c, c0_ // GW))
    tok = pl.BlockSpec((nb, L, GW), lambda b, c: (b, c, 0))
    tail = pl.BlockSpec((nb, 8, GW), lambda b, c: (b, 0, 0))
    sc = pl.BlockSpec((nb, N_HEADS, HEAD_DIM, HEAD_DIM), lambda b, c: (b, 0, 0, 0))
    sn = pl.BlockSpec((nb, N_HEADS, HEAD_DIM), lambda b, c: (b, 0, 0))
    sm = pl.BlockSpec((nb, 1, LANES), lambda b, c: (b, 0, 0))
    return pl.pallas_call(
        functools.partial(_mlstm_body, t_valid=t_valid),
        out_shape=(jax.ShapeDtypeStruct((B, tp, GW), F32), jax.ShapeDtypeStruct(c0.shape, F32),
                   jax.ShapeDtypeStruct(n0.shape, F32), jax.ShapeDtypeStruct(m0.shape, F32)),
        grid=(B // nb, -(-t_valid // L)),
        in_specs=[col(C_MQ), col(C_MK), col(C_MV), col(C_MO), col(C_MZ),
                  pl.BlockSpec((nb, L, LANES), lambda b, c: (b, c, C_MG // LANES)), tail, tail,
                  pl.BlockSpec((16, GW), lambda b, c: (0, 0)), sc, sn, sm],
        out_specs=(tok, sc, sn, sm),
        scratch_shapes=[pltpu.VMEM((nb, N_HEADS, HEAD_DIM, HEAD_DIM), F32), pltpu.VMEM((nb, N_HEADS, HEAD_DIM), F32),
                        pltpu.VMEM((nb, 1, LANES), F32), pltpu.VMEM((nb, L, GW), F32), pltpu.VMEM((nb, L, GW), F32)],
        compiler_params=_cparams(("parallel", "arbitrary")), name="mlstm")(
            P, P, P, P, P, P, tail_q, tail_k, prm_blk, c0, n0, m0)


(_RP_MU_R, _RP_MU_K, _RP_MU_V, _RP_MU_L, _RP_W0, _RP_A0, _RP_KK, _RP_KA, _RP_RK, _RP_LN_G, _RP_LN_B) = range(11)


def _rwkv_body(r_ref, k_ref, v_ref, z_ref, l_ref, sr_ref, sk_ref, sv_ref, sl_ref, prm_ref, lora_ref, s0_ref,
               y_ref, s_ref, s_scr, cr_scr, ck_scr, cv_scr, cl_scr, *, prec, t_valid):
    c_id = pl.program_id(1)

    @pl.when(c_id == 0)
    def _():
        s_scr[...] = s0_ref[...]
        cr_scr[...] = sr_ref[...]
        ck_scr[...] = sk_ref[...]
        cv_scr[...] = sv_ref[...]
        cl_scr[...] = sl_ref[...]

    nb, L = r_ref.shape[0], r_ref.shape[1]
    D = HEAD_DIM
    row = lax.broadcasted_iota(jnp.int32, (L, L), 0)
    col = lax.broadcasted_iota(jnp.int32, (L, L), 1)
    lower = row >= col
    strict = row > col
    tril = lower.astype(F32)
    n_sq = int(np.log2(L)) - 1
    pc, pa, prec = prec
    trow = lax.broadcasted_iota(jnp.int32, (L, 1), 0)
    valid = (c_id * L + trow) < t_valid
    prm = prm_ref[...]
    prow = lambda i, n=GW: prm[i:i + 1, :n]

    def shifted_mix(x_ref, carry_scr, g, mu):
        x = x_ref[g]
        prev = jnp.where(trow >= 1, pltpu.roll(x, 1, 0), carry_scr[g])
        carry_scr[g] = x[L - 1:L, :]
        return x + (prev - x) * mu

    prep = []
    for g in range(nb):
        r = shifted_mix(r_ref, cr_scr, g, prow(_RP_MU_R))
        k = shifted_mix(k_ref, ck_scr, g, prow(_RP_MU_K))
        v = shifted_mix(v_ref, cv_scr, g, prow(_RP_MU_V))
        lo = shifted_mix(l_ref, cl_scr, g, prow(_RP_MU_L, LANES))
        lo_in = jnp.where(lax.broadcasted_iota(jnp.int32, (1, LANES), 1) < DECAY_LORA, jnp.tanh(lo), lo)
        lora = _mm(lo_in, lora_ref[...])
        w_pre = prow(_RP_W0) + lora[:, :GW]
        log_w = -(float(np.exp(-0.5)) * jax.nn.sigmoid(w_pre))
        a = jax.nn.sigmoid(prow(_RP_A0) + lora[:, GW:])
        k_eff = k * (1.0 + (a - 1.0) * prow(_RP_KA))
        kk_raw = k * prow(_RP_KK)
        log_w = jnp.where(valid, log_w, 0.0)
        k_eff = jnp.where(valid, k_eff, 0.0)
        kk_raw = jnp.where(valid, kk_raw, 0.0)
        v = jnp.where(valid, v, 0.0)
        prep.append((r, log_w, k_eff, v, kk_raw, a))

    def chain(g, h):
        sl = slice(h * D, (h + 1) * D)
        r, w, k, v, kk, a = (x[:, sl] for x in prep[g])
        kk = kk / jnp.maximum(jnp.sqrt(jnp.sum(kk * kk, -1, keepdims=True)), 1e-12)
        cum = _mm(tril, w, pc)
        yield
        c_last = cum[L - 1:L, :]
        e_neg = jnp.exp(-cum)
        kh = kk * jnp.exp(cum - w)
        bt = kk * a * e_neg
        kt = k * e_neg
        rh = r * jnp.exp(cum)
        gram = _mm_nt(jnp.concatenate([kh, rh], 0), jnp.concatenate([bt, kt], 0), pa)
        yield
        A = jnp.where(strict, gram[:L, :L], 0.0)
        Bm = jnp.where(strict, gram[:L, L:], 0.0)
        Mb = jnp.where(lower, gram[L:, :L], 0.0)
        Mk = jnp.where(lower, gram[L:, L:], 0.0)
        X = jnp.concatenate([kh, _mm(Bm, v, pa)], 1)
        yield
        Pw = A
        X = X - _mm(Pw, X, pa)
        yield
        for _ in range(n_sq):
            Pw = _mm(Pw, Pw, pa)
            yield
            X = X + _mm(Pw, X, pa)
            yield
        e_end = jnp.exp(c_last - cum)
        bp = kk * a * e_end
        kp = k * e_end
        xtb = _mm_tn(X, bp, prec)
        yield
        wtb = xtb[:D]
        N = _mm_tn(v, kp, prec) - xtb[D:]
        yield
        mbx = _mm(Mb, X, prec)
        yield
        qp = rh - mbx[:, :D]
        y0 = _mm(Mk, v, prec) - mbx[:, D:]
        yield
        s0 = s_scr[g, h]
        y = _mm_nt(qp, s0, prec) + y0
        yield
        s_scr[g, h] = s0 * jnp.exp(c_last) - _mm(s0, wtb, prec) + N
        mu = jnp.mean(y, -1, keepdims=True)
        var = jnp.mean(jnp.square(y - mu), -1, keepdims=True)
        y = (y - mu) * lax.rsqrt(var + RWKV_LN_EPS) * prm[_RP_LN_G:_RP_LN_G + 1, sl] + prm[_RP_LN_B:_RP_LN_B + 1, sl]
        y = y + jnp.sum(r * k * prm[_RP_RK:_RP_RK + 1, sl], -1, keepdims=True) * v
        y_ref[g, :, sl] = y * _silu(z_ref[g, :, sl])

    _run_interleaved([chain(g, h) for g in range(nb) for h in range(N_HEADS)])

    @pl.when(c_id == pl.num_programs(1) - 1)
    def _():
        s_ref[...] = s_scr[...]


def _rwkv(P, t_valid, shift_buf, prm_blk, lora_w, s0, prec):
    B, tp, _ = P.shape
    L = RWKV_L
    nb = BATCH_ROWS if B % BATCH_ROWS == 0 else 1
    col = lambda c0_: pl.BlockSpec((nb, L, GW), lambda b, c: (b, c, c0_ // GW))
    tok = pl.BlockSpec((nb, L, GW), lambda b, c: (b, c, 0))
    car = lambda w: pl.BlockSpec((nb, 1, w), lambda b, c: (b, 0, 0))
    st = pl.BlockSpec((nb, N_HEADS, HEAD_DIM, HEAD_DIM), lambda b, c: (b, 0, 0, 0))
    shifts = [shift_buf[..., i * GW:(i + 1) * GW] for i in range(3)] + [shift_buf[..., 3 * GW:]]
    return pl.pallas_call(
        functools.partial(_rwkv_body, prec=prec, t_valid=t_valid),
        out_shape=(jax.ShapeDtypeStruct((B, tp, GW), F32), jax.ShapeDtypeStruct(s0.shape, F32)),
        grid=(B // nb, -(-t_valid // L)),
        in_specs=[col(C_RR), col(C_RK), col(C_RV), col(C_RZ),
                  pl.BlockSpec((nb, L, LANES), lambda b, c: (b, c, C_RL // LANES)),
                  car(GW), car(GW), car(GW), car(LANES),
                  pl.BlockSpec((16, GW), lambda b, c: (0, 0)), pl.BlockSpec((LANES, 2 * GW), lambda b, c: (0, 0)), st],
        out_specs=(tok, st),
        scratch_shapes=[pltpu.VMEM((nb, N_HEADS, HEAD_DIM, HEAD_DIM), F32), pltpu.VMEM((nb, 1, GW), F32),
                        pltpu.VMEM((nb, 1, GW), F32), pltpu.VMEM((nb, 1, GW), F32), pltpu.VMEM((nb, 1, LANES), F32)],
        compiler_params=_cparams(("parallel", "arbitrary")), name="rwkv")(
            P, P, P, P, P, *shifts, prm_blk, lora_w, s0)


def _subproj_accumulate(load_rows, w_ref, o_ref):
    n = o_ref.shape[-2]
    accs = []
    for h in range(NSA_KV_HEADS):
        acc = jnp.zeros((n, 2 * LANES), F32)
        for s in range(0, CMP_STRIDE, 2):
            xs = jnp.concatenate([load_rows(h, s, n), load_rows(h, s + 1, n)], 1)
            acc = acc + jnp.dot(xs.astype(BF16), w_ref[s // 2], preferred_element_type=F32)
        accs.append(acc)
    out = jnp.concatenate([accs[0][:, :LANES], accs[1][:, :LANES], accs[0][:, LANES:], accs[1][:, LANES:]], 1)
    o_ref[...] = out.reshape(o_ref.shape)


def _subproj_body(x0_ref, x1_ref, w_ref, o_ref):
    xs = (x0_ref, x1_ref)
    _subproj_accumulate(lambda h, s, n: xs[h][pl.ds(s, n, stride=CMP_STRIDE), :], w_ref, o_ref)


def _subproj(rows2d, col0, w):
    n = rows2d.shape[0]
    tm = min(2048, n)
    return pl.pallas_call(
        _subproj_body, out_shape=jax.ShapeDtypeStruct((n // CMP_STRIDE, 4 * LANES), F32), grid=(n // tm,),
        in_specs=[pl.BlockSpec((tm, LANES), lambda i: (i, col0)), pl.BlockSpec((tm, LANES), lambda i: (i, col0 + 1)),
                  pl.BlockSpec((CMP_STRIDE // 2, 2 * LANES, 2 * LANES), lambda i: (0, 0, 0))],
        out_specs=pl.BlockSpec((tm // CMP_STRIDE, 4 * LANES), lambda i: (i, 0)),
        compiler_params=_cparams(("parallel",)), name="cmp_subproj")(rows2d, rows2d, w)


_PAGES_PER_STEP = 32


def _subproj_pages_body(pt_ref, *refs):
    del pt_ref
    npg = len(refs) - 3
    w_ref, o_ref, rows_scr = refs[npg], refs[npg + 1], refs[npg + 2]
    for p in range(npg):
        for h in range(NSA_KV_HEADS):
            rows_scr[h, p * PAGE_SIZE:(p + 1) * PAGE_SIZE, :] = refs[p][0, 0, h].reshape(2 * HEAD_DIM, PAGE_SIZE).T
    _subproj_accumulate(lambda h, s, n: rows_scr[h, pl.ds(s, n, stride=CMP_STRIDE), :], w_ref, o_ref)


def _subproj_pages(cache_t, page_table, layer, w):
    B, n_pages = page_table.shape
    npg = min(_PAGES_PER_STEP, n_pages)
    spp = PAGE_SIZE // CMP_STRIDE

    def page_spec(p):
        return pl.BlockSpec((1, 1, NSA_KV_HEADS, 2, HEAD_DIM, PAGE_SIZE),
                            lambda b, g, pt: (layer, pt[b, g * npg + p], 0, 0, 0, 0))

    gs = pltpu.PrefetchScalarGridSpec(
        num_scalar_prefetch=1, grid=(B, n_pages // npg),
        in_specs=[page_spec(p) for p in range(npg)]
        + [pl.BlockSpec((CMP_STRIDE // 2, 2 * LANES, 2 * LANES), lambda b, g, pt: (0, 0, 0))],
        out_specs=pl.BlockSpec((1, npg * spp, 4 * LANES), lambda b, g, pt: (b, g, 0)),
        scratch_shapes=[pltpu.VMEM((NSA_KV_HEADS, npg * PAGE_SIZE, LANES), F32)])
    return pl.pallas_call(
        _subproj_pages_body, out_shape=jax.ShapeDtypeStruct((B, n_pages * spp, 4 * LANES), F32), grid_spec=gs,
        compiler_params=_cparams(("parallel", "arbitrary"), 48), name="cmp_subproj_pages")(
            page_table, *([cache_t] * npg), w)


def _cmp_mlp_body(h_ref, w2_ref, g_ref, o_ref):
    x = h_ref[0]
    kv = _mm(jax.nn.gelu(x), w2_ref[...])
    g = g_ref[...]
    segs = []
    for j in range(4):
        seg = kv[:, j * HEAD_DIM:(j + 1) * HEAD_DIM]
        if j % 2 == 0:
            seg = seg * lax.rsqrt(jnp.mean(seg * seg, -1, keepdims=True) + NORM_EPS) * g
        segs.append(seg)
    o_ref[0] = jnp.concatenate(segs, 1)


def _cmp_mlp(hid, w2bd, g):
    B, n, _ = hid.shape
    tn = min(512, n)
    return pl.pallas_call(
        _cmp_mlp_body, out_shape=jax.ShapeDtypeStruct((B, n, 256), F32), grid=(B, n // tn),
        in_specs=[pl.BlockSpec((1, tn, 256), lambda b, i: (b, i, 0)), pl.BlockSpec((256, 256), lambda b, i: (0, 0)),
                  pl.BlockSpec((1, HEAD_DIM), lambda b, i: (0, 0))],
        out_specs=pl.BlockSpec((1, tn, 256), lambda b, i: (b, i, 0)),
        compiler_params=_cparams(("parallel", "parallel")), name="cmp_mlp")(hid, w2bd, g.reshape(1, HEAD_DIM))


def _cmp_attn_body(q_ref, kv_ref, cov_ref, o_ref, imp_ref, *, pos0):
    qi = pl.program_id(2)
    tq = q_ref.shape[1]
    n = kv_ref.shape[1]
    kv = kv_ref[0]
    k = kv[:, :HEAD_DIM]
    v = kv[:, HEAD_DIM:]
    q = q_ref[0]
    pos = pos0 + qi * tq + lax.broadcasted_iota(jnp.int32, (tq, 1), 0)
    end = lax.broadcasted_iota(jnp.int32, (1, n), 1) * CMP_STRIDE + (CMP_BLOCK - 1)
    mask = end <= pos
    psum = jnp.zeros((tq, n), F32)
    outs = []
    for g in range(2):
        s = _mm_nt(q[:, g * HEAD_DIM:(g + 1) * HEAD_DIM], k) * SCALE
        s = jnp.where(mask, s, -jnp.inf)
        m = jnp.max(s, -1, keepdims=True)
        e = jnp.exp(s - jnp.where(m == -jnp.inf, 0.0, m))
        p = e / jnp.maximum(jnp.sum(e, -1, keepdims=True), 1e-30)
        outs.append(_mm(p, v))
        psum = psum + p
    o_ref[0] = jnp.concatenate(outs, 1)
    hi = psum.astype(BF16)
    lo = (psum - hi.astype(F32)).astype(BF16)
    cov = cov_ref[...]
    imp_ref[0, 0] = _mm(hi, cov) + _mm(lo, cov)


def _cmp_attn(qn, kv_cmp, cover, pos0, tq):
    B, T, _ = qn.shape
    n = kv_cmp.shape[1]
    nbp = cover.shape[1]
    return pl.pallas_call(
        functools.partial(_cmp_attn_body, pos0=pos0),
        out_shape=(jax.ShapeDtypeStruct((B, T, GW), F32), jax.ShapeDtypeStruct((B, NSA_KV_HEADS, T, nbp), F32)),
        grid=(B, NSA_KV_HEADS, T // tq),
        in_specs=[pl.BlockSpec((1, tq, LANES), lambda b, h, i: (b, i, h)),
                  pl.BlockSpec((1, n, LANES), lambda b, h, i: (b, 0, h)),
                  pl.BlockSpec((n, nbp), lambda b, h, i: (0, 0))],
        out_specs=(pl.BlockSpec((1, tq, LANES), lambda b, h, i: (b, i, h)),
                   pl.BlockSpec((1, 1, tq, nbp), lambda b, h, i: (b, h, i, 0))),
        compiler_params=_cparams(("parallel", "parallel", "arbitrary")), name="cmp_attn")(qn, kv_cmp, cover)


def _topk_body(imp_ref, sel_ref, idx_ref, cnt_scr, *, pos0, t_rows, n_blk, want_idx):
    ti = pl.program_id(0)
    x = imp_ref[...]
    nbp = x.shape[1]
    nbu = cnt_scr.shape[0]
    xt = jnp.concatenate([x[:, j * LANES:(j + 1) * LANES].T for j in range(nbp // LANES)], 0)[:nbu]
    r = ti * LANES + lax.broadcasted_iota(jnp.int32, (1, LANES), 1)
    cur = (pos0 + r % t_rows) // SLC_BLOCK
    t_last = (ti * LANES) % t_rows + LANES - 1 if t_rows % LANES == 0 else t_rows - 1
    cur_max = (pos0 + t_last) // SLC_BLOCK
    blk = lax.broadcasted_iota(jnp.int32, (nbu, 1), 0)
    forced = (blk == 0) | (blk == cur) | (blk == cur - 1)
    val = jnp.where(forced, jnp.inf, jnp.where(blk <= cur, xt, -jnp.inf))
    val = jnp.where(blk < n_blk, val, -jnp.inf)
    cnt_scr[...] = jnp.zeros(cnt_scr.shape, F32)
    for g0 in range(0, n_blk, 8):
        @pl.when(g0 <= cur_max)
        def _():
            cnt = cnt_scr[...]
            for i in range(g0, min(g0 + 8, n_blk)):
                vi = val[i:i + 1, :]
                ahead = (vi > val) | ((vi == val) & (blk > i))
                cnt = cnt + jnp.where(ahead, 1.0, 0.0)
            cnt_scr[...] = cnt
    cnt = cnt_scr[...]
    chosen = (cnt < float(N_SELECT)) & (val > -jnp.inf)
    self32 = jnp.where(chosen, 1.0, 0.0)
    if nbu < nbp:
        self32 = jnp.concatenate([self32, jnp.zeros((nbp - nbu, LANES), F32)], 0)
    sel_ref[...] = jnp.concatenate([self32[j * LANES:(j + 1) * LANES, :].T for j in range(nbp // LANES)], 1)
    if want_idx:
        blk_f = blk.astype(F32)
        rows = []
        for j in range(N_SELECT):
            hit = chosen & (cnt == float(j))
            rows.append(jnp.sum(jnp.where(hit, blk_f + 1.0, 0.0), 0, keepdims=True) - 1.0)
        idx_ref[...] = jnp.concatenate(rows, 0).astype(jnp.int32)
    else:
        idx_ref[...] = jnp.zeros(idx_ref.shape, jnp.int32)


def _topk(imp2d, pos0, t_rows, n_blk, want_idx):
    R, nbp = imp2d.shape
    return pl.pallas_call(
        functools.partial(_topk_body, pos0=pos0, t_rows=t_rows, n_blk=n_blk, want_idx=want_idx),
        out_shape=(jax.ShapeDtypeStruct((R, nbp), F32), jax.ShapeDtypeStruct((N_SELECT, R), jnp.int32)),
        scratch_shapes=[pltpu.VMEM((_round_up(n_blk, 8), LANES), F32)],
        grid=(R // LANES,),
        in_specs=[pl.BlockSpec((LANES, nbp), lambda i: (i, 0))],
        out_specs=(pl.BlockSpec((LANES, nbp), lambda i: (i, 0)), pl.BlockSpec((N_SELECT, LANES), lambda i: (0, i))),
        compiler_params=_cparams(("parallel",)), name="topk")(imp2d)


def _attn_body(*refs, mode, pairs, tk, pos_q0, pos_k0, scale):
    if mode == 'slc':
        q_ref, kv_ref, sel_ref, blk_ref, o_ref = refs
    else:
        q_ref, kv_ref, o_ref = refs
    qi = pl.program_id(2)
    tq = q_ref.shape[1]
    n_k = kv_ref.shape[1]
    n_tiles = n_k // tk
    q = q_ref[0]
    pq0 = pos_q0 + qi * tq
    qpos = pq0 + lax.broadcasted_iota(jnp.int32, (tq, 1), 0)
    unroll = 2 if n_tiles % 2 == 0 else 1
    if mode == 'none':
        lo, hi = 0, n_tiles // unroll
    else:
        r_hi = jnp.minimum(n_k - 1, pq0 + tq - 1 - pos_k0)
        hi = r_hi // (tk * unroll) + 1
        lo = jnp.maximum(0, pq0 - (WINDOW - 1) - pos_k0) // (tk * unroll) if mode == 'win' else 0
    qs = [q[:, q_lo:q_lo + HEAD_DIM] * scale for q_lo, _, _ in pairs]
    if mode == 'slc':
        nbp = sel_ref.shape[3]
        sel_bias = ((sel_ref[0, 0] - 1.0) * (-NEG_BIG)).astype(BF16)
        qs = [jnp.concatenate([sel_bias, qh.astype(BF16)], 1) for qh in qs]

    def step(it, carry, causal=True):
        tiles = []
        for u in range(unroll):
            r0 = pl.multiple_of((it * unroll + u) * tk, tk)
            kvt = kv_ref[0, pl.ds(r0, tk), :]
            kpos = pos_k0 + r0 + lax.broadcasted_iota(jnp.int32, (1, tk), 1)
            onehot = None
            if mode == 'win':
                d = qpos - kpos
                mask = (d >= 0) & (d < WINDOW) & (kpos >= 0)
            elif mode == 'slc':
                onehot = blk_ref[pl.ds(r0, tk), :]
                mask = (kpos <= qpos) if causal else None
            else:
                mask = None
            tiles.append((kvt, mask, onehot))
        new = [None] * (3 * len(pairs))

        def chain(p):
            _, k_lo, v_lo = pairs[p]
            m, l, acc = carry[3 * p:3 * p + 3]
            if mode == 'slc':
                ss = [_mm_nt(qs[p], jnp.concatenate([oh, kvt[:, k_lo:k_lo + HEAD_DIM].astype(BF16)], 1))
                      for kvt, _, oh in tiles]
            else:
                ss = [_mm_nt(qs[p], kvt[:, k_lo:k_lo + HEAD_DIM]) for kvt, _, _ in tiles]
            yield
            ss = [s if mask is None else jnp.where(mask, s, -jnp.inf) for s, (_, mask, _) in zip(ss, tiles)]
            m_new = m
            for s in ss:
                m_new = jnp.maximum(m_new, jnp.max(s, -1, keepdims=True))
            alpha = jnp.exp(m - m_new)
            m_wide = jnp.concatenate([m_new] * (tk // LANES), 1)
            prs = [jnp.exp(s - m_wide) for s in ss]
            yield
            l = alpha * l
            acc = alpha[:, :HEAD_DIM] * acc
            for pr, (kvt, _, _) in zip(prs, tiles):
                l = l + jnp.sum(pr, -1, keepdims=True)
                acc = acc + _mm(pr, kvt[:, v_lo:v_lo + HEAD_DIM])
            new[3 * p:3 * p + 3] = [m_new, l, acc]

        _run_interleaved([chain(p) for p in range(len(pairs))])
        return tuple(new)

    init = []
    for _ in pairs:
        init += [jnp.full((tq, LANES), NEG_BIG, F32), jnp.zeros((tq, LANES), F32), jnp.zeros((tq, HEAD_DIM), F32)]
    if mode == 'slc':
        hi_full = jnp.minimum(hi, (pq0 - pos_k0 + 1) // (tk * unroll))
        res = lax.fori_loop(lo, hi_full, functools.partial(step, causal=False), tuple(init))
        res = lax.fori_loop(hi_full, hi, step, res)
    else:
        res = lax.fori_loop(lo, hi, step, tuple(init))
    outs = []
    for p in range(len(pairs)):
        m, l, acc = res[3 * p:3 * p + 3]
        outs.append(jnp.where(m[:, :HEAD_DIM] > 0.5 * NEG_BIG, acc / jnp.maximum(l[:, :HEAD_DIM], 1e-30), 0.0))
    o_ref[0] = jnp.concatenate(outs, 1)


def _attn(q, kv, mode, pairs, kv_width, pos_q0, pos_k0, tq, tk, sel=None):
    B, T, _ = q.shape
    n_k = kv.shape[1]
    in_specs = [pl.BlockSpec((1, tq, LANES), lambda b, h, i: (b, i, h)),
                pl.BlockSpec((1, n_k, kv_width), lambda b, h, i: (b, 0, h))]
    args = [q, kv]
    if mode == 'slc':
        nbp = sel.shape[-1]
        in_specs.append(pl.BlockSpec((1, 1, tq, nbp), lambda b, h, i: (b, h, i, 0)))
        args.append(sel)
        key_blk = (pos_k0 + np.arange(n_k)) // SLC_BLOCK
        in_specs.append(pl.BlockSpec((n_k, nbp), lambda b, h, i: (0, 0)))
        args.append(jnp.asarray(key_blk[:, None] == np.arange(nbp)[None, :], dtype=BF16))
    return pl.pallas_call(
        functools.partial(_attn_body, mode=mode, pairs=pairs, tk=tk, pos_q0=pos_q0, pos_k0=pos_k0, scale=SCALE),
        out_shape=jax.ShapeDtypeStruct((B, T, GW), F32), grid=(B, 2, T // tq),
        in_specs=in_specs, out_specs=pl.BlockSpec((1, tq, LANES), lambda b, h, i: (b, i, h)),
        compiler_params=_cparams(("parallel", "parallel", "arbitrary")), name="attn_" + mode)(*args)


_GQA_PAIRS = ((0, 0, HEAD_DIM), (HEAD_DIM, 0, HEAD_DIM))
_MHA_PAIRS = ((0, 0, HEAD_DIM), (HEAD_DIM, 2 * HEAD_DIM, 3 * HEAD_DIM))


def _slc_paged_body(idx_ref, phys_ref, *refs, pos0, blk0, t_real):
    del phys_ref
    n_slots = t_real * N_SELECT
    q_ref = refs[0]
    blk_refs = refs[1:1 + n_slots]
    new_ref = refs[1 + n_slots]
    o_ref = refs[2 + n_slots]
    b, h = pl.program_id(0), pl.program_id(1)
    bpp = PAGE_SIZE // SLC_BLOCK
    tok = lax.broadcasted_iota(jnp.int32, (1, PAGE_SIZE), 1)
    lane = lax.broadcasted_iota(jnp.int32, (1, SLC_BLOCK), 1)
    newblk = new_ref[0]

    def chain(t):
        base = ((b * NSA_KV_HEADS + h) * t_real + t) * N_SELECT
        qrow = q_ref[0, t:t + 1, :]
        q2 = jnp.concatenate([qrow[:, :HEAD_DIM], qrow[:, HEAD_DIM:], jnp.zeros((6, HEAD_DIM), F32)], 0) * SCALE
        pos = pos0 + t
        scores, vts = [], []
        n_new = jnp.int32(0)
        for j in range(N_SELECT):
            idx = idx_ref[base + j]
            idc = jnp.maximum(idx, 0)
            kv_t = blk_refs[t * N_SELECT + j][0, 0, 0]
            ok = ((idx >= 0) & (idx < blk0) & (tok // SLC_BLOCK == idc % bpp)
                  & ((idc // bpp) * PAGE_SIZE + tok <= pos))
            scores.append(_mm(q2, kv_t[0]) + jnp.where(ok, 0.0, -jnp.inf))
            vts.append(kv_t[1])
            n_new = n_new + jnp.where(idx >= blk0, 1, 0)
        ok_new = (n_new > 0) & (blk0 * SLC_BLOCK + lane <= pos)
        s_new = _mm_nt(q2, newblk[:, :HEAD_DIM]) + jnp.where(ok_new, 0.0, -jnp.inf)
        yield
        m = jnp.max(s_new, -1, keepdims=True)
        for s in scores:
            m = jnp.maximum(m, jnp.max(s, -1, keepdims=True))
        m = jnp.where(m == -jnp.inf, 0.0, m)
        e_new = jnp.exp(s_new - m)
        den = jnp.sum(e_new, -1, keepdims=True)
        es = [jnp.exp(s - m) for s in scores]
        yield
        o = _mm(e_new, newblk[:, HEAD_DIM:])
        for e, vt in zip(es, vts):
            den = den + jnp.sum(e, -1, keepdims=True)
            o = o + _mm_nt(e, vt)
        o = o / jnp.maximum(den, 1e-30)
        orow = jnp.concatenate([o[0:1], o[1:2]], 1)
        o_ref[0, 0, t] = jnp.broadcast_to(orow, (8, LANES))

    _run_interleaved([chain(t) for t in range(t_real)])


def _slc_paged(q_rot, cache_t, new_rows, idx_flat, page_flat, layer, pos0, blk0, t_real):
    B = q_rot.shape[0]
    tp = q_rot.shape[1]
    n_slots = t_real * N_SELECT

    def blk_spec(s):
        def imap(b, h, idx, page):
            return (layer, page[(b * NSA_KV_HEADS + h) * n_slots + s], h, 0, 0, 0)
        return pl.BlockSpec((1, 1, 1, 2, HEAD_DIM, PAGE_SIZE), imap)

    gs = pltpu.PrefetchScalarGridSpec(
        num_scalar_prefetch=2, grid=(B, NSA_KV_HEADS),
        in_specs=[pl.BlockSpec((1, tp, LANES), lambda b, h, idx, page: (b, 0, h))]
        + [blk_spec(s) for s in range(n_slots)]
        + [pl.BlockSpec((1, SLC_BLOCK, LANES), lambda b, h, idx, page: (b, 0, h))],
        out_specs=pl.BlockSpec((1, 1, t_real, 8, LANES), lambda b, h, idx, page: (b, h, 0, 0, 0)))
    out = pl.pallas_call(
        functools.partial(_slc_paged_body, pos0=pos0, blk0=blk0, t_real=t_real),
        out_shape=jax.ShapeDtypeStruct((B, NSA_KV_HEADS, t_real, 8, LANES), F32), grid_spec=gs,
        compiler_params=_cparams(("parallel", "arbitrary")), name="slc_paged")(
            idx_flat, page_flat, q_rot, *([cache_t] * n_slots), new_rows)
    return jnp.transpose(out[:, :, :, 0, :], (0, 2, 1, 3)).reshape(B, t_real, GW)


def _nsa_prep_body(q_ref, s_ref, w_ref, cos_ref, sin_ref, g_ref, ones_ref, qn_ref, qr_ref, so_ref, wo_ref):
    lane = lax.broadcasted_iota(jnp.int32, (1, GW), 1)
    first_half = (lane % HEAD_DIM) < (HEAD_DIM // 2)
    is_k = (lane // HEAD_DIM) % 2 == 0
    cos, sin = cos_ref[...], sin_ref[...]
    ones = ones_ref[...]

    def norm(x, g):
        ms = _mm(x * x, ones, 'l3') * (1.0 / HEAD_DIM)
        return x * lax.rsqrt(ms + NORM_EPS) * g

    def rope(x):
        swapped = jnp.where(first_half, pltpu.roll(x, GW - HEAD_DIM // 2, 1), pltpu.roll(x, HEAD_DIM // 2, 1))
        return x * cos + swapped * sin

    qn = norm(q_ref[0], g_ref[0:1, :])
    qn_ref[0] = qn
    qr_ref[0] = rope(qn)
    for x_ref, o_ref, gi in ((s_ref, so_ref, 1), (w_ref, wo_ref, 2)):
        x = x_ref[0]
        o_ref[0] = jnp.where(is_k, rope(norm(x, g_ref[gi:gi + 1, :])), x)


def _nsa_prep(P3, n_rows, tr, cos, sin, g_blk, ones_bd):
    B = P3.shape[0]
    col = lambda c0: pl.BlockSpec((1, tr, GW), lambda b, i: (b, i, c0 // GW))
    tab = pl.BlockSpec((tr, GW), lambda b, i: (i, 0))
    out = pl.BlockSpec((1, tr, GW), lambda b, i: (b, i, 0))
    shp = jax.ShapeDtypeStruct((B, n_rows, GW), F32)
    return pl.pallas_call(
        _nsa_prep_body, out_shape=(shp, shp, shp, shp), grid=(B, n_rows // tr),
        in_specs=[col(C_NQ), col(C_NSLC), col(C_NWIN), tab, tab, pl.BlockSpec((8, GW), lambda b, i: (0, 0)),
                  pl.BlockSpec((GW, GW), lambda b, i: (0, 0))],
        out_specs=(out, out, out, out),
        compiler_params=_cparams(("parallel", "parallel")), name="nsa_prep")(P3, P3, P3, cos, sin, g_blk, ones_bd)


def _rms(x, g):
    return x * lax.rsqrt(jnp.mean(x * x, -1, keepdims=True) + NORM_EPS) * g


def _rope(x, pos):
    half = HEAD_DIM // 2
    inv = ROPE_THETA ** (-jnp.arange(half, dtype=F32) / half)
    ang = pos.astype(F32)[:, None] * inv
    cos, sin = jnp.cos(ang)[:, None, :], jnp.sin(ang)[:, None, :]
    x1, x2 = x[..., :half], x[..., half:]
    return jnp.concatenate([x1 * cos - x2 * sin, x1 * sin + x2 * cos], -1)


def _pad_t(x, tp, value=0.0):
    t = x.shape[1]
    if t == tp:
        return x
    return jnp.pad(x, ((0, 0), (0, tp - t)) + ((0, 0),) * (x.ndim - 2), constant_values=value)


def _round_up(n, m):
    return -(-n // m) * m


def _cover_matrix(n_cmp, n_cmp_pad, n_slc, nbp):
    start = np.arange(n_cmp_pad)[:, None] * CMP_STRIDE
    blk = np.arange(nbp)[None, :]
    cov = (start < (blk + 1) * SLC_BLOCK) & (start + CMP_BLOCK > blk * SLC_BLOCK)
    cov &= (np.arange(n_cmp_pad)[:, None] < n_cmp) & (blk < n_slc)
    return jnp.asarray(cov.astype(np.float32), dtype=BF16)


def _mlstm_mixer(P, Pp, conv_buf, c0, n0, m0, prm):
    B, T, _ = P.shape
    keep = MLSTM_CONV - 1
    tail = jnp.pad(conv_buf, ((0, 0), (8 - keep, 0), (0, 0)))
    m0p = jnp.pad(m0, ((0, 0), (0, LANES - N_HEADS))).reshape(B, 1, LANES)
    out, C, n, m = _mlstm(Pp, T, tail[..., :GW], tail[..., GW:], prm['mlstm_blk'], c0, n0, m0p)
    qk_raw = jnp.concatenate([P[:, -keep:, C_MQ:C_MQ + GW], P[:, -keep:, C_MK:C_MK + GW]], -1)
    conv_new = jnp.concatenate([conv_buf, qk_raw], 1)[:, -keep:]
    return out[:, :T], C, n, m[:, 0, :N_HEADS], conv_new


def _rwkv_mixer(P, Pp, shift_buf, s0, prm, prec):
    B, T, _ = P.shape
    out, S = _rwkv(Pp, T, shift_buf, prm['rwkv_blk'], prm['rwkv_lora'], s0, prec)
    shift_new = jnp.concatenate([P[:, -1:, C_RR:C_RR + 3 * GW], P[:, -1:, C_RL:C_RL + LANES]], -1)
    return out[:, :T], S, shift_new


def _memory_kv(mem, prm):
    B = mem.shape[0]
    kv = _proj_in(mem.reshape(B * N_MEM, D_MODEL), prm['mem_norm_g'], prm['w_mem_kv_bf16'], 2 * GW)
    kv = kv.reshape(B, N_MEM, MEM_HEADS, 2, HEAD_DIM)
    return jnp.stack([_rms(kv[:, :, :, 0], prm['mem_qk_g'][1]), kv[:, :, :, 1]], 3)


def _memory_mixer(P, mem_kv, prm):
    B, T, _ = P.shape
    tp = _round_up(T, 8)
    qn = _rms(P[..., C_CQ:C_CQ + GW].reshape(B, T, MEM_HEADS, HEAD_DIM), prm['mem_qk_g'][0]).reshape(B, T, GW)
    tq = min(256, tp)
    o = _attn(_pad_t(qn, tp), mem_kv.reshape(B, N_MEM, 2 * GW), 'none', _MHA_PAIRS, 2 * LANES, 0, 0, tq, N_MEM)
    return o[:, :T]


def _rope_tables(pos):
    half = HEAD_DIM // 2
    inv = ROPE_THETA ** (-jnp.arange(half, dtype=F32) / half)
    ang = pos.astype(F32)[:, None] * inv
    cos, sin = jnp.cos(ang), jnp.sin(ang)
    return (jnp.tile(jnp.concatenate([cos, cos], -1), (1, N_HEADS)),
            jnp.tile(jnp.concatenate([-sin, sin], -1), (1, N_HEADS)))


def _nsa_mixer(P2d, Pp, B, T, pos0, win_prefix, prm, past):
    P = P2d.reshape(B, T, DP)
    g = prm['nsa_qk_g']
    tp = _round_up(T, 8)
    tq = min(256, tp)
    cos, sin = _rope_tables(pos0 + jnp.arange(tp, dtype=jnp.int32))
    qn, q_rot, slc_rows, win_rows = _nsa_prep(Pp, tp, tq, cos, sin, prm['nsa_prep_blk'], prm['head_ones'])
    kvrows = lambda t: t[:, :T].reshape(B, T, NSA_KV_HEADS, 2, HEAD_DIM)
    cmp_new = kvrows(P[..., C_NCMP:C_NCMP + 2 * KVW])
    slc_new, win_new = kvrows(slc_rows), kvrows(win_rows)

    L_all = pos0 + T
    n_sub = max(-(-L_all // CMP_STRIDE), CMP_BLOCK // CMP_STRIDE)
    n_cmp = n_sub - 1
    bd, w2bd, pe_hid = prm['cmp_bd'], prm['cmp_w2bd'], prm['cmp_pe_hid']
    if past is None:
        G = _subproj(P2d, C_NCMP // LANES, bd).reshape(B, T // CMP_STRIDE, 4 * LANES)
    else:
        cache_cmp_t, cache_slc_t, page_table, layer = past
        g_pages = _subproj_pages(cache_cmp_t, page_table, layer, bd)
        new_rows = _pad_t(cmp_new.reshape(B, T, 2 * KVW), CMP_STRIDE).reshape(B * CMP_STRIDE, 2 * KVW)
        g_new = _subproj(new_rows, 0, bd).reshape(B, 1, 4 * LANES)
        G = jnp.concatenate([g_pages, g_new], 1)
    n_cmp_pad = _round_up(n_cmp, LANES)
    gb = G[:, 1:, 256:]
    ga = G[:, :, :256]
    fit = lambda t: _pad_t(t, max(n_cmp_pad, t.shape[1]))[:, :n_cmp_pad]
    hid = fit(ga) + fit(gb) + pe_hid
    kv_cmp = _cmp_mlp(hid, w2bd, g[1])
    n_slc = -(-L_all // SLC_BLOCK)
    nbp = _round_up(n_slc, LANES)
    cover = _cover_matrix(n_cmp, n_cmp_pad, n_slc, nbp)
    tq = min(256, tp)
    o_cmp, imp = _cmp_attn(qn, kv_cmp, cover, pos0, tq)

    R = B * NSA_KV_HEADS * tp
    rp = _round_up(R, LANES)
    imp2d = jnp.pad(imp.reshape(R, nbp), ((0, rp - R), (0, 0)))
    sel, idx_t = _topk(imp2d, pos0, tp, n_slc, want_idx=past is not None)

    if past is None:
        sel4 = sel[:R].reshape(B, NSA_KV_HEADS, tp, nbp)
        o_slc = _attn(q_rot, slc_new.reshape(B, T, 2 * KVW), 'slc', _GQA_PAIRS, LANES, pos0, pos0, tq,
                      min(256, T), sel=sel4)
    else:
        bpp = PAGE_SIZE // SLC_BLOCK
        idx = idx_t[:, :R].T.reshape(B, NSA_KV_HEADS, tp, N_SELECT)[:, :, :T]
        idc = jnp.clip(idx, 0, page_table.shape[1] * bpp - 1)
        page = page_table[jnp.arange(B)[:, None, None, None], idc // bpp]
        new_rows = _pad_t(slc_new.reshape(B, T, 2 * KVW), SLC_BLOCK)
        o_slc = _slc_paged(q_rot, cache_slc_t, new_rows, idx.reshape(-1), page.reshape(-1).astype(jnp.int32),
                           layer, pos0, pos0 // SLC_BLOCK, T)

    if win_prefix.shape[1] == 0:
        win_ctx = win_new
        pos_k0 = pos0
    else:
        win_ctx = jnp.concatenate([win_prefix, win_new], 1)
        pos_k0 = pos0 - win_prefix.shape[1]
    n_k = win_ctx.shape[1]
    tkw = min(256, _round_up(n_k, LANES))
    kv_win = _pad_t(win_ctx.reshape(B, n_k, 2 * KVW), _round_up(n_k, tkw))
    o_win = _attn(q_rot, kv_win, 'win', _GQA_PAIRS, LANES, pos0, pos_k0, tq, tkw)

    keep = win_prefix.shape[1] if past is not None else min(WINDOW, T)
    flat = lambda o: o[:, :T].reshape(B * T, GW)
    return (flat(o_cmp), flat(o_slc), flat(o_win)), cmp_new, slc_new, win_ctx[:, -keep:]


RWKV_PREC = ('r3', 'bf16', 'bf16')


def _layer(x, pos0, st, mem_kv, prm, past, rwkv_prec=RWKV_PREC):
    conv_buf, c0, n0, m0, s0, shift_buf, win_prefix = st
    B, T, _ = x.shape
    x2d = x.reshape(B * T, D_MODEL)
    P2d = _proj_in(x2d, prm['norm_g'], prm['w_in_bf16'], 640)
    P = P2d.reshape(B, T, DP)
    Pp = _pad_t(P, _round_up(T, max(MLSTM_L, RWKV_L)))
    y_m, C, n, m, conv_new = _mlstm_mixer(P, Pp, conv_buf, c0, n0, m0, prm)
    (o_cmp, o_slc, o_win), cmp_new, slc_new, win_new = _nsa_mixer(P2d, Pp, B, T, pos0, win_prefix, prm, past)
    y_r, S, shift_new = _rwkv_mixer(P, Pp, shift_buf, s0, prm, rwkv_prec)
    o_mem = _memory_mixer(P, mem_kv, prm)
    flat = lambda t: t.reshape(B * T, GW)
    out = _proj_out(x2d, P2d, flat(y_m), flat(y_r), o_cmp, o_slc, o_win, flat(o_mem), prm['gate_expand'],
                    prm['w_out_bf16']).reshape(B, T, D_MODEL)
    return out, (cmp_new, slc_new, win_new, C, n, m, conv_new, S, shift_new)


def _prep_params(l, p):
    prm = {k: v[l] for k, v in p.items()}
    src = jnp.asarray(np.maximum(_SRC, 0), jnp.int32)
    keep = jnp.asarray((_SRC >= 0).astype(np.float32))
    prm['w_in_bf16'] = (jnp.take(prm['w_in'], src, axis=1) * keep).astype(BF16)
    prm['w_out_bf16'] = prm['w_out'].astype(BF16)
    wm = prm['w_mem_kv'].reshape(D_MODEL, 2, MEM_HEADS, HEAD_DIM)
    prm['w_mem_kv_bf16'] = jnp.transpose(wm, (0, 2, 1, 3)).reshape(D_MODEL, 2 * GW).astype(BF16)
    rows = lambda *vs: jnp.concatenate([jnp.pad(v.reshape(-1, v.shape[-1]), ((0, 0), (0, GW - v.shape[-1])))
                                        for v in vs], 0)
    pad16 = lambda blk: jnp.pad(blk, ((0, 16 - blk.shape[0]), (0, 0)))
    cw, cb = prm['mlstm_conv_w'], prm['mlstm_conv_b']
    prm['mlstm_blk'] = pad16(rows(cw[:, :GW], cw[:, GW:], cb[:GW], cb[GW:], prm['mlstm_norm_g'],
                                  prm['mlstm_gate_b'].reshape(1, 2 * N_HEADS)))
    mu = prm['rwkv_mu']
    prm['rwkv_blk'] = pad16(rows(mu[:GW], mu[GW:2 * GW], mu[2 * GW:3 * GW], mu[3 * GW:], prm['rwkv_w0'],
                                 prm['rwkv_a0'], prm['rwkv_kk'], prm['rwkv_ln']))
    g = prm['nsa_qk_g']
    one = jnp.ones((HEAD_DIM,), F32)
    prm['nsa_prep_blk'] = jnp.pad(rows(jnp.tile(g[0], N_HEADS), jnp.concatenate([g[2], one, g[2], one]),
                                       jnp.concatenate([g[3], one, g[3], one])), ((0, 5), (0, 0)))
    head_of = np.arange(GW) // HEAD_DIM
    prm['head_ones'] = jnp.asarray(head_of[:, None] == head_of[None, :], dtype=BF16)
    gate_lane = np.arange(LANES)[:, None]
    out_lane = np.arange(3 * GW)[None, :]
    prm['gate_expand'] = jnp.asarray(gate_lane == (out_lane // GW) * N_HEADS + (out_lane % GW) // HEAD_DIM, dtype=BF16)
    zl = jnp.zeros((DECAY_LORA, GW), F32)
    prm['rwkv_lora'] = jnp.concatenate([jnp.concatenate([prm['rwkv_w2'], zl], 1),
                                        jnp.concatenate([zl, prm['rwkv_a2']], 1)], 0)
    w1 = prm['nsa_cmp_w1']
    eye_h = jnp.eye(NSA_KV_HEADS, dtype=F32)
    eye_c = jnp.eye(2, dtype=F32)
    w1r = w1.reshape(2, 2, CMP_STRIDE, HEAD_DIM, HEAD_DIM)
    bd = jnp.einsum('cC,crsde->scdrCe', eye_c, w1r)
    prm['cmp_bd'] = bd.reshape(CMP_STRIDE // 2, 2 * LANES, 2 * LANES).astype(BF16)
    w2 = prm['nsa_cmp_w2']
    prm['cmp_w2bd'] = jnp.einsum('hH,cC,ced->hceHCd', eye_h, eye_c, w2).reshape(2 * KVW, 2 * KVW)
    pe_hid = jnp.einsum('csd,csde->ce', prm['nsa_pe'], w1, precision=HI)
    prm['cmp_pe_hid'] = jnp.tile(pe_hid.reshape(1, 2 * HEAD_DIM), (1, NSA_KV_HEADS)).reshape(2 * KVW)
    return prm


def kernel(x_prompt, x_sample, cache_cmp_kv, cache_slc_kv, cache_win_kv, cache_mem_kv, state_mlstm_C, state_mlstm_n, state_mlstm_m, state_mlstm_conv, state_rwkv_S, state_rwkv_shift, page_table, mem_prompt, norm_g, w_in, w_out, mlstm_conv_w, mlstm_conv_b, mlstm_gate_b, mlstm_norm_g, nsa_qk_g, nsa_pe, nsa_cmp_w1, nsa_cmp_w2, rwkv_mu, rwkv_w0, rwkv_w2, rwkv_a0, rwkv_a2, rwkv_kk, rwkv_ln, mem_norm_g, w_mem_kv, mem_qk_g):
    params = dict(norm_g=norm_g, w_in=w_in, w_out=w_out, mlstm_conv_w=mlstm_conv_w, mlstm_conv_b=mlstm_conv_b,
                  mlstm_gate_b=mlstm_gate_b, mlstm_norm_g=mlstm_norm_g, nsa_qk_g=nsa_qk_g, nsa_pe=nsa_pe,
                  nsa_cmp_w1=nsa_cmp_w1, nsa_cmp_w2=nsa_cmp_w2, rwkv_mu=rwkv_mu, rwkv_w0=rwkv_w0, rwkv_w2=rwkv_w2,
                  rwkv_a0=rwkv_a0, rwkv_a2=rwkv_a2, rwkv_kk=rwkv_kk, rwkv_ln=rwkv_ln, mem_norm_g=mem_norm_g,
                  w_mem_kv=w_mem_kv, mem_qk_g=mem_qk_g)
    depth = norm_g.shape[0]
    B = x_prompt.shape[0]
    past_len = page_table.shape[1] * PAGE_SIZE
    cache_cmp_t = jnp.transpose(cache_cmp_kv, (0, 1, 3, 4, 5, 2))
    cache_slc_t = jnp.transpose(cache_slc_kv, (0, 1, 3, 4, 5, 2))
    xp, xs = x_prompt, x_sample
    new_p, new_s, new_mem = [], [], []
    for l in range(depth):
        prm = _prep_params(l, params)
        mem_kv_p = _memory_kv(mem_prompt, prm)
        st_p = (jnp.zeros((B, MLSTM_CONV - 1, 2 * GW), F32),
                jnp.zeros((B, N_HEADS, HEAD_DIM, HEAD_DIM), F32),
                jnp.zeros((B, N_HEADS, HEAD_DIM), F32),
                jnp.full((B, N_HEADS), M_INIT, F32),
                jnp.zeros((B, N_HEADS, HEAD_DIM, HEAD_DIM), F32),
                jnp.zeros((B, 1, RWKV_SHIFT), F32),
                jnp.zeros((B, 0, NSA_KV_HEADS, 2, HEAD_DIM), F32))
        xp, sp = _layer(xp, 0, st_p, mem_kv_p, prm, None)
        st_s = (state_mlstm_conv[l], state_mlstm_C[l], state_mlstm_n[l], state_mlstm_m[l],
                state_rwkv_S[l], state_rwkv_shift[l], cache_win_kv[l])
        xs, ss = _layer(xs, past_len, st_s, cache_mem_kv[l], prm, (cache_cmp_t, cache_slc_t, page_table, l))
        new_p.append(sp)
        new_s.append(ss)
        new_mem.append(mem_kv_p)
    stack = lambda states, i: jnp.stack([s[i] for s in states])
    outs = [xp, xs]
    for i in range(3):
        outs += [stack(new_p, i), stack(new_s, i)]
    outs.append(jnp.stack(new_mem))
    for i in range(3, 9):
        outs += [stack(new_p, i), stack(new_s, i)]
    return tuple(outs)
```

```python
import functools

import numpy as np
import jax
import jax.numpy as jnp
from jax import lax
from jax.experimental import pallas as pl
from jax.experimental.pallas import tpu as pltpu

F32 = jnp.float32
BF16 = jnp.bfloat16
HI = lax.Precision.HIGHEST

D_MODEL = 1024
PAGE_SIZE = 128
HEAD_DIM = 64
GW = D_MODEL // 4
N_HEADS = GW // HEAD_DIM
SCALE = HEAD_DIM ** -0.5
MLSTM_CONV = 4
M_INIT = -1e30
NSA_KV_HEADS = 2
KVW = NSA_KV_HEADS * HEAD_DIM
CMP_BLOCK = 32
CMP_STRIDE = 16
SLC_BLOCK = 64
N_SELECT = 16
WINDOW = 512
DECAY_LORA = 64
AAA_LORA = 64
RWKV_SHIFT = 3 * GW + DECAY_LORA + AAA_LORA
RWKV_LN_EPS = HEAD_DIM * 1e-5
N_MEM = 256
MEM_HEADS = 4
ROPE_THETA = 10000.0
NORM_EPS = 1e-6

LANES = 128
MLSTM_L = 128
RWKV_L = 64
NEG_BIG = -1e30

_M0, _N0, _R0, _C0 = 0, 1288, 2580, 3732
C_MQ, C_MK, C_MV, C_MO, C_MZ = 0, 256, 512, 768, 1024
C_NQ, C_NCMP, C_NSLC, C_NWIN, C_NZ = 1280, 1536, 1792, 2048, 2304
C_RR, C_RK, C_RV, C_RZ = 2560, 2816, 3072, 3328
C_CQ, C_CZ = 3584, 3840
C_RL = 4096
C_MG = 4224
C_NG = 4352
DP = 4480


def _packed_src():
    src = -np.ones((DP,), np.int64)

    def put(dst, lo, n):
        src[dst:dst + n] = np.arange(lo, lo + n)

    put(C_MQ, _M0, 256); put(C_MK, _M0 + 256, 256); put(C_MV, _M0 + 512, 256)
    put(C_MG, _M0 + 768, 8); put(C_MO, _M0 + 776, 256); put(C_MZ, _M0 + 1032, 256)
    put(C_NQ, _N0, 256)
    for i, base in enumerate((C_NCMP, C_NSLC, C_NWIN)):
        ksrc = _N0 + 256 + 256 * i
        vsrc = ksrc + 128
        for h in range(2):
            put(base + 128 * h, ksrc + 64 * h, 64)
            put(base + 128 * h + 64, vsrc + 64 * h, 64)
    put(C_NG, _N0 + 1024, 12); put(C_NZ, _N0 + 1036, 256)
    put(C_RR, _R0, 256); put(C_RK, _R0 + 256, 256); put(C_RV, _R0 + 512, 256)
    put(C_RL, _R0 + 768, 128); put(C_RZ, _R0 + 896, 256)
    put(C_CQ, _C0, 256); put(C_CZ, _C0 + 256, 256)
    return src


_SRC = _packed_src()


def _split2(a):
    hi = a.astype(BF16)
    return hi, (a - hi.astype(F32)).astype(BF16)


def _dg(a, b, dims, prec):
    dn = (dims, ((), ()))
    if prec == 'bf16':
        return lax.dot_general(a.astype(BF16), b.astype(BF16), dn, preferred_element_type=F32)
    d = lambda x, y: lax.dot_general(x, y, dn, preferred_element_type=F32)
    if prec == 'x3':
        ah, al = _split2(a)
        bh, bl = _split2(b)
        return d(ah, bh) + (d(ah, bl) + d(al, bh))
    if prec in ('r3', 'l3'):
        exact, other = (a, b) if prec == 'r3' else (b, a)
        o1, rest = other.astype(BF16), None
        rest = other - o1.astype(F32)
        o2 = rest.astype(BF16)
        o3 = (rest - o2.astype(F32)).astype(BF16)
        e = exact.astype(BF16)
        if prec == 'r3':
            return d(e, o1) + (d(e, o2) + d(e, o3))
        return d(o1, e) + (d(o2, e) + d(o3, e))
    return lax.dot_general(a, b, dn, preferred_element_type=F32, precision=prec)


def _mm(a, b, prec=None):
    return _dg(a, b, ((1,), (0,)), prec)


def _mm_nt(a, b, prec=None):
    return _dg(a, b, ((1,), (1,)), prec)


def _mm_tn(a, b, prec=None):
    return _dg(a, b, ((0,), (0,)), prec)


def _run_interleaved(chains):
    chains = list(chains)
    while chains:
        alive = []
        for ch in chains:
            try:
                next(ch)
                alive.append(ch)
            except StopIteration:
                pass
        chains = alive


def _cparams(sem, vmem_mb=None):
    kw = dict(dimension_semantics=sem)
    if vmem_mb is not None:
        kw['vmem_limit_bytes'] = vmem_mb * 1024 * 1024
    return pltpu.CompilerParams(**kw)


def _proj_in_body(x_ref, g_ref, w_ref, o_ref, *, tn):
    x = x_ref[...]
    h = (x * lax.rsqrt(jnp.mean(x * x, -1, keepdims=True) + NORM_EPS) * g_ref[...]).astype(BF16)
    for j in range(o_ref.shape[1] // tn):
        o_ref[:, j * tn:(j + 1) * tn] = jnp.dot(h, w_ref[:, j * tn:(j + 1) * tn], preferred_element_type=F32)


def _proj_in(x2d, g, w_bf16, tn):
    n, d = x2d.shape
    dn = w_bf16.shape[1]
    tm = min(512, n)
    return pl.pallas_call(
        functools.partial(_proj_in_body, tn=tn), out_shape=jax.ShapeDtypeStruct((n, dn), F32), grid=(n // tm,),
        in_specs=[pl.BlockSpec((tm, d), lambda i: (i, 0)),
                  pl.BlockSpec((1, d), lambda i: (0, 0)),
                  pl.BlockSpec((d, dn), lambda i: (0, 0))],
        out_specs=pl.BlockSpec((tm, dn), lambda i: (i, 0)),
        compiler_params=_cparams(("parallel",), 56), name="proj_in")(x2d, g.reshape(1, d), w_bf16)


def _silu(x):
    return x * jax.nn.sigmoid(x)


def _proj_out_body(x_ref, ym_ref, yr_ref, oc_ref, os_ref, ow_ref, om_ref, g_ref, zn_ref, zc_ref, e_ref, w_ref, o_ref):
    ge = _mm(jax.nn.sigmoid(g_ref[...]), e_ref[...], 'l3')
    y_n = (ge[:, :GW] * oc_ref[...] + ge[:, GW:2 * GW] * os_ref[...] + ge[:, 2 * GW:] * ow_ref[...]) * _silu(zn_ref[...])
    y_c = om_ref[...] * _silu(zc_ref[...])
    acc = x_ref[...]
    for i, y in enumerate((ym_ref[...], y_n, yr_ref[...], y_c)):
        acc = acc + jnp.dot(y.astype(BF16), w_ref[i * GW:(i + 1) * GW, :], preferred_element_type=F32)
    o_ref[...] = acc


def _proj_out(x2d, P2d, y_m, y_r, o_cmp, o_slc, o_win, o_mem, gate_expand, w_bf16):
    n, d = x2d.shape
    tm = min(512, n)
    yspec = pl.BlockSpec((tm, GW), lambda i: (i, 0))
    return pl.pallas_call(
        _proj_out_body, out_shape=jax.ShapeDtypeStruct((n, d), F32), grid=(n // tm,),
        in_specs=[pl.BlockSpec((tm, d), lambda i: (i, 0)), yspec, yspec, yspec, yspec, yspec, yspec,
                  pl.BlockSpec((tm, LANES), lambda i: (i, C_NG // LANES)),
                  pl.BlockSpec((tm, GW), lambda i: (i, C_NZ // GW)), pl.BlockSpec((tm, GW), lambda i: (i, C_CZ // GW)),
                  pl.BlockSpec((LANES, 3 * GW), lambda i: (0, 0)), pl.BlockSpec((d, d), lambda i: (0, 0))],
        out_specs=pl.BlockSpec((tm, d), lambda i: (i, 0)),
        compiler_params=_cparams(("parallel",), 48), name="proj_out")(
            x2d, y_m, y_r, o_cmp, o_slc, o_win, o_mem, P2d, P2d, P2d, gate_expand, w_bf16)


_MP_CONV_Q, _MP_CONV_K, _MP_BIAS_Q, _MP_BIAS_K, _MP_NORM, _MP_GATE_B = 0, 4, 8, 9, 10, 11


def _mlstm_body(q_ref, k_ref, v_ref, o_ref, z_ref, g_ref, tq_ref, tk_ref, prm_ref, c0_ref, n0_ref, m0_ref,
                h_ref, c_ref, n_ref, m_ref, c_scr, n_scr, m_scr, pq_scr, pk_scr, *, t_valid):
    c = pl.program_id(1)
    nb, L = q_ref.shape[0], q_ref.shape[1]

    @pl.when(c == 0)
    def _():
        c_scr[...] = c0_ref[...]
        n_scr[...] = n0_ref[...]
        m_scr[...] = m0_ref[...]
        pq_scr[...] = jnp.zeros(pq_scr.shape, F32)
        pk_scr[...] = jnp.zeros(pk_scr.shape, F32)
        pq_scr[:, L - 8:L, :] = tq_ref[...]
        pk_scr[:, L - 8:L, :] = tk_ref[...]

    row = lax.broadcasted_iota(jnp.int32, (L, L), 0)
    col = lax.broadcasted_iota(jnp.int32, (L, L), 1)
    causal = row >= col
    tril = causal.astype(F32)
    triu = (row <= col).astype(F32)
    trow = lax.broadcasted_iota(jnp.int32, (L, 1), 0)
    valid = (c * L + trow) < t_valid
    lane = lax.broadcasted_iota(jnp.int32, (1, LANES), 1)
    prm = prm_ref[...]

    def conv(x, prev, w0, b):
        acc = prm[b:b + 1, :] + prm[w0 + MLSTM_CONV - 1:w0 + MLSTM_CONV, :] * x
        for s in range(1, MLSTM_CONV):
            shifted = jnp.where(trow >= s, pltpu.roll(x, s, 0), pltpu.roll(prev, s, 0))
            acc = acc + prm[w0 + MLSTM_CONV - 1 - s:w0 + MLSTM_CONV - s, :] * shifted
        return _silu(acc)

    cums, qs, ks = [], [], []
    for gi in range(nb):
        q_raw, k_raw = q_ref[gi], k_ref[gi]
        qs.append(conv(q_raw, pq_scr[gi], _MP_CONV_Q, _MP_BIAS_Q) * (HEAD_DIM ** -0.5))
        ks.append(conv(k_raw, pk_scr[gi], _MP_CONV_K, _MP_BIAS_K))
        pq_scr[gi] = q_raw
        pk_scr[gi] = k_raw
        x = g_ref[gi] + prm[_MP_GATE_B:_MP_GATE_B + 1, :LANES]
        log_f = jnp.minimum(x, 0.0) - jnp.log1p(jnp.exp(-jnp.abs(x)))
        g = jnp.where(lane < N_HEADS, x, jnp.where(lane < 2 * N_HEADS, log_f, 0.0))
        g = jnp.where(valid, g, jnp.where(lane < N_HEADS, NEG_BIG, 0.0))
        gt = g.T
        cums.append((g, gt, _mm(tril, g, 'r3'), _mm(gt, triu, 'l3')))

    def chain(gi, h):
        g, gt, bc, br = cums[gi]
        sl = slice(h * HEAD_DIM, (h + 1) * HEAD_DIM)
        qh = qs[gi][:, sl]
        kh = ks[gi][:, sl]
        vh = v_ref[gi, :, sl]
        qk = _mm_nt(qh, kh)
        ch = c_scr[gi, h]
        qc = _mm(qh, ch)
        yield
        b_col = bc[:, 4 + h:5 + h]
        li_col = g[:, h:h + 1]
        b_row = br[4 + h:5 + h, :]
        li_row = gt[h:h + 1, :]
        m_prev = m_scr[gi, :, h:h + 1]
        log_d = jnp.where(causal, b_col - b_row + li_row, -jnp.inf)
        log_inter = b_col + m_prev
        m_t = jnp.maximum(jnp.max(log_d, -1, keepdims=True), log_inter)
        s = qk * jnp.exp(log_d - m_t)
        w_inter = jnp.exp(log_inter - m_t)
        nh = n_scr[gi, h:h + 1, :]
        sv = _mm(s, vh)
        yield
        num = sv + w_inter * qc
        den = jnp.sum(s, -1, keepdims=True) + w_inter * jnp.sum(qh * nh, -1, keepdims=True)
        hh = num / jnp.maximum(jnp.abs(den), jnp.exp(-m_t))
        hn = hh * lax.rsqrt(jnp.mean(hh * hh, -1, keepdims=True) + NORM_EPS) * prm[_MP_NORM:_MP_NORM + 1, sl]
        h_ref[gi, :, sl] = jax.nn.sigmoid(o_ref[gi, :, sl]) * hn * _silu(z_ref[gi, :, sl])
        b_end = b_col[L - 1:L, :]
        log_w = b_end - b_col + li_col
        m_new = jnp.maximum(b_end + m_prev, jnp.max(log_w, 0, keepdims=True))
        wk = jnp.exp(log_w - m_new)
        decay = jnp.exp(b_end + m_prev - m_new)
        kw = kh * wk
        c_scr[gi, h] = decay * ch + _mm_tn(kw, vh)
        n_scr[gi, h:h + 1, :] = decay * nh + jnp.sum(kw, 0, keepdims=True)
        m_scr[gi, :, h:h + 1] = m_new

    _run_interleaved([chain(gi, h) for gi in range(nb) for h in range(N_HEADS)])

    @pl.when(c == pl.num_programs(1) - 1)
    def _():
        c_ref[...] = c_scr[...]
        n_ref[...] = n_scr[...]
        m_ref[...] = m_scr[...]


BATCH_ROWS = 4


def _mlstm(P, t_valid, tail_q, tail_k, prm_blk, c0, n0, m0):
    B, tp, _ = P.shape
    L = MLSTM_L
    nb = BATCH_ROWS if B % BATCH_ROWS == 0 else 1
    col = lambda c0_: pl.BlockSpec((nb, L, GW), lambda b, c: (b, c, c0_ // GW))
    tok = pl.BlockSpec((nb, L, GW), lambda b, c: (b, c, 0))
    tail = pl.BlockSpec((nb, 8, GW), lambda b, c: (b, 0, 0))
    sc = pl.BlockSpec((nb, N_HEADS, HEAD_DIM, HEAD_DIM), lambda b, c: (b, 0, 0, 0))
    sn = pl.BlockSpec((nb, N_HEADS, HEAD_DIM), lambda b, c: (b, 0, 0))
    sm = pl.BlockSpec((nb, 1, LANES), lambda b, c: (b, 0, 0))
    return pl.pallas_call(
        functools.partial(_mlstm_body, t_valid=t_valid),
        out_shape=(jax.ShapeDtypeStruct((B, tp, GW), F32), jax.ShapeDtypeStruct(c0.shape, F32),
                   jax.ShapeDtypeStruct(n0.shape, F32), jax.ShapeDtypeStruct(m0.shape, F32)),
        grid=(B // nb, -(-t_valid // L)),
        in_specs=[col(C_MQ), col(C_MK), col(C_MV), col(C_MO), col(C_MZ),
                  pl.BlockSpec((nb, L, LANES), lambda b, c: (b, c, C_MG // LANES)), tail, tail,
                  pl.BlockSpec((16, GW), lambda b, c: (0, 0)), sc, sn, sm],
        out_specs=(tok, sc, sn, sm),
        scratch_shapes=[pltpu.VMEM((nb, N_HEADS, HEAD_DIM, HEAD_DIM), F32), pltpu.VMEM((nb, N_HEADS, HEAD_DIM), F32),
                        pltpu.VMEM((nb, 1, LANES), F32), pltpu.VMEM((nb, L, GW), F32), pltpu.VMEM((nb, L, GW), F32)],
        compiler_params=_cparams(("parallel", "arbitrary")), name="mlstm")(
            P, P, P, P, P, P, tail_q, tail_k, prm_blk, c0, n0, m0)


(_RP_MU_R, _RP_MU_K, _RP_MU_V, _RP_MU_L, _RP_W0, _RP_A0, _RP_KK, _RP_KA, _RP_RK, _RP_LN_G, _RP_LN_B) = range(11)


def _rwkv_body(r_ref, k_ref, v_ref, z_ref, l_ref, sr_ref, sk_ref, sv_ref, sl_ref, prm_ref, lora_ref, s0_ref,
               y_ref, s_ref, s_scr, cr_scr, ck_scr, cv_scr, cl_scr, *, prec, t_valid):
    c_id = pl.program_id(1)

    @pl.when(c_id == 0)
    def _():
        s_scr[...] = s0_ref[...]
        cr_scr[...] = sr_ref[...]
        ck_scr[...] = sk_ref[...]
        cv_scr[...] = sv_ref[...]
        cl_scr[...] = sl_ref[...]

    nb, L = r_ref.shape[0], r_ref.shape[1]
    D = HEAD_DIM
    row = lax.broadcasted_iota(jnp.int32, (L, L), 0)
    col = lax.broadcasted_iota(jnp.int32, (L, L), 1)
    lower = row >= col
    strict = row > col
    tril = lower.astype(F32)
    n_sq = int(np.log2(L)) - 1
    pc, pa, prec = prec
    trow = lax.broadcasted_iota(jnp.int32, (L, 1), 0)
    valid = (c_id * L + trow) < t_valid
    prm = prm_ref[...]
    prow = lambda i, n=GW: prm[i:i + 1, :n]

    def shifted_mix(x_ref, carry_scr, g, mu):
        x = x_ref[g]
        prev = jnp.where(trow >= 1, pltpu.roll(x, 1, 0), carry_scr[g])
        carry_scr[g] = x[L - 1:L, :]
        return x + (prev - x) * mu

    prep = []
    for g in range(nb):
        r = shifted_mix(r_ref, cr_scr, g, prow(_RP_MU_R))
        k = shifted_mix(k_ref, ck_scr, g, prow(_RP_MU_K))
        v = shifted_mix(v_ref, cv_scr, g, prow(_RP_MU_V))
        lo = shifted_mix(l_ref, cl_scr, g, prow(_RP_MU_L, LANES))
        lo_in = jnp.where(lax.broadcasted_iota(jnp.int32, (1, LANES), 1) < DECAY_LORA, jnp.tanh(lo), lo)
        lora = _mm(lo_in, lora_ref[...])
        w_pre = prow(_RP_W0) + lora[:, :GW]
        log_w = -(float(np.exp(-0.5)) * jax.nn.sigmoid(w_pre))
        a = jax.nn.sigmoid(prow(_RP_A0) + lora[:, GW:])
        k_eff = k * (1.0 + (a - 1.0) * prow(_RP_KA))
        kk_raw = k * prow(_RP_KK)
        log_w = jnp.where(valid, log_w, 0.0)
        k_eff = jnp.where(valid, k_eff, 0.0)
        kk_raw = jnp.where(valid, kk_raw, 0.0)
        v = jnp.where(valid, v, 0.0)
        prep.append((r, log_w, k_eff, v, kk_raw, a))

    def chain(g, h):
        sl = slice(h * D, (h + 1) * D)
        r, w, k, v, kk, a = (x[:, sl] for x in prep[g])
        kk = kk / jnp.maximum(jnp.sqrt(jnp.sum(kk * kk, -1, keepdims=True)), 1e-12)
        cum = _mm(tril, w, pc)
        yield
        c_last = cum[L - 1:L, :]
        e_neg = jnp.exp(-cum)
        kh = kk * jnp.exp(cum - w)
        bt = kk * a * e_neg
        kt = k * e_neg
        rh = r * jnp.exp(cum)
        gram = _mm_nt(jnp.concatenate([kh, rh], 0), jnp.concatenate([bt, kt], 0), pa)
        yield
        A = jnp.where(strict, gram[:L, :L], 0.0)
        Bm = jnp.where(strict, gram[:L, L:], 0.0)
        Mb = jnp.where(lower, gram[L:, :L], 0.0)
        Mk = jnp.where(lower, gram[L:, L:], 0.0)
        X = jnp.concatenate([kh, _mm(Bm, v, pa)], 1)
        yield
        Pw = A
        X = X - _mm(Pw, X, pa)
        yield
        for _ in range(n_sq):
            Pw = _mm(Pw, Pw, pa)
            yield
            X = X + _mm(Pw, X, pa)
            yield
        e_end = jnp.exp(c_last - cum)
        bp = kk * a * e_end
        kp = k * e_end
        xtb = _mm_tn(X, bp, prec)
        yield
        wtb = xtb[:D]
        N = _mm_tn(v, kp, prec) - xtb[D:]
        yield
        mbx = _mm(Mb, X, prec)
        yield
        qp = rh - mbx[:, :D]
        y0 = _mm(Mk, v, prec) - mbx[:, D:]
        yield
        s0 = s_scr[g, h]
        y = _mm_nt(qp, s0, prec) + y0
        yield
        s_scr[g, h] = s0 * jnp.exp(c_last) - _mm(s0, wtb, prec) + N
        mu = jnp.mean(y, -1, keepdims=True)
        var = jnp.mean(jnp.square(y - mu), -1, keepdims=True)
        y = (y - mu) * lax.rsqrt(var + RWKV_LN_EPS) * prm[_RP_LN_G:_RP_LN_G + 1, sl] + prm[_RP_LN_B:_RP_LN_B + 1, sl]
        y = y + jnp.sum(r * k * prm[_RP_RK:_RP_RK + 1, sl], -1, keepdims=True) * v
        y_ref[g, :, sl] = y * _silu(z_ref[g, :, sl])

    _run_interleaved([chain(g, h) for g in range(nb) for h in range(N_HEADS)])

    @pl.when(c_id == pl.num_programs(1) - 1)
    def _():
        s_ref[...] = s_scr[...]


def _rwkv(P, t_valid, shift_buf, prm_blk, lora_w, s0, prec):
    B, tp, _ = P.shape
    L = RWKV_L
    nb = BATCH_ROWS if B % BATCH_ROWS == 0 else 1
    col = lambda c0_: pl.BlockSpec((nb, L, GW), lambda b, c: (b, c, c0_ // GW))
    tok = pl.BlockSpec((nb, L, GW), lambda b, c: (b, c, 0))
    car = lambda w: pl.BlockSpec((nb, 1, w), lambda b, c: (b, 0, 0))
    st = pl.BlockSpec((nb, N_HEADS, HEAD_DIM, HEAD_DIM), lambda b, c: (b, 0, 0, 0))
    shifts = [shift_buf[..., i * GW:(i + 1) * GW] for i in range(3)] + [shift_buf[..., 3 * GW:]]
    return pl.pallas_call(
        functools.partial(_rwkv_body, prec=prec, t_valid=t_valid),
        out_shape=(jax.ShapeDtypeStruct((B, tp, GW), F32), jax.ShapeDtypeStruct(s0.shape, F32)),
        grid=(B // nb, -(-t_valid // L)),
        in_specs=[col(C_RR), col(C_RK), col(C_RV), col(C_RZ),
                  pl.BlockSpec((nb, L, LANES), lambda b, c: (b, c, C_RL // LANES)),
                  car(GW), car(GW), car(GW), car(LANES),
                  pl.BlockSpec((16, GW), lambda b, c: (0, 0)), pl.BlockSpec((LANES, 2 * GW), lambda b, c: (0, 0)), st],
        out_specs=(tok, st),
        scratch_shapes=[pltpu.VMEM((nb, N_HEADS, HEAD_DIM, HEAD_DIM), F32), pltpu.VMEM((nb, 1, GW), F32),
                        pltpu.VMEM((nb, 1, GW), F32), pltpu.VMEM((nb, 1, GW), F32), pltpu.VMEM((nb, 1, LANES), F32)],
        compiler_params=_cparams(("parallel", "arbitrary")), name="rwkv")(
            P, P, P, P, P, *shifts, prm_blk, lora_w, s0)


def _subproj_accumulate(load_rows, w_ref, o_ref):
    n = o_ref.shape[-2]
    accs = []
    for h in range(NSA_KV_HEADS):
        acc = jnp.zeros((n, 2 * LANES), F32)
        for s in range(0, CMP_STRIDE, 2):
            xs = jnp.concatenate([load_rows(h, s, n), load_rows(h, s + 1, n)], 1)
            acc = acc + jnp.dot(xs.astype(BF16), w_ref[s // 2], preferred_element_type=F32)
        accs.append(acc)
    out = jnp.concatenate([accs[0][:, :LANES], accs[1][:, :LANES], accs[0][:, LANES:], accs[1][:, LANES:]], 1)
    o_ref[...] = out.reshape(o_ref.shape)


def _subproj_body(x0_ref, x1_ref, w_ref, o_ref):
    xs = (x0_ref, x1_ref)
    _subproj_accumulate(lambda h, s, n: xs[h][pl.ds(s, n, stride=CMP_STRIDE), :], w_ref, o_ref)


def _subproj(rows2d, col0, w):
    n = rows2d.shape[0]
    tm = min(2048, n)
    return pl.pallas_call(
        _subproj_body, out_shape=jax.ShapeDtypeStruct((n // CMP_STRIDE, 4 * LANES), F32), grid=(n // tm,),
        in_specs=[pl.BlockSpec((tm, LANES), lambda i: (i, col0)), pl.BlockSpec((tm, LANES), lambda i: (i, col0 + 1)),
                  pl.BlockSpec((CMP_STRIDE // 2, 2 * LANES, 2 * LANES), lambda i: (0, 0, 0))],
        out_specs=pl.BlockSpec((tm // CMP_STRIDE, 4 * LANES), lambda i: (i, 0)),
        compiler_params=_cparams(("parallel",)), name="cmp_subproj")(rows2d, rows2d, w)


_PAGES_PER_STEP = 32


def _subproj_pages_body(pt_ref, *refs):
    del pt_ref
    npg = len(refs) - 3
    w_ref, o_ref, rows_scr = refs[npg], refs[npg + 1], refs[npg + 2]
    for p in range(npg):
        for h in range(NSA_KV_HEADS):
            rows_scr[h, p * PAGE_SIZE:(p + 1) * PAGE_SIZE, :] = refs[p][0, 0, h].reshape(2 * HEAD_DIM, PAGE_SIZE).T
    _subproj_accumulate(lambda h, s, n: rows_scr[h, pl.ds(s, n, stride=CMP_STRIDE), :], w_ref, o_ref)


def _subproj_pages(cache_t, page_table, layer, w):
    B, n_pages = page_table.shape
    npg = min(_PAGES_PER_STEP, n_pages)
    spp = PAGE_SIZE // CMP_STRIDE

    def page_spec(p):
        return pl.BlockSpec((1, 1, NSA_KV_HEADS, 2, HEAD_DIM, PAGE_SIZE),
                            lambda b, g, pt: (layer, pt[b, g * npg + p], 0, 0, 0, 0))

    gs = pltpu.PrefetchScalarGridSpec(
        num_scalar_prefetch=1, grid=(B, n_pages // npg),
        in_specs=[page_spec(p) for p in range(npg)]
        + [pl.BlockSpec((CMP_STRIDE // 2, 2 * LANES, 2 * LANES), lambda b, g, pt: (0, 0, 0))],
        out_specs=pl.BlockSpec((1, npg * spp, 4 * LANES), lambda b, g, pt: (b, g, 0)),
        scratch_shapes=[pltpu.VMEM((NSA_KV_HEADS, npg * PAGE_SIZE, LANES), F32)])
    return pl.pallas_call(
        _subproj_pages_body, out_shape=jax.ShapeDtypeStruct((B, n_pages * spp, 4 * LANES), F32), grid_spec=gs,
        compiler_params=_cparams(("parallel", "arbitrary"), 48), name="cmp_subproj_pages")(
            page_table, *([cache_t] * npg), w)


def _cmp_mlp_body(h_ref, w2_ref, g_ref, o_ref):
    x = h_ref[0]
    kv = _mm(jax.nn.gelu(x), w2_ref[...])
    g = g_ref[...]
    segs = []
    for j in range(4):
        seg = kv[:, j * HEAD_DIM:(j + 1) * HEAD_DIM]
        if j % 2 == 0:
            seg = seg * lax.rsqrt(jnp.mean(seg * seg, -1, keepdims=True) + NORM_EPS) * g
        segs.append(seg)
    o_ref[0] = jnp.concatenate(segs, 1)


def _cmp_mlp(hid, w2bd, g):
    B, n, _ = hid.shape
    tn = min(512, n)
    return pl.pallas_call(
        _cmp_mlp_body, out_shape=jax.ShapeDtypeStruct((B, n, 256), F32), grid=(B, n // tn),
        in_specs=[pl.BlockSpec((1, tn, 256), lambda b, i: (b, i, 0)), pl.BlockSpec((256, 256), lambda b, i: (0, 0)),
                  pl.BlockSpec((1, HEAD_DIM), lambda b, i: (0, 0))],
        out_specs=pl.BlockSpec((1, tn, 256), lambda b, i: (b, i, 0)),
        compiler_params=_cparams(("parallel", "parallel")), name="cmp_mlp")(hid, w2bd, g.reshape(1, HEAD_DIM))


def _cmp_attn_body(q_ref, kv_ref, cov_ref, o_ref, imp_ref, *, pos0):
    qi = pl.program_id(2)
    tq = q_ref.shape[1]
    n = kv_ref.shape[1]
    kv = kv_ref[0]
    k = kv[:, :HEAD_DIM]
    v = kv[:, HEAD_DIM:]
    q = q_ref[0]
    pos = pos0 + qi * tq + lax.broadcasted_iota(jnp.int32, (tq, 1), 0)
    end = lax.broadcasted_iota(jnp.int32, (1, n), 1) * CMP_STRIDE + (CMP_BLOCK - 1)
    mask = end <= pos
    psum = jnp.zeros((tq, n), F32)
    outs = []
    for g in range(2):
        s = _mm_nt(q[:, g * HEAD_DIM:(g + 1) * HEAD_DIM], k) * SCALE
        s = jnp.where(mask, s, -jnp.inf)
        m = jnp.max(s, -1, keepdims=True)
        e = jnp.exp(s - jnp.where(m == -jnp.inf, 0.0, m))
        p = e / jnp.maximum(jnp.sum(e, -1, keepdims=True), 1e-30)
        outs.append(_mm(p, v))
        psum = psum + p
    o_ref[0] = jnp.concatenate(outs, 1)
    hi = psum.astype(BF16)
    lo = (psum - hi.astype(F32)).astype(BF16)
    cov = cov_ref[...]
    imp_ref[0, 0] = _mm(hi, cov) + _mm(lo, cov)


def _cmp_attn(qn, kv_cmp, cover, pos0, tq):
    B, T, _ = qn.shape
    n = kv_cmp.shape[1]
    nbp = cover.shape[1]
    return pl.pallas_call(
        functools.partial(_cmp_attn_body, pos0=pos0),
        out_shape=(jax.ShapeDtypeStruct((B, T, GW), F32), jax.ShapeDtypeStruct((B, NSA_KV_HEADS, T, nbp), F32)),
        grid=(B, NSA_KV_HEADS, T // tq),
        in_specs=[pl.BlockSpec((1, tq, LANES), lambda b, h, i: (b, i, h)),
                  pl.BlockSpec((1, n, LANES), lambda b, h, i: (b, 0, h)),
                  pl.BlockSpec((n, nbp), lambda b, h, i: (0, 0))],
        out_specs=(pl.BlockSpec((1, tq, LANES), lambda b, h, i: (b, i, h)),
                   pl.BlockSpec((1, 1, tq, nbp), lambda b, h, i: (b, h, i, 0))),
        compiler_params=_cparams(("parallel", "parallel", "arbitrary")), name="cmp_attn")(qn, kv_cmp, cover)


def _topk_body(imp_ref, sel_ref, idx_ref, cnt_scr, *, pos0, t_rows, n_blk, want_idx):
    for sub in range(imp_ref.shape[0] // LANES):
        rows = slice(sub * LANES, (sub + 1) * LANES)
        _topk_tile(imp_ref.at[rows, :], sel_ref.at[rows, :], idx_ref.at[:, rows], cnt_scr,
                   pl.program_id(0) * (imp_ref.shape[0] // LANES) + sub,
                   pos0=pos0, t_rows=t_rows, n_blk=n_blk, want_idx=want_idx)


def _topk_tile(imp_ref, sel_ref, idx_ref, cnt_scr, ti, *, pos0, t_rows, n_blk, want_idx):
    x = imp_ref[...]
    nbp = x.shape[1]
    nbu = cnt_scr.shape[0]
    xt = jnp.concatenate([x[:, j * LANES:(j + 1) * LANES].T for j in range(nbp // LANES)], 0)[:nbu]
    r = ti * LANES + lax.broadcasted_iota(jnp.int32, (1, LANES), 1)
    cur = (pos0 + r % t_rows) // SLC_BLOCK
    t_last = (ti * LANES) % t_rows + LANES - 1 if t_rows % LANES == 0 else t_rows - 1
    cur_max = (pos0 + t_last) // SLC_BLOCK
    blk = lax.broadcasted_iota(jnp.int32, (nbu, 1), 0)
    forced = (blk == 0) | (blk == cur) | (blk == cur - 1)
    val = jnp.where(forced, jnp.inf, jnp.where(blk <= cur, xt, -jnp.inf))
    val = jnp.where(blk < n_blk, val, -jnp.inf)
    cnt_scr[...] = jnp.zeros(cnt_scr.shape, F32)
    for g0 in range(0, n_blk, 8):
        @pl.when(g0 <= cur_max)
        def _():
            cnt = cnt_scr[...]
            for i in range(g0, min(g0 + 8, n_blk)):
                vi = val[i:i + 1, :]
                ahead = (vi > val) | ((vi == val) & (blk > i))
                cnt = cnt + jnp.where(ahead, 1.0, 0.0)
            cnt_scr[...] = cnt
    cnt = cnt_scr[...]
    chosen = (cnt < float(N_SELECT)) & (val > -jnp.inf)
    self32 = jnp.where(chosen, 1.0, 0.0)
    if nbu < nbp:
        self32 = jnp.concatenate([self32, jnp.zeros((nbp - nbu, LANES), F32)], 0)
    sel_ref[...] = jnp.concatenate([self32[j * LANES:(j + 1) * LANES, :].T for j in range(nbp // LANES)], 1)
    if want_idx:
        blk_f = blk.astype(F32)
        rows = []
        for j in range(N_SELECT):
            hit = chosen & (cnt == float(j))
            rows.append(jnp.sum(jnp.where(hit, blk_f + 1.0, 0.0), 0, keepdims=True) - 1.0)
        idx_ref[...] = jnp.concatenate(rows, 0).astype(jnp.int32)
    else:
        idx_ref[...] = jnp.zeros(idx_ref.shape, jnp.int32)


def _topk(imp2d, pos0, t_rows, n_blk, want_idx):
    R, nbp = imp2d.shape
    tr = 4 * LANES if R % (4 * LANES) == 0 else LANES
    return pl.pallas_call(
        functools.partial(_topk_body, pos0=pos0, t_rows=t_rows, n_blk=n_blk, want_idx=want_idx),
        out_shape=(jax.ShapeDtypeStruct((R, nbp), F32), jax.ShapeDtypeStruct((N_SELECT, R), jnp.int32)),
        scratch_shapes=[pltpu.VMEM((_round_up(n_blk, 8), LANES), F32)],
        grid=(R // tr,),
        in_specs=[pl.BlockSpec((tr, nbp), lambda i: (i, 0))],
        out_specs=(pl.BlockSpec((tr, nbp), lambda i: (i, 0)), pl.BlockSpec((N_SELECT, tr), lambda i: (0, i))),
        compiler_params=_cparams(("parallel",)), name="topk")(imp2d)


def _attn_body(*refs, mode, pairs, tk, pos_q0, pos_k0, scale):
    if mode == 'slc':
        q_ref, kv_ref, sel_ref, blk_ref, o_ref = refs
    else:
        q_ref, kv_ref, o_ref = refs
    qi = pl.program_id(2)
    tq = q_ref.shape[1]
    n_k = kv_ref.shape[1]
    n_tiles = n_k // tk
    q = q_ref[0]
    pq0 = pos_q0 + qi * tq
    qpos = pq0 + lax.broadcasted_iota(jnp.int32, (tq, 1), 0)
    unroll = (4 if mode == 'slc' and n_tiles % 4 == 0 else 2) if n_tiles % 2 == 0 else 1
    if mode == 'none':
        lo, hi = 0, n_tiles // unroll
    else:
        r_hi = jnp.minimum(n_k - 1, pq0 + tq - 1 - pos_k0)
        hi = r_hi // (tk * unroll) + 1
        lo = jnp.maximum(0, pq0 - (WINDOW - 1) - pos_k0) // (tk * unroll) if mode == 'win' else 0
    qs = [q[:, q_lo:q_lo + HEAD_DIM] * scale for q_lo, _, _ in pairs]
    if mode == 'slc':
        nbp = sel_ref.shape[3]
        sel_bias = ((sel_ref[0, 0] - 1.0) * (-NEG_BIG)).astype(BF16)
        qs = [jnp.concatenate([sel_bias, qh.astype(BF16)], 1) for qh in qs]

    def step(it, carry, causal=True):
        tiles = []
        for u in range(unroll):
            r0 = pl.multiple_of((it * unroll + u) * tk, tk)
            kvt = kv_ref[0, pl.ds(r0, tk), :]
            kpos = pos_k0 + r0 + lax.broadcasted_iota(jnp.int32, (1, tk), 1)
            onehot = None
            if mode == 'win':
                d = qpos - kpos
                mask = (d >= 0) & (d < WINDOW) & (kpos >= 0)
            elif mode == 'slc':
                onehot = blk_ref[pl.ds(r0, tk), :]
                mask = (kpos <= qpos) if causal else None
            else:
                mask = None
            tiles.append((kvt, mask, onehot))
        new = [None] * (3 * len(pairs))

        def chain(p):
            _, k_lo, v_lo = pairs[p]
            m, l, acc = carry[3 * p:3 * p + 3]
            if mode == 'slc':
                ss = [_mm_nt(qs[p], jnp.concatenate([oh, kvt[:, k_lo:k_lo + HEAD_DIM].astype(BF16)], 1))
                      for kvt, _, oh in tiles]
            else:
                ss = [_mm_nt(qs[p], kvt[:, k_lo:k_lo + HEAD_DIM]) for kvt, _, _ in tiles]
            yield
            ss = [s if mask is None else jnp.where(mask, s, -jnp.inf) for s, (_, mask, _) in zip(ss, tiles)]
            m_new = m
            for s in ss:
                m_new = jnp.maximum(m_new, jnp.max(s, -1, keepdims=True))
            alpha = jnp.exp(m - m_new)
            m_wide = jnp.concatenate([m_new] * (tk // LANES), 1)
            prs = [jnp.exp(s - m_wide) for s in ss]
            yield
            l = alpha * l
            acc = alpha[:, :HEAD_DIM] * acc
            for pr, (kvt, _, _) in zip(prs, tiles):
                l = l + jnp.sum(pr, -1, keepdims=True)
                acc = acc + _mm(pr, kvt[:, v_lo:v_lo + HEAD_DIM])
            new[3 * p:3 * p + 3] = [m_new, l, acc]

        _run_interleaved([chain(p) for p in range(len(pairs))])
        return tuple(new)

    init = []
    for _ in pairs:
        init += [jnp.full((tq, LANES), NEG_BIG, F32), jnp.zeros((tq, LANES), F32), jnp.zeros((tq, HEAD_DIM), F32)]
    if mode == 'slc':
        hi_full = jnp.minimum(hi, (pq0 - pos_k0 + 1) // (tk * unroll))
        res = lax.fori_loop(lo, hi_full, functools.partial(step, causal=False), tuple(init))
        res = lax.fori_loop(hi_full, hi, step, res)
    else:
        res = lax.fori_loop(lo, hi, step, tuple(init))
    outs = []
    for p in range(len(pairs)):
        m, l, acc = res[3 * p:3 * p + 3]
        outs.append(jnp.where(m[:, :HEAD_DIM] > 0.5 * NEG_BIG, acc / jnp.maximum(l[:, :HEAD_DIM], 1e-30), 0.0))
    o_ref[0] = jnp.concatenate(outs, 1)


def _attn(q, kv, mode, pairs, kv_width, pos_q0, pos_k0, tq, tk, sel=None):
    B, T, _ = q.shape
    n_k = kv.shape[1]
    in_specs = [pl.BlockSpec((1, tq, LANES), lambda b, h, i: (b, i, h)),
                pl.BlockSpec((1, n_k, kv_width), lambda b, h, i: (b, 0, h))]
    args = [q, kv]
    if mode == 'slc':
        nbp = sel.shape[-1]
        in_specs.append(pl.BlockSpec((1, 1, tq, nbp), lambda b, h, i: (b, h, i, 0)))
        args.append(sel)
        key_blk = (pos_k0 + np.arange(n_k)) // SLC_BLOCK
        in_specs.append(pl.BlockSpec((n_k, nbp), lambda b, h, i: (0, 0)))
        args.append(jnp.asarray(key_blk[:, None] == np.arange(nbp)[None, :], dtype=BF16))
    return pl.pallas_call(
        functools.partial(_attn_body, mode=mode, pairs=pairs, tk=tk, pos_q0=pos_q0, pos_k0=pos_k0, scale=SCALE),
        out_shape=jax.ShapeDtypeStruct((B, T, GW), F32), grid=(B, 2, T // tq),
        in_specs=in_specs, out_specs=pl.BlockSpec((1, tq, LANES), lambda b, h, i: (b, i, h)),
        compiler_params=_cparams(("parallel", "parallel", "arbitrary")), name="attn_" + mode)(*args)


_GQA_PAIRS = ((0, 0, HEAD_DIM), (HEAD_DIM, 0, HEAD_DIM))
_MHA_PAIRS = ((0, 0, HEAD_DIM), (HEAD_DIM, 2 * HEAD_DIM, 3 * HEAD_DIM))


def _slc_paged_body(idx_ref, phys_ref, *refs, pos0, blk0, t_real):
    del phys_ref
    n_slots = t_real * N_SELECT
    q_ref = refs[0]
    blk_refs = refs[1:1 + n_slots]
    new_ref = refs[1 + n_slots]
    o_ref = refs[2 + n_slots]
    b, h = pl.program_id(0), pl.program_id(1)
    bpp = PAGE_SIZE // SLC_BLOCK
    tok = lax.broadcasted_iota(jnp.int32, (1, PAGE_SIZE), 1)
    lane = lax.broadcasted_iota(jnp.int32, (1, SLC_BLOCK), 1)
    newblk = new_ref[0]

    def chain(t):
        base = ((b * NSA_KV_HEADS + h) * t_real + t) * N_SELECT
        qrow = q_ref[0, t:t + 1, :]
        q2 = jnp.concatenate([qrow[:, :HEAD_DIM], qrow[:, HEAD_DIM:], jnp.zeros((6, HEAD_DIM), F32)], 0) * SCALE
        pos = pos0 + t
        scores, vts = [], []
        n_new = jnp.int32(0)
        for j in range(N_SELECT):
            idx = idx_ref[base + j]
            idc = jnp.maximum(idx, 0)
            kv_t = blk_refs[t * N_SELECT + j][0, 0, 0]
            ok = ((idx >= 0) & (idx < blk0) & (tok // SLC_BLOCK == idc % bpp)
                  & ((idc // bpp) * PAGE_SIZE + tok <= pos))
            scores.append(_mm(q2, kv_t[0]) + jnp.where(ok, 0.0, -jnp.inf))
            vts.append(kv_t[1])
            n_new = n_new + jnp.where(idx >= blk0, 1, 0)
        ok_new = (n_new > 0) & (blk0 * SLC_BLOCK + lane <= pos)
        s_new = _mm_nt(q2, newblk[:, :HEAD_DIM]) + jnp.where(ok_new, 0.0, -jnp.inf)
        yield
        m = jnp.max(s_new, -1, keepdims=True)
        for s in scores:
            m = jnp.maximum(m, jnp.max(s, -1, keepdims=True))
        m = jnp.where(m == -jnp.inf, 0.0, m)
        e_new = jnp.exp(s_new - m)
        den = jnp.sum(e_new, -1, keepdims=True)
        es = [jnp.exp(s - m) for s in scores]
        yield
        o = _mm(e_new, newblk[:, HEAD_DIM:])
        for e, vt in zip(es, vts):
            den = den + jnp.sum(e, -1, keepdims=True)
            o = o + _mm_nt(e, vt)
        o = o / jnp.maximum(den, 1e-30)
        orow = jnp.concatenate([o[0:1], o[1:2]], 1)
        o_ref[0, 0, t] = jnp.broadcast_to(orow, (8, LANES))

    _run_interleaved([chain(t) for t in range(t_real)])


def _slc_paged(q_rot, cache_t, new_rows, idx_flat, page_flat, layer, pos0, blk0, t_real):
    B = q_rot.shape[0]
    tp = q_rot.shape[1]
    n_slots = t_real * N_SELECT

    def blk_spec(s):
        def imap(b, h, idx, page):
            return (layer, page[(b * NSA_KV_HEADS + h) * n_slots + s], h, 0, 0, 0)
        return pl.BlockSpec((1, 1, 1, 2, HEAD_DIM, PAGE_SIZE), imap)

    gs = pltpu.PrefetchScalarGridSpec(
        num_scalar_prefetch=2, grid=(B, NSA_KV_HEADS),
        in_specs=[pl.BlockSpec((1, tp, LANES), lambda b, h, idx, page: (b, 0, h))]
        + [blk_spec(s) for s in range(n_slots)]
        + [pl.BlockSpec((1, SLC_BLOCK, LANES), lambda b, h, idx, page: (b, 0, h))],
        out_specs=pl.BlockSpec((1, 1, t_real, 8, LANES), lambda b, h, idx, page: (b, h, 0, 0, 0)))
    out = pl.pallas_call(
        functools.partial(_slc_paged_body, pos0=pos0, blk0=blk0, t_real=t_real),
        out_shape=jax.ShapeDtypeStruct((B, NSA_KV_HEADS, t_real, 8, LANES), F32), grid_spec=gs,
        compiler_params=_cparams(("parallel", "arbitrary")), name="slc_paged")(
            idx_flat, page_flat, q_rot, *([cache_t] * n_slots), new_rows)
    return jnp.transpose(out[:, :, :, 0, :], (0, 2, 1, 3)).reshape(B, t_real, GW)


def _nsa_prep_body(q_ref, s_ref, w_ref, cos_ref, sin_ref, g_ref, ones_ref, qn_ref, qr_ref, so_ref, wo_ref):
    lane = lax.broadcasted_iota(jnp.int32, (1, GW), 1)
    first_half = (lane % HEAD_DIM) < (HEAD_DIM // 2)
    is_k = (lane // HEAD_DIM) % 2 == 0
    cos, sin = cos_ref[...], sin_ref[...]
    ones = ones_ref[...]

    def norm(x, g):
        ms = _mm(x * x, ones, 'l3') * (1.0 / HEAD_DIM)
        return x * lax.rsqrt(ms + NORM_EPS) * g

    def rope(x):
        swapped = jnp.where(first_half, pltpu.roll(x, GW - HEAD_DIM // 2, 1), pltpu.roll(x, HEAD_DIM // 2, 1))
        return x * cos + swapped * sin

    qn = norm(q_ref[0], g_ref[0:1, :])
    qn_ref[0] = qn
    qr_ref[0] = rope(qn)
    for x_ref, o_ref, gi in ((s_ref, so_ref, 1), (w_ref, wo_ref, 2)):
        x = x_ref[0]
        o_ref[0] = jnp.where(is_k, rope(norm(x, g_ref[gi:gi + 1, :])), x)


def _nsa_prep(P3, n_rows, tr, cos, sin, g_blk, ones_bd):
    B = P3.shape[0]
    col = lambda c0: pl.BlockSpec((1, tr, GW), lambda b, i: (b, i, c0 // GW))
    tab = pl.BlockSpec((tr, GW), lambda b, i: (i, 0))
    out = pl.BlockSpec((1, tr, GW), lambda b, i: (b, i, 0))
    shp = jax.ShapeDtypeStruct((B, n_rows, GW), F32)
    return pl.pallas_call(
        _nsa_prep_body, out_shape=(shp, shp, shp, shp), grid=(B, n_rows // tr),
        in_specs=[col(C_NQ), col(C_NSLC), col(C_NWIN), tab, tab, pl.BlockSpec((8, GW), lambda b, i: (0, 0)),
                  pl.BlockSpec((GW, GW), lambda b, i: (0, 0))],
        out_specs=(out, out, out, out),
        compiler_params=_cparams(("parallel", "parallel")), name="nsa_prep")(P3, P3, P3, cos, sin, g_blk, ones_bd)


def _rms(x, g):
    return x * lax.rsqrt(jnp.mean(x * x, -1, keepdims=True) + NORM_EPS) * g


def _rope(x, pos):
    half = HEAD_DIM // 2
    inv = ROPE_THETA ** (-jnp.arange(half, dtype=F32) / half)
    ang = pos.astype(F32)[:, None] * inv
    cos, sin = jnp.cos(ang)[:, None, :], jnp.sin(ang)[:, None, :]
    x1, x2 = x[..., :half], x[..., half:]
    return jnp.concatenate([x1 * cos - x2 * sin, x1 * sin + x2 * cos], -1)


def _pad_t(x, tp, value=0.0):
    t = x.shape[1]
    if t == tp:
        return x
    return jnp.pad(x, ((0, 0), (0, tp - t)) + ((0, 0),) * (x.ndim - 2), constant_values=value)


def _round_up(n, m):
    return -(-n // m) * m


def _cover_matrix(n_cmp, n_cmp_pad, n_slc, nbp):
    start = np.arange(n_cmp_pad)[:, None] * CMP_STRIDE
    blk = np.arange(nbp)[None, :]
    cov = (start < (blk + 1) * SLC_BLOCK) & (start + CMP_BLOCK > blk * SLC_BLOCK)
    cov &= (np.arange(n_cmp_pad)[:, None] < n_cmp) & (blk < n_slc)
    return jnp.asarray(cov.astype(np.float32), dtype=BF16)


def _mlstm_mixer(P, Pp, conv_buf, c0, n0, m0, prm):
    B, T, _ = P.shape
    keep = MLSTM_CONV - 1
    tail = jnp.pad(conv_buf, ((0, 0), (8 - keep, 0), (0, 0)))
    m0p = jnp.pad(m0, ((0, 0), (0, LANES - N_HEADS))).reshape(B, 1, LANES)
    out, C, n, m = _mlstm(Pp, T, tail[..., :GW], tail[..., GW:], prm['mlstm_blk'], c0, n0, m0p)
    qk_raw = jnp.concatenate([P[:, -keep:, C_MQ:C_MQ + GW], P[:, -keep:, C_MK:C_MK + GW]], -1)
    conv_new = jnp.concatenate([conv_buf, qk_raw], 1)[:, -keep:]
    return out[:, :T], C, n, m[:, 0, :N_HEADS], conv_new


def _rwkv_mixer(P, Pp, shift_buf, s0, prm, prec):
    B, T, _ = P.shape
    out, S = _rwkv(Pp, T, shift_buf, prm['rwkv_blk'], prm['rwkv_lora'], s0, prec)
    shift_new = jnp.concatenate([P[:, -1:, C_RR:C_RR + 3 * GW], P[:, -1:, C_RL:C_RL + LANES]], -1)
    return out[:, :T], S, shift_new


def _memory_kv(mem, prm):
    B = mem.shape[0]
    kv = _proj_in(mem.reshape(B * N_MEM, D_MODEL), prm['mem_norm_g'], prm['w_mem_kv_bf16'], 2 * GW)
    kv = kv.reshape(B, N_MEM, MEM_HEADS, 2, HEAD_DIM)
    return jnp.stack([_rms(kv[:, :, :, 0], prm['mem_qk_g'][1]), kv[:, :, :, 1]], 3)


def _memory_mixer(P, mem_kv, prm):
    B, T, _ = P.shape
    tp = _round_up(T, 8)
    qn = _rms(P[..., C_CQ:C_CQ + GW].reshape(B, T, MEM_HEADS, HEAD_DIM), prm['mem_qk_g'][0]).reshape(B, T, GW)
    tq = min(256, tp)
    o = _attn(_pad_t(qn, tp), mem_kv.reshape(B, N_MEM, 2 * GW), 'none', _MHA_PAIRS, 2 * LANES, 0, 0, tq, N_MEM)
    return o[:, :T]


def _rope_tables(pos):
    half = HEAD_DIM // 2
    inv = ROPE_THETA ** (-jnp.arange(half, dtype=F32) / half)
    ang = pos.astype(F32)[:, None] * inv
    cos, sin = jnp.cos(ang), jnp.sin(ang)
    return (jnp.tile(jnp.concatenate([cos, cos], -1), (1, N_HEADS)),
            jnp.tile(jnp.concatenate([-sin, sin], -1), (1, N_HEADS)))


def _nsa_mixer(P2d, Pp, B, T, pos0, win_prefix, prm, past):
    P = P2d.reshape(B, T, DP)
    g = prm['nsa_qk_g']
    tp = _round_up(T, 8)
    tq = min(256, tp)
    cos, sin = _rope_tables(pos0 + jnp.arange(tp, dtype=jnp.int32))
    qn, q_rot, slc_rows, win_rows = _nsa_prep(Pp, tp, tq, cos, sin, prm['nsa_prep_blk'], prm['head_ones'])
    kvrows = lambda t: t[:, :T].reshape(B, T, NSA_KV_HEADS, 2, HEAD_DIM)
    cmp_new = kvrows(P[..., C_NCMP:C_NCMP + 2 * KVW])
    slc_new, win_new = kvrows(slc_rows), kvrows(win_rows)

    L_all = pos0 + T
    n_sub = max(-(-L_all // CMP_STRIDE), CMP_BLOCK // CMP_STRIDE)
    n_cmp = n_sub - 1
    bd, w2bd, pe_hid = prm['cmp_bd'], prm['cmp_w2bd'], prm['cmp_pe_hid']
    if past is None:
        G = _subproj(P2d, C_NCMP // LANES, bd).reshape(B, T // CMP_STRIDE, 4 * LANES)
    else:
        cache_cmp_t, cache_slc_t, page_table, layer = past
        g_pages = _subproj_pages(cache_cmp_t, page_table, layer, bd)
        new_rows = _pad_t(cmp_new.reshape(B, T, 2 * KVW), CMP_STRIDE).reshape(B * CMP_STRIDE, 2 * KVW)
        g_new = _subproj(new_rows, 0, bd).reshape(B, 1, 4 * LANES)
        G = jnp.concatenate([g_pages, g_new], 1)
    n_cmp_pad = _round_up(n_cmp, LANES)
    gb = G[:, 1:, 256:]
    ga = G[:, :, :256]
    fit = lambda t: _pad_t(t, max(n_cmp_pad, t.shape[1]))[:, :n_cmp_pad]
    hid = fit(ga) + fit(gb) + pe_hid
    kv_cmp = _cmp_mlp(hid, w2bd, g[1])
    n_slc = -(-L_all // SLC_BLOCK)
    nbp = _round_up(n_slc, LANES)
    cover = _cover_matrix(n_cmp, n_cmp_pad, n_slc, nbp)
    tq = min(256, tp)
    o_cmp, imp = _cmp_attn(qn, kv_cmp, cover, pos0, tq)

    R = B * NSA_KV_HEADS * tp
    rp = _round_up(R, LANES)
    imp2d = jnp.pad(imp.reshape(R, nbp), ((0, rp - R), (0, 0)))
    sel, idx_t = _topk(imp2d, pos0, tp, n_slc, want_idx=past is not None)

    if past is None:
        sel4 = sel[:R].reshape(B, NSA_KV_HEADS, tp, nbp)
        o_slc = _attn(q_rot, slc_new.reshape(B, T, 2 * KVW), 'slc', _GQA_PAIRS, LANES, pos0, pos0, tq,
                      min(256, T), sel=sel4)
    else:
        bpp = PAGE_SIZE // SLC_BLOCK
        idx = idx_t[:, :R].T.reshape(B, NSA_KV_HEADS, tp, N_SELECT)[:, :, :T]
        idc = jnp.clip(idx, 0, page_table.shape[1] * bpp - 1)
        page = page_table[jnp.arange(B)[:, None, None, None], idc // bpp]
        new_rows = _pad_t(slc_new.reshape(B, T, 2 * KVW), SLC_BLOCK)
        o_slc = _slc_paged(q_rot, cache_slc_t, new_rows, idx.reshape(-1), page.reshape(-1).astype(jnp.int32),
                           layer, pos0, pos0 // SLC_BLOCK, T)

    if win_prefix.shape[1] == 0:
        win_ctx = win_new
        pos_k0 = pos0
    else:
        win_ctx = jnp.concatenate([win_prefix, win_new], 1)
        pos_k0 = pos0 - win_prefix.shape[1]
    n_k = win_ctx.shape[1]
    tkw = min(256, _round_up(n_k, LANES))
    kv_win = _pad_t(win_ctx.reshape(B, n_k, 2 * KVW), _round_up(n_k, tkw))
    o_win = _attn(q_rot, kv_win, 'win', _GQA_PAIRS, LANES, pos0, pos_k0, tq, tkw)

    keep = win_prefix.shape[1] if past is not None else min(WINDOW, T)
    flat = lambda o: o[:, :T].reshape(B * T, GW)
    return (flat(o_cmp), flat(o_slc), flat(o_win)), cmp_new, slc_new, win_ctx[:, -keep:]


RWKV_PREC = ('r3', 'bf16', 'bf16')


def _layer(x, pos0, st, mem_kv, prm, past, rwkv_prec=RWKV_PREC):
    conv_buf, c0, n0, m0, s0, shift_buf, win_prefix = st
    B, T, _ = x.shape
    x2d = x.reshape(B * T, D_MODEL)
    P2d = _proj_in(x2d, prm['norm_g'], prm['w_in_bf16'], 640)
    P = P2d.reshape(B, T, DP)
    Pp = _pad_t(P, _round_up(T, max(MLSTM_L, RWKV_L)))
    y_m, C, n, m, conv_new = _mlstm_mixer(P, Pp, conv_buf, c0, n0, m0, prm)
    (o_cmp, o_slc, o_win), cmp_new, slc_new, win_new = _nsa_mixer(P2d, Pp, B, T, pos0, win_prefix, prm, past)
    y_r, S, shift_new = _rwkv_mixer(P, Pp, shift_buf, s0, prm, rwkv_prec)
    o_mem = _memory_mixer(P, mem_kv, prm)
    flat = lambda t: t.reshape(B * T, GW)
    out = _proj_out(x2d, P2d, flat(y_m), flat(y_r), o_cmp, o_slc, o_win, flat(o_mem), prm['gate_expand'],
                    prm['w_out_bf16']).reshape(B, T, D_MODEL)
    return out, (cmp_new, slc_new, win_new, C, n, m, conv_new, S, shift_new)


def _prep_params(l, p):
    prm = {k: v[l] for k, v in p.items()}
    pieces, start = [], 0
    while start < DP:
        end = start + 1
        while end < DP and (_SRC[end] == _SRC[end - 1] + 1 if _SRC[start] >= 0 else _SRC[end] < 0):
            end += 1
        if _SRC[start] >= 0:
            pieces.append(prm['w_in'][:, int(_SRC[start]):int(_SRC[start]) + end - start].astype(BF16))
        else:
            pieces.append(jnp.zeros((D_MODEL, end - start), BF16))
        start = end
    prm['w_in_bf16'] = jnp.concatenate(pieces, 1)
    prm['w_out_bf16'] = prm['w_out'].astype(BF16)
    wm = prm['w_mem_kv'].reshape(D_MODEL, 2, MEM_HEADS, HEAD_DIM)
    prm['w_mem_kv_bf16'] = jnp.transpose(wm, (0, 2, 1, 3)).reshape(D_MODEL, 2 * GW).astype(BF16)
    rows = lambda *vs: jnp.concatenate([jnp.pad(v.reshape(-1, v.shape[-1]), ((0, 0), (0, GW - v.shape[-1])))
                                        for v in vs], 0)
    pad16 = lambda blk: jnp.pad(blk, ((0, 16 - blk.shape[0]), (0, 0)))
    cw, cb = prm['mlstm_conv_w'], prm['mlstm_conv_b']
    prm['mlstm_blk'] = pad16(rows(cw[:, :GW], cw[:, GW:], cb[:GW], cb[GW:], prm['mlstm_norm_g'],
                                  prm['mlstm_gate_b'].reshape(1, 2 * N_HEADS)))
    mu = prm['rwkv_mu']
    prm['rwkv_blk'] = pad16(rows(mu[:GW], mu[GW:2 * GW], mu[2 * GW:3 * GW], mu[3 * GW:], prm['rwkv_w0'],
                                 prm['rwkv_a0'], prm['rwkv_kk'], prm['rwkv_ln']))
    g = prm['nsa_qk_g']
    one = jnp.ones((HEAD_DIM,), F32)
    prm['nsa_prep_blk'] = jnp.pad(rows(jnp.tile(g[0], N_HEADS), jnp.concatenate([g[2], one, g[2], one]),
                                       jnp.concatenate([g[3], one, g[3], one])), ((0, 5), (0, 0)))
    head_of = np.arange(GW) // HEAD_DIM
    prm['head_ones'] = jnp.asarray(head_of[:, None] == head_of[None, :], dtype=BF16)
    gate_lane = np.arange(LANES)[:, None]
    out_lane = np.arange(3 * GW)[None, :]
    prm['gate_expand'] = jnp.asarray(gate_lane == (out_lane // GW) * N_HEADS + (out_lane % GW) // HEAD_DIM, dtype=BF16)
    zl = jnp.zeros((DECAY_LORA, GW), F32)
    prm['rwkv_lora'] = jnp.concatenate([jnp.concatenate([prm['rwkv_w2'], zl], 1),
                                        jnp.concatenate([zl, prm['rwkv_a2']], 1)], 0)
    w1 = prm['nsa_cmp_w1']
    eye_h = jnp.eye(NSA_KV_HEADS, dtype=F32)
    eye_c = jnp.eye(2, dtype=F32)
    w1r = w1.reshape(2, 2, CMP_STRIDE, HEAD_DIM, HEAD_DIM)
    bd = jnp.einsum('cC,crsde->scdrCe', eye_c, w1r)
    prm['cmp_bd'] = bd.reshape(CMP_STRIDE // 2, 2 * LANES, 2 * LANES).astype(BF16)
    w2 = prm['nsa_cmp_w2']
    prm['cmp_w2bd'] = jnp.einsum('hH,cC,ced->hceHCd', eye_h, eye_c, w2).reshape(2 * KVW, 2 * KVW)
    pe_hid = jnp.einsum('csd,csde->ce', prm['nsa_pe'], w1, precision=HI)
    prm['cmp_pe_hid'] = jnp.tile(pe_hid.reshape(1, 2 * HEAD_DIM), (1, NSA_KV_HEADS)).reshape(2 * KVW)
    return prm


def kernel(x_prompt, x_sample, cache_cmp_kv, cache_slc_kv, cache_win_kv, cache_mem_kv, state_mlstm_C, state_mlstm_n, state_mlstm_m, state_mlstm_conv, state_rwkv_S, state_rwkv_shift, page_table, mem_prompt, norm_g, w_in, w_out, mlstm_conv_w, mlstm_conv_b, mlstm_gate_b, mlstm_norm_g, nsa_qk_g, nsa_pe, nsa_cmp_w1, nsa_cmp_w2, rwkv_mu, rwkv_w0, rwkv_w2, rwkv_a0, rwkv_a2, rwkv_kk, rwkv_ln, mem_norm_g, w_mem_kv, mem_qk_g):
    params = dict(norm_g=norm_g, w_in=w_in, w_out=w_out, mlstm_conv_w=mlstm_conv_w, mlstm_conv_b=mlstm_conv_b,
                  mlstm_gate_b=mlstm_gate_b, mlstm_norm_g=mlstm_norm_g, nsa_qk_g=nsa_qk_g, nsa_pe=nsa_pe,
                  nsa_cmp_w1=nsa_cmp_w1, nsa_cmp_w2=nsa_cmp_w2, rwkv_mu=rwkv_mu, rwkv_w0=rwkv_w0, rwkv_w2=rwkv_w2,
                  rwkv_a0=rwkv_a0, rwkv_a2=rwkv_a2, rwkv_kk=rwkv_kk, rwkv_ln=rwkv_ln, mem_norm_g=mem_norm_g,
                  w_mem_kv=w_mem_kv, mem_qk_g=mem_qk_g)
    depth = norm_g.shape[0]
    B = x_prompt.shape[0]
    past_len = page_table.shape[1] * PAGE_SIZE
    cache_cmp_t = jnp.transpose(cache_cmp_kv, (0, 1, 3, 4, 5, 2))
    cache_slc_t = jnp.transpose(cache_slc_kv, (0, 1, 3, 4, 5, 2))
    xp, xs = x_prompt, x_sample
    new_p, new_s, new_mem = [], [], []
    for l in range(depth):
        prm = _prep_params(l, params)
        mem_kv_p = _memory_kv(mem_prompt, prm)
        st_p = (jnp.zeros((B, MLSTM_CONV - 1, 2 * GW), F32),
                jnp.zeros((B, N_HEADS, HEAD_DIM, HEAD_DIM), F32),
                jnp.zeros((B, N_HEADS, HEAD_DIM), F32),
                jnp.full((B, N_HEADS), M_INIT, F32),
                jnp.zeros((B, N_HEADS, HEAD_DIM, HEAD_DIM), F32),
                jnp.zeros((B, 1, RWKV_SHIFT), F32),
                jnp.zeros((B, 0, NSA_KV_HEADS, 2, HEAD_DIM), F32))
        xp, sp = _layer(xp, 0, st_p, mem_kv_p, prm, None)
        st_s = (state_mlstm_conv[l], state_mlstm_C[l], state_mlstm_n[l], state_mlstm_m[l],
                state_rwkv_S[l], state_rwkv_shift[l], cache_win_kv[l])
        xs, ss = _layer(xs, past_len, st_s, cache_mem_kv[l], prm, (cache_cmp_t, cache_slc_t, page_table, l))
        new_p.append(sp)
        new_s.append(ss)
        new_mem.append(mem_kv_p)
    stack = lambda states, i: jnp.stack([s[i] for s in states])
    outs = [xp, xs]
    for i in range(3):
        outs += [stack(new_p, i), stack(new_s, i)]
    outs.append(jnp.stack(new_mem))
    for i in range(3, 9):
        outs += [stack(new_p, i), stack(new_s, i)]
    return tuple(outs)
```

```python
import functools

import numpy as np
import jax
import jax.numpy as jnp
from jax import lax
from jax.experimental import pallas as pl
from jax.experimental.pallas import tpu as pltpu

F32 = jnp.float32
BF16 = jnp.bfloat16
HI = lax.Precision.HIGHEST

D_MODEL = 1024
PAGE_SIZE = 128
HEAD_DIM = 64
GW = D_MODEL // 4
N_HEADS = GW // HEAD_DIM
SCALE = HEAD_DIM ** -0.5
MLSTM_CONV = 4
M_INIT = -1e30
NSA_KV_HEADS = 2
KVW = NSA_KV_HEADS * HEAD_DIM
CMP_BLOCK = 32
CMP_STRIDE = 16
SLC_BLOCK = 64
N_SELECT = 16
WINDOW = 512
DECAY_LORA = 64
AAA_LORA = 64
RWKV_SHIFT = 3 * GW + DECAY_LORA + AAA_LORA
RWKV_LN_EPS = HEAD_DIM * 1e-5
N_MEM = 256
MEM_HEADS = 4
ROPE_THETA = 10000.0
NORM_EPS = 1e-6

LANES = 128
MLSTM_L = 128
RWKV_L = 64
NEG_BIG = -1e30

_M0, _N0, _R0, _C0 = 0, 1288, 2580, 3732
C_MQ, C_MK, C_MV, C_MO, C_MZ = 0, 256, 512, 768, 1024
C_NQ, C_NCMP, C_NSLC, C_NWIN, C_NZ = 1280, 1536, 1792, 2048, 2304
C_RR, C_RK, C_RV, C_RZ = 2560, 2816, 3072, 3328
C_CQ, C_CZ = 3584, 3840
C_RL = 4096
C_MG = 4224
C_NG = 4352
DP = 4480


def _packed_src():
    src = -np.ones((DP,), np.int64)

    def put(dst, lo, n):
        src[dst:dst + n] = np.arange(lo, lo + n)

    put(C_MQ, _M0, 256); put(C_MK, _M0 + 256, 256); put(C_MV, _M0 + 512, 256)
    put(C_MG, _M0 + 768, 8); put(C_MO, _M0 + 776, 256); put(C_MZ, _M0 + 1032, 256)
    put(C_NQ, _N0, 256)
    for i, base in enumerate((C_NCMP, C_NSLC, C_NWIN)):
        ksrc = _N0 + 256 + 256 * i
        vsrc = ksrc + 128
        for h in range(2):
            put(base + 128 * h, ksrc + 64 * h, 64)
            put(base + 128 * h + 64, vsrc + 64 * h, 64)
    put(C_NG, _N0 + 1024, 12); put(C_NZ, _N0 + 1036, 256)
    put(C_RR, _R0, 256); put(C_RK, _R0 + 256, 256); put(C_RV, _R0 + 512, 256)
    put(C_RL, _R0 + 768, 128); put(C_RZ, _R0 + 896, 256)
    put(C_CQ, _C0, 256); put(C_CZ, _C0 + 256, 256)
    return src


_SRC = _packed_src()


def _split2(a):
    hi = a.astype(BF16)
    return hi, (a - hi.astype(F32)).astype(BF16)


def _dg(a, b, dims, prec):
    dn = (dims, ((), ()))
    if prec == 'bf16':
        return lax.dot_general(a.astype(BF16), b.astype(BF16), dn, preferred_element_type=F32)
    d = lambda x, y: lax.dot_general(x, y, dn, preferred_element_type=F32)
    if prec == 'x3':
        ah, al = _split2(a)
        bh, bl = _split2(b)
        return d(ah, bh) + (d(ah, bl) + d(al, bh))
    if prec in ('r3', 'l3'):
        exact, other = (a, b) if prec == 'r3' else (b, a)
        o1, rest = other.astype(BF16), None
        rest = other - o1.astype(F32)
        o2 = rest.astype(BF16)
        o3 = (rest - o2.astype(F32)).astype(BF16)
        e = exact.astype(BF16)
        if prec == 'r3':
            return d(e, o1) + (d(e, o2) + d(e, o3))
        return d(o1, e) + (d(o2, e) + d(o3, e))
    return lax.dot_general(a, b, dn, preferred_element_type=F32, precision=prec)


def _mm(a, b, prec=None):
    return _dg(a, b, ((1,), (0,)), prec)


def _mm_nt(a, b, prec=None):
    return _dg(a, b, ((1,), (1,)), prec)


def _mm_tn(a, b, prec=None):
    return _dg(a, b, ((0,), (0,)), prec)


def _run_interleaved(chains):
    chains = list(chains)
    while chains:
        alive = []
        for ch in chains:
            try:
                next(ch)
                alive.append(ch)
            except StopIteration:
                pass
        chains = alive


def _cparams(sem, vmem_mb=None):
    kw = dict(dimension_semantics=sem)
    if vmem_mb is not None:
        kw['vmem_limit_bytes'] = vmem_mb * 1024 * 1024
    return pltpu.CompilerParams(**kw)


def _proj_in_body(x_ref, g_ref, w_ref, o_ref, *, tn):
    x = x_ref[...]
    h = (x * lax.rsqrt(jnp.mean(x * x, -1, keepdims=True) + NORM_EPS) * g_ref[...]).astype(BF16)
    for j in range(o_ref.shape[1] // tn):
        o_ref[:, j * tn:(j + 1) * tn] = jnp.dot(h, w_ref[:, j * tn:(j + 1) * tn], preferred_element_type=F32)


def _proj_in(x2d, g, w_bf16, tn):
    n, d = x2d.shape
    dn = w_bf16.shape[1]
    tm = min(512, n)
    return pl.pallas_call(
        functools.partial(_proj_in_body, tn=tn), out_shape=jax.ShapeDtypeStruct((n, dn), F32), grid=(n // tm,),
        in_specs=[pl.BlockSpec((tm, d), lambda i: (i, 0)),
                  pl.BlockSpec((1, d), lambda i: (0, 0)),
                  pl.BlockSpec((d, dn), lambda i: (0, 0))],
        out_specs=pl.BlockSpec((tm, dn), lambda i: (i, 0)),
        compiler_params=_cparams(("parallel",), 56), name="proj_in")(x2d, g.reshape(1, d), w_bf16)


def _silu(x):
    return x * jax.nn.sigmoid(x)


def _proj_out_body(x_ref, ym_ref, yr_ref, oc_ref, os_ref, ow_ref, om_ref, g_ref, zn_ref, zc_ref, e_ref, w_ref, o_ref):
    ge = _mm(jax.nn.sigmoid(g_ref[...]), e_ref[...], 'l3')
    y_n = (ge[:, :GW] * oc_ref[...] + ge[:, GW:2 * GW] * os_ref[...] + ge[:, 2 * GW:] * ow_ref[...]) * _silu(zn_ref[...])
    y_c = om_ref[...] * _silu(zc_ref[...])
    acc = x_ref[...]
    for i, y in enumerate((ym_ref[...], y_n, yr_ref[...], y_c)):
        acc = acc + jnp.dot(y.astype(BF16), w_ref[i * GW:(i + 1) * GW, :], preferred_element_type=F32)
    o_ref[...] = acc


def _proj_out(x2d, P2d, y_m, y_r, o_cmp, o_slc, o_win, o_mem, gate_expand, w_bf16):
    n, d = x2d.shape
    tm = min(512, n)
    yspec = pl.BlockSpec((tm, GW), lambda i: (i, 0))
    return pl.pallas_call(
        _proj_out_body, out_shape=jax.ShapeDtypeStruct((n, d), F32), grid=(n // tm,),
        in_specs=[pl.BlockSpec((tm, d), lambda i: (i, 0)), yspec, yspec, yspec, yspec, yspec, yspec,
                  pl.BlockSpec((tm, LANES), lambda i: (i, C_NG // LANES)),
                  pl.BlockSpec((tm, GW), lambda i: (i, C_NZ // GW)), pl.BlockSpec((tm, GW), lambda i: (i, C_CZ // GW)),
                  pl.BlockSpec((LANES, 3 * GW), lambda i: (0, 0)), pl.BlockSpec((d, d), lambda i: (0, 0))],
        out_specs=pl.BlockSpec((tm, d), lambda i: (i, 0)),
        compiler_params=_cparams(("parallel",), 48), name="proj_out")(
            x2d, y_m, y_r, o_cmp, o_slc, o_win, o_mem, P2d, P2d, P2d, gate_expand, w_bf16)


_MP_CONV_Q, _MP_CONV_K, _MP_BIAS_Q, _MP_BIAS_K, _MP_NORM, _MP_GATE_B = 0, 4, 8, 9, 10, 11


def _mlstm_body(q_ref, k_ref, v_ref, o_ref, z_ref, g_ref, tq_ref, tk_ref, prm_ref, c0_ref, n0_ref, m0_ref,
                h_ref, c_ref, n_ref, m_ref, c_scr, n_scr, m_scr, pq_scr, pk_scr, *, t_valid):
    c = pl.program_id(1)
    nb, L = q_ref.shape[0], q_ref.shape[1]

    @pl.when(c == 0)
    def _():
        c_scr[...] = c0_ref[...]
        n_scr[...] = n0_ref[...]
        m_scr[...] = m0_ref[...]
        pq_scr[...] = jnp.zeros(pq_scr.shape, F32)
        pk_scr[...] = jnp.zeros(pk_scr.shape, F32)
        pq_scr[:, L - 8:L, :] = tq_ref[...]
        pk_scr[:, L - 8:L, :] = tk_ref[...]

    row = lax.broadcasted_iota(jnp.int32, (L, L), 0)
    col = lax.broadcasted_iota(jnp.int32, (L, L), 1)
    causal = row >= col
    tril = causal.astype(F32)
    triu = (row <= col).astype(F32)
    trow = lax.broadcasted_iota(jnp.int32, (L, 1), 0)
    valid = (c * L + trow) < t_valid
    lane = lax.broadcasted_iota(jnp.int32, (1, LANES), 1)
    prm = prm_ref[...]

    def conv(x, prev, w0, b):
        acc = prm[b:b + 1, :] + prm[w0 + MLSTM_CONV - 1:w0 + MLSTM_CONV, :] * x
        for s in range(1, MLSTM_CONV):
            shifted = jnp.where(trow >= s, pltpu.roll(x, s, 0), pltpu.roll(prev, s, 0))
            acc = acc + prm[w0 + MLSTM_CONV - 1 - s:w0 + MLSTM_CONV - s, :] * shifted
        return _silu(acc)

    cums, qs, ks = [], [], []
    for gi in range(nb):
        q_raw, k_raw = q_ref[gi], k_ref[gi]
        qs.append(conv(q_raw, pq_scr[gi], _MP_CONV_Q, _MP_BIAS_Q) * (HEAD_DIM ** -0.5))
        ks.append(conv(k_raw, pk_scr[gi], _MP_CONV_K, _MP_BIAS_K))
        pq_scr[gi] = q_raw
        pk_scr[gi] = k_raw
        x = g_ref[gi] + prm[_MP_GATE_B:_MP_GATE_B + 1, :LANES]
        log_f = jnp.minimum(x, 0.0) - jnp.log1p(jnp.exp(-jnp.abs(x)))
        g = jnp.where(lane < N_HEADS, x, jnp.where(lane < 2 * N_HEADS, log_f, 0.0))
        g = jnp.where(valid, g, jnp.where(lane < N_HEADS, NEG_BIG, 0.0))
        gt = g.T
        cums.append((g, gt, _mm(tril, g, 'r3'), _mm(gt, triu, 'l3')))

    def chain(gi, h):
        g, gt, bc, br = cums[gi]
        sl = slice(h * HEAD_DIM, (h + 1) * HEAD_DIM)
        qh = qs[gi][:, sl]
        kh = ks[gi][:, sl]
        vh = v_ref[gi, :, sl]
        qk = _mm_nt(qh, kh)
        ch = c_scr[gi, h]
        qc = _mm(qh, ch)
        yield
        b_col = bc[:, 4 + h:5 + h]
        li_col = g[:, h:h + 1]
        b_row = br[4 + h:5 + h, :]
        li_row = gt[h:h + 1, :]
        m_prev = m_scr[gi, :, h:h + 1]
        log_d = jnp.where(causal, b_col - b_row + li_row, -jnp.inf)
        log_inter = b_col + m_prev
        m_t = jnp.maximum(jnp.max(log_d, -1, keepdims=True), log_inter)
        s = qk * jnp.exp(log_d - m_t)
        w_inter = jnp.exp(log_inter - m_t)
        nh = n_scr[gi, h:h + 1, :]
        sv = _mm(s, vh)
        yield
        num = sv + w_inter * qc
        den = jnp.sum(s, -1, keepdims=True) + w_inter * jnp.sum(qh * nh, -1, keepdims=True)
        hh = num / jnp.maximum(jnp.abs(den), jnp.exp(-m_t))
        hn = hh * lax.rsqrt(jnp.mean(hh * hh, -1, keepdims=True) + NORM_EPS) * prm[_MP_NORM:_MP_NORM + 1, sl]
        h_ref[gi, :, sl] = jax.nn.sigmoid(o_ref[gi, :, sl]) * hn * _silu(z_ref[gi, :, sl])
        b_end = b_col[L - 1:L, :]
        log_w = b_end - b_col + li_col
        m_new = jnp.maximum(b_end + m_prev, jnp.max(log_w, 0, keepdims=True))
        wk = jnp.exp(log_w - m_new)
        decay = jnp.exp(b_end + m_prev - m_new)
        kw = kh * wk
        c_scr[gi, h] = decay * ch + _mm_tn(kw, vh)
        n_scr[gi, h:h + 1, :] = decay * nh + jnp.sum(kw, 0, keepdims=True)
        m_scr[gi, :, h:h + 1] = m_new

    _run_interleaved([chain(gi, h) for gi in range(nb) for h in range(N_HEADS)])

    @pl.when(c == pl.num_programs(1) - 1)
    def _():
        c_ref[...] = c_scr[...]
        n_ref[...] = n_scr[...]
        m_ref[...] = m_scr[...]


BATCH_ROWS = 4


def _mlstm(P, t_valid, tail_q, tail_k, prm_blk, c0, n0, m0):
    B, tp, _ = P.shape
    L = MLSTM_L
    nb = BATCH_ROWS if B % BATCH_ROWS == 0 else 1
    col = lambda c0_: pl.BlockSpec((nb, L, GW), lambda b, c: (b, c, c0_ // GW))
    tok = pl.BlockSpec((nb, L, GW), lambda b, c: (b, c, 0))
    tail = pl.BlockSpec((nb, 8, GW), lambda b, c: (b, 0, 0))
    sc = pl.BlockSpec((nb, N_HEADS, HEAD_DIM, HEAD_DIM), lambda b, c: (b, 0, 0, 0))
    sn = pl.BlockSpec((nb, N_HEADS, HEAD_DIM), lambda b, c: (b, 0, 0))
    sm = pl.BlockSpec((nb, 1, LANES), lambda b, c: (b, 0, 0))
    return pl.pallas_call(
        functools.partial(_mlstm_body, t_valid=t_valid),
        out_shape=(jax.ShapeDtypeStruct((B, tp, GW), F32), jax.ShapeDtypeStruct(c0.shape, F32),
                   jax.ShapeDtypeStruct(n0.shape, F32), jax.ShapeDtypeStruct(m0.shape, F32)),
        grid=(B // nb, -(-t_valid // L)),
        in_specs=[col(C_MQ), col(C_MK), col(C_MV), col(C_MO), col(C_MZ),
                  pl.BlockSpec((nb, L, LANES), lambda b, c: (b, c, C_MG // LANES)), tail, tail,
                  pl.BlockSpec((16, GW), lambda b, c: (0, 0)), sc, sn, sm],
        out_specs=(tok, sc, sn, sm),
        scratch_shapes=[pltpu.VMEM((nb, N_HEADS, HEAD_DIM, HEAD_DIM), F32), pltpu.VMEM((nb, N_HEADS, HEAD_DIM), F32),
                        pltpu.VMEM((nb, 1, LANES), F32), pltpu.VMEM((nb, L, GW), F32), pltpu.VMEM((nb, L, GW), F32)],
        compiler_params=_cparams(("parallel", "arbitrary")), name="mlstm")(
            P, P, P, P, P, P, tail_q, tail_k, prm_blk, c0, n0, m0)


(_RP_MU_R, _RP_MU_K, _RP_MU_V, _RP_MU_L, _RP_W0, _RP_A0, _RP_KK, _RP_KA, _RP_RK, _RP_LN_G, _RP_LN_B) = range(11)


def _rwkv_body(r_ref, k_ref, v_ref, z_ref, l_ref, sr_ref, sk_ref, sv_ref, sl_ref, prm_ref, lora_ref, s0_ref,
               y_ref, s_ref, s_scr, cr_scr, ck_scr, cv_scr, cl_scr, *, prec, t_valid):
    c_id = pl.program_id(1)

    @pl.when(c_id == 0)
    def _():
        s_scr[...] = s0_ref[...]
        cr_scr[...] = sr_ref[...]
        ck_scr[...] = sk_ref[...]
        cv_scr[...] = sv_ref[...]
        cl_scr[...] = sl_ref[...]

    nb, L = r_ref.shape[0], r_ref.shape[1]
    D = HEAD_DIM
    row = lax.broadcasted_iota(jnp.int32, (L, L), 0)
    col = lax.broadcasted_iota(jnp.int32, (L, L), 1)
    lower = row >= col
    strict = row > col
    tril = lower.astype(F32)
    n_sq = int(np.log2(L)) - 1
    pc, pa, prec = prec
    trow = lax.broadcasted_iota(jnp.int32, (L, 1), 0)
    valid = (c_id * L + trow) < t_valid
    prm = prm_ref[...]
    prow = lambda i, n=GW: prm[i:i + 1, :n]

    def shifted_mix(x_ref, carry_scr, g, mu):
        x = x_ref[g]
        prev = jnp.where(trow >= 1, pltpu.roll(x, 1, 0), carry_scr[g])
        carry_scr[g] = x[L - 1:L, :]
        return x + (prev - x) * mu

    prep = []
    for g in range(nb):
        r = shifted_mix(r_ref, cr_scr, g, prow(_RP_MU_R))
        k = shifted_mix(k_ref, ck_scr, g, prow(_RP_MU_K))
        v = shifted_mix(v_ref, cv_scr, g, prow(_RP_MU_V))
        lo = shifted_mix(l_ref, cl_scr, g, prow(_RP_MU_L, LANES))
        lo_in = jnp.where(lax.broadcasted_iota(jnp.int32, (1, LANES), 1) < DECAY_LORA, jnp.tanh(lo), lo)
        lora = _mm(lo_in, lora_ref[...])
        w_pre = prow(_RP_W0) + lora[:, :GW]
        log_w = -(float(np.exp(-0.5)) * jax.nn.sigmoid(w_pre))
        a = jax.nn.sigmoid(prow(_RP_A0) + lora[:, GW:])
        k_eff = k * (1.0 + (a - 1.0) * prow(_RP_KA))
        kk_raw = k * prow(_RP_KK)
        log_w = jnp.where(valid, log_w, 0.0)
        k_eff = jnp.where(valid, k_eff, 0.0)
        kk_raw = jnp.where(valid, kk_raw, 0.0)
        v = jnp.where(valid, v, 0.0)
        prep.append((r, log_w, k_eff, v, kk_raw, a))

    def chain(g, h):
        sl = slice(h * D, (h + 1) * D)
        r, w, k, v, kk, a = (x[:, sl] for x in prep[g])
        kk = kk / jnp.maximum(jnp.sqrt(jnp.sum(kk * kk, -1, keepdims=True)), 1e-12)
        cum = _mm(tril, w, pc)
        yield
        c_last = cum[L - 1:L, :]
        e_neg = jnp.exp(-cum)
        kh = kk * jnp.exp(cum - w)
        bt = kk * a * e_neg
        kt = k * e_neg
        rh = r * jnp.exp(cum)
        gram = _mm_nt(jnp.concatenate([kh, rh], 0), jnp.concatenate([bt, kt], 0), pa)
        yield
        A = jnp.where(strict, gram[:L, :L], 0.0)
        Bm = jnp.where(strict, gram[:L, L:], 0.0)
        Mb = jnp.where(lower, gram[L:, :L], 0.0)
        Mk = jnp.where(lower, gram[L:, L:], 0.0)
        X = jnp.concatenate([kh, _mm(Bm, v, pa)], 1)
        yield
        Pw = A
        X = X - _mm(Pw, X, pa)
        yield
        for _ in range(n_sq):
            Pw = _mm(Pw, Pw, pa)
            yield
            X = X + _mm(Pw, X, pa)
            yield
        e_end = jnp.exp(c_last - cum)
        bp = kk * a * e_end
        kp = k * e_end
        xtb = _mm_tn(X, bp, prec)
        yield
        wtb = xtb[:D]
        N = _mm_tn(v, kp, prec) - xtb[D:]
        yield
        mbx = _mm(Mb, X, prec)
        yield
        qp = rh - mbx[:, :D]
        y0 = _mm(Mk, v, prec) - mbx[:, D:]
        yield
        s0 = s_scr[g, h]
        y = _mm_nt(qp, s0, prec) + y0
        yield
        s_scr[g, h] = s0 * jnp.exp(c_last) - _mm(s0, wtb, prec) + N
        mu = jnp.mean(y, -1, keepdims=True)
        var = jnp.mean(jnp.square(y - mu), -1, keepdims=True)
        y = (y - mu) * lax.rsqrt(var + RWKV_LN_EPS) * prm[_RP_LN_G:_RP_LN_G + 1, sl] + prm[_RP_LN_B:_RP_LN_B + 1, sl]
        y = y + jnp.sum(r * k * prm[_RP_RK:_RP_RK + 1, sl], -1, keepdims=True) * v
        y_ref[g, :, sl] = y * _silu(z_ref[g, :, sl])

    _run_interleaved([chain(g, h) for g in range(nb) for h in range(N_HEADS)])

    @pl.when(c_id == pl.num_programs(1) - 1)
    def _():
        s_ref[...] = s_scr[...]


def _rwkv(P, t_valid, shift_buf, prm_blk, lora_w, s0, prec):
    B, tp, _ = P.shape
    L = RWKV_L
    nb = BATCH_ROWS if B % BATCH_ROWS == 0 else 1
    col = lambda c0_: pl.BlockSpec((nb, L, GW), lambda b, c: (b, c, c0_ // GW))
    tok = pl.BlockSpec((nb, L, GW), lambda b, c: (b, c, 0))
    car = lambda w: pl.BlockSpec((nb, 1, w), lambda b, c: (b, 0, 0))
    st = pl.BlockSpec((nb, N_HEADS, HEAD_DIM, HEAD_DIM), lambda b, c: (b, 0, 0, 0))
    shifts = [shift_buf[..., i * GW:(i + 1) * GW] for i in range(3)] + [shift_buf[..., 3 * GW:]]
    return pl.pallas_call(
        functools.partial(_rwkv_body, prec=prec, t_valid=t_valid),
        out_shape=(jax.ShapeDtypeStruct((B, tp, GW), F32), jax.ShapeDtypeStruct(s0.shape, F32)),
        grid=(B // nb, -(-t_valid // L)),
        in_specs=[col(C_RR), col(C_RK), col(C_RV), col(C_RZ),
                  pl.BlockSpec((nb, L, LANES), lambda b, c: (b, c, C_RL // LANES)),
                  car(GW), car(GW), car(GW), car(LANES),
                  pl.BlockSpec((16, GW), lambda b, c: (0, 0)), pl.BlockSpec((LANES, 2 * GW), lambda b, c: (0, 0)), st],
        out_specs=(tok, st),
        scratch_shapes=[pltpu.VMEM((nb, N_HEADS, HEAD_DIM, HEAD_DIM), F32), pltpu.VMEM((nb, 1, GW), F32),
                        pltpu.VMEM((nb, 1, GW), F32), pltpu.VMEM((nb, 1, GW), F32), pltpu.VMEM((nb, 1, LANES), F32)],
        compiler_params=_cparams(("parallel", "arbitrary")), name="rwkv")(
            P, P, P, P, P, *shifts, prm_blk, lora_w, s0)


def _subproj_accumulate(load_rows, w_ref, o_ref):
    n = o_ref.shape[-2]
    accs = []
    for h in range(NSA_KV_HEADS):
        acc = jnp.zeros((n, 2 * LANES), F32)
        for s in range(0, CMP_STRIDE, 2):
            xs = jnp.concatenate([load_rows(h, s, n), load_rows(h, s + 1, n)], 1)
            acc = acc + jnp.dot(xs.astype(BF16), w_ref[s // 2], preferred_element_type=F32)
        accs.append(acc)
    out = jnp.concatenate([accs[0][:, :LANES], accs[1][:, :LANES], accs[0][:, LANES:], accs[1][:, LANES:]], 1)
    o_ref[...] = out.reshape(o_ref.shape)


def _subproj_body(x0_ref, x1_ref, w_ref, o_ref):
    xs = (x0_ref, x1_ref)
    _subproj_accumulate(lambda h, s, n: xs[h][pl.ds(s, n, stride=CMP_STRIDE), :], w_ref, o_ref)


def _subproj(rows2d, col0, w):
    n = rows2d.shape[0]
    tm = min(2048, n)
    return pl.pallas_call(
        _subproj_body, out_shape=jax.ShapeDtypeStruct((n // CMP_STRIDE, 4 * LANES), F32), grid=(n // tm,),
        in_specs=[pl.BlockSpec((tm, LANES), lambda i: (i, col0)), pl.BlockSpec((tm, LANES), lambda i: (i, col0 + 1)),
                  pl.BlockSpec((CMP_STRIDE // 2, 2 * LANES, 2 * LANES), lambda i: (0, 0, 0))],
        out_specs=pl.BlockSpec((tm // CMP_STRIDE, 4 * LANES), lambda i: (i, 0)),
        compiler_params=_cparams(("parallel",)), name="cmp_subproj")(rows2d, rows2d, w)


_PAGES_PER_STEP = 32


def _subproj_pages_body(pt_ref, *refs):
    del pt_ref
    npg = len(refs) - 3
    w_ref, o_ref, rows_scr = refs[npg], refs[npg + 1], refs[npg + 2]
    for p in range(npg):
        for h in range(NSA_KV_HEADS):
            rows_scr[h, p * PAGE_SIZE:(p + 1) * PAGE_SIZE, :] = refs[p][0, 0, h].reshape(2 * HEAD_DIM, PAGE_SIZE).T
    _subproj_accumulate(lambda h, s, n: rows_scr[h, pl.ds(s, n, stride=CMP_STRIDE), :], w_ref, o_ref)


def _subproj_pages(cache_t, page_table, layer, w):
    B, n_pages = page_table.shape
    npg = min(_PAGES_PER_STEP, n_pages)
    spp = PAGE_SIZE // CMP_STRIDE

    def page_spec(p):
        return pl.BlockSpec((1, 1, NSA_KV_HEADS, 2, HEAD_DIM, PAGE_SIZE),
                            lambda b, g, pt: (layer, pt[b, g * npg + p], 0, 0, 0, 0))

    gs = pltpu.PrefetchScalarGridSpec(
        num_scalar_prefetch=1, grid=(B, n_pages // npg),
        in_specs=[page_spec(p) for p in range(npg)]
        + [pl.BlockSpec((CMP_STRIDE // 2, 2 * LANES, 2 * LANES), lambda b, g, pt: (0, 0, 0))],
        out_specs=pl.BlockSpec((1, npg * spp, 4 * LANES), lambda b, g, pt: (b, g, 0)),
        scratch_shapes=[pltpu.VMEM((NSA_KV_HEADS, npg * PAGE_SIZE, LANES), F32)])
    return pl.pallas_call(
        _subproj_pages_body, out_shape=jax.ShapeDtypeStruct((B, n_pages * spp, 4 * LANES), F32), grid_spec=gs,
        compiler_params=_cparams(("parallel", "arbitrary"), 48), name="cmp_subproj_pages")(
            page_table, *([cache_t] * npg), w)


def _cmp_mlp_body(h_ref, w2_ref, g_ref, o_ref):
    x = h_ref[0]
    kv = _mm(jax.nn.gelu(x), w2_ref[...])
    g = g_ref[...]
    segs = []
    for j in range(4):
        seg = kv[:, j * HEAD_DIM:(j + 1) * HEAD_DIM]
        if j % 2 == 0:
            seg = seg * lax.rsqrt(jnp.mean(seg * seg, -1, keepdims=True) + NORM_EPS) * g
        segs.append(seg)
    o_ref[0] = jnp.concatenate(segs, 1)


def _cmp_mlp(hid, w2bd, g):
    B, n, _ = hid.shape
    tn = min(512, n)
    return pl.pallas_call(
        _cmp_mlp_body, out_shape=jax.ShapeDtypeStruct((B, n, 256), F32), grid=(B, n // tn),
        in_specs=[pl.BlockSpec((1, tn, 256), lambda b, i: (b, i, 0)), pl.BlockSpec((256, 256), lambda b, i: (0, 0)),
                  pl.BlockSpec((1, HEAD_DIM), lambda b, i: (0, 0))],
        out_specs=pl.BlockSpec((1, tn, 256), lambda b, i: (b, i, 0)),
        compiler_params=_cparams(("parallel", "parallel")), name="cmp_mlp")(hid, w2bd, g.reshape(1, HEAD_DIM))


def _cmp_attn_body(q_ref, kv_ref, cov_ref, o_ref, imp_ref, *, pos0):
    qi = pl.program_id(2)
    tq = q_ref.shape[1]
    n = kv_ref.shape[1]
    kv = kv_ref[0]
    k = kv[:, :HEAD_DIM]
    v = kv[:, HEAD_DIM:]
    q = q_ref[0]
    pos = pos0 + qi * tq + lax.broadcasted_iota(jnp.int32, (tq, 1), 0)
    end = lax.broadcasted_iota(jnp.int32, (1, n), 1) * CMP_STRIDE + (CMP_BLOCK - 1)
    mask = end <= pos
    psum = jnp.zeros((tq, n), F32)
    outs = []
    for g in range(2):
        s = _mm_nt(q[:, g * HEAD_DIM:(g + 1) * HEAD_DIM], k) * SCALE
        s = jnp.where(mask, s, -jnp.inf)
        m = jnp.max(s, -1, keepdims=True)
        e = jnp.exp(s - jnp.where(m == -jnp.inf, 0.0, m))
        p = e / jnp.maximum(jnp.sum(e, -1, keepdims=True), 1e-30)
        outs.append(_mm(p, v))
        psum = psum + p
    o_ref[0] = jnp.concatenate(outs, 1)
    hi = psum.astype(BF16)
    lo = (psum - hi.astype(F32)).astype(BF16)
    cov = cov_ref[...]
    imp_ref[0, 0] = _mm(hi, cov) + _mm(lo, cov)


def _cmp_attn(qn, kv_cmp, cover, pos0, tq):
    B, T, _ = qn.shape
    n = kv_cmp.shape[1]
    nbp = cover.shape[1]
    return pl.pallas_call(
        functools.partial(_cmp_attn_body, pos0=pos0),
        out_shape=(jax.ShapeDtypeStruct((B, T, GW), F32), jax.ShapeDtypeStruct((B, NSA_KV_HEADS, T, nbp), F32)),
        grid=(B, NSA_KV_HEADS, T // tq),
        in_specs=[pl.BlockSpec((1, tq, LANES), lambda b, h, i: (b, i, h)),
                  pl.BlockSpec((1, n, LANES), lambda b, h, i: (b, 0, h)),
                  pl.BlockSpec((n, nbp), lambda b, h, i: (0, 0))],
        out_specs=(pl.BlockSpec((1, tq, LANES), lambda b, h, i: (b, i, h)),
                   pl.BlockSpec((1, 1, tq, nbp), lambda b, h, i: (b, h, i, 0))),
        compiler_params=_cparams(("parallel", "parallel", "arbitrary")), name="cmp_attn")(qn, kv_cmp, cover)


def _topk_body(imp_ref, sel_ref, idx_ref, cnt_scr, *, pos0, t_rows, n_blk, want_idx):
    for sub in range(imp_ref.shape[0] // LANES):
        rows = slice(sub * LANES, (sub + 1) * LANES)
        _topk_tile(imp_ref.at[rows, :], sel_ref.at[rows, :], idx_ref.at[:, rows], cnt_scr,
                   pl.program_id(0) * (imp_ref.shape[0] // LANES) + sub,
                   pos0=pos0, t_rows=t_rows, n_blk=n_blk, want_idx=want_idx)


def _topk_tile(imp_ref, sel_ref, idx_ref, cnt_scr, ti, *, pos0, t_rows, n_blk, want_idx):
    x = imp_ref[...]
    nbp = x.shape[1]
    nbu = cnt_scr.shape[0]
    xt = jnp.concatenate([x[:, j * LANES:(j + 1) * LANES].T for j in range(nbp // LANES)], 0)[:nbu]
    r = ti * LANES + lax.broadcasted_iota(jnp.int32, (1, LANES), 1)
    cur = (pos0 + r % t_rows) // SLC_BLOCK
    t_last = (ti * LANES) % t_rows + LANES - 1 if t_rows % LANES == 0 else t_rows - 1
    cur_max = (pos0 + t_last) // SLC_BLOCK
    blk = lax.broadcasted_iota(jnp.int32, (nbu, 1), 0)
    forced = (blk == 0) | (blk == cur) | (blk == cur - 1)
    val = jnp.where(forced, jnp.inf, jnp.where(blk <= cur, xt, -jnp.inf))
    val = jnp.where(blk < n_blk, val, -jnp.inf)
    cnt_scr[...] = jnp.zeros(cnt_scr.shape, F32)
    for g0 in range(0, n_blk, 8):
        @pl.when(g0 <= cur_max)
        def _():
            cnt = cnt_scr[...]
            for i in range(g0, min(g0 + 8, n_blk)):
                vi = val[i:i + 1, :]
                ahead = (vi > val) | ((vi == val) & (blk > i))
                cnt = cnt + jnp.where(ahead, 1.0, 0.0)
            cnt_scr[...] = cnt
    cnt = cnt_scr[...]
    chosen = (cnt < float(N_SELECT)) & (val > -jnp.inf)
    self32 = jnp.where(chosen, 1.0, 0.0)
    if nbu < nbp:
        self32 = jnp.concatenate([self32, jnp.zeros((nbp - nbu, LANES), F32)], 0)
    sel_ref[...] = jnp.concatenate([self32[j * LANES:(j + 1) * LANES, :].T for j in range(nbp // LANES)], 1)
    if want_idx:
        blk_f = blk.astype(F32)
        rows = []
        for j in range(N_SELECT):
            hit = chosen & (cnt == float(j))
            rows.append(jnp.sum(jnp.where(hit, blk_f + 1.0, 0.0), 0, keepdims=True) - 1.0)
        idx_ref[...] = jnp.concatenate(rows, 0).astype(jnp.int32)
    else:
        idx_ref[...] = jnp.zeros(idx_ref.shape, jnp.int32)


def _topk(imp2d, pos0, t_rows, n_blk, want_idx):
    R, nbp = imp2d.shape
    tr = 4 * LANES if R % (4 * LANES) == 0 else LANES
    return pl.pallas_call(
        functools.partial(_topk_body, pos0=pos0, t_rows=t_rows, n_blk=n_blk, want_idx=want_idx),
        out_shape=(jax.ShapeDtypeStruct((R, nbp), F32), jax.ShapeDtypeStruct((N_SELECT, R), jnp.int32)),
        scratch_shapes=[pltpu.VMEM((_round_up(n_blk, 8), LANES), F32)],
        grid=(R // tr,),
        in_specs=[pl.BlockSpec((tr, nbp), lambda i: (i, 0))],
        out_specs=(pl.BlockSpec((tr, nbp), lambda i: (i, 0)), pl.BlockSpec((N_SELECT, tr), lambda i: (0, i))),
        compiler_params=_cparams(("parallel",)), name="topk")(imp2d)


def _attn_body(*refs, mode, pairs, tk, pos_q0, pos_k0, scale):
    if mode == 'slc':
        q_ref, kv_ref, sel_ref, blk_ref, o_ref = refs
    elif mode == 'none':
        q_ref, kv_ref, g_ref, ones_ref, o_ref = refs
    else:
        q_ref, kv_ref, o_ref = refs
    qi = pl.program_id(2)
    tq = q_ref.shape[1]
    n_k = kv_ref.shape[1]
    n_tiles = n_k // tk
    q = q_ref[0]
    if mode == 'none':
        ms = _mm(q * q, ones_ref[...], 'l3') * (1.0 / HEAD_DIM)
        q = q * lax.rsqrt(ms + NORM_EPS) * g_ref[...]
    pq0 = pos_q0 + qi * tq
    qpos = pq0 + lax.broadcasted_iota(jnp.int32, (tq, 1), 0)
    unroll = (4 if mode == 'slc' and n_tiles % 4 == 0 else 2) if n_tiles % 2 == 0 else 1
    if mode == 'none':
        lo, hi = 0, n_tiles // unroll
    else:
        r_hi = jnp.minimum(n_k - 1, pq0 + tq - 1 - pos_k0)
        hi = r_hi // (tk * unroll) + 1
        lo = jnp.maximum(0, pq0 - (WINDOW - 1) - pos_k0) // (tk * unroll) if mode == 'win' else 0
    qs = [q[:, q_lo:q_lo + HEAD_DIM] * scale for q_lo, _, _ in pairs]
    if mode == 'slc':
        nbp = sel_ref.shape[3]
        sel_bias = ((sel_ref[0, 0] - 1.0) * (-NEG_BIG)).astype(BF16)
        qs = [jnp.concatenate([sel_bias, qh.astype(BF16)], 1) for qh in qs]

    def step(it, carry, causal=True):
        tiles = []
        for u in range(unroll):
            r0 = pl.multiple_of((it * unroll + u) * tk, tk)
            kvt = kv_ref[0, pl.ds(r0, tk), :]
            kpos = pos_k0 + r0 + lax.broadcasted_iota(jnp.int32, (1, tk), 1)
            onehot = None
            if mode == 'win':
                d = qpos - kpos
                mask = (d >= 0) & (d < WINDOW) & (kpos >= 0)
            elif mode == 'slc':
                onehot = blk_ref[pl.ds(r0, tk), :]
                mask = (kpos <= qpos) if causal else None
            else:
                mask = None
            tiles.append((kvt, mask, onehot))
        new = [None] * (3 * len(pairs))

        def chain(p):
            _, k_lo, v_lo = pairs[p]
            m, l, acc = carry[3 * p:3 * p + 3]
            if mode == 'slc':
                ss = [_mm_nt(qs[p], jnp.concatenate([oh, kvt[:, k_lo:k_lo + HEAD_DIM].astype(BF16)], 1))
                      for kvt, _, oh in tiles]
            else:
                ss = [_mm_nt(qs[p], kvt[:, k_lo:k_lo + HEAD_DIM]) for kvt, _, _ in tiles]
            yield
            ss = [s if mask is None else jnp.where(mask, s, -jnp.inf) for s, (_, mask, _) in zip(ss, tiles)]
            m_new = m
            for s in ss:
                m_new = jnp.maximum(m_new, jnp.max(s, -1, keepdims=True))
            alpha = jnp.exp(m - m_new)
            m_wide = jnp.concatenate([m_new] * (tk // LANES), 1)
            prs = [jnp.exp(s - m_wide) for s in ss]
            yield
            l = alpha * l
            acc = alpha[:, :HEAD_DIM] * acc
            for pr, (kvt, _, _) in zip(prs, tiles):
                l = l + jnp.sum(pr, -1, keepdims=True)
                acc = acc + _mm(pr, kvt[:, v_lo:v_lo + HEAD_DIM])
            new[3 * p:3 * p + 3] = [m_new, l, acc]

        _run_interleaved([chain(p) for p in range(len(pairs))])
        return tuple(new)

    init = []
    for _ in pairs:
        init += [jnp.full((tq, LANES), NEG_BIG, F32), jnp.zeros((tq, LANES), F32), jnp.zeros((tq, HEAD_DIM), F32)]
    if mode == 'slc':
        hi_full = jnp.minimum(hi, (pq0 - pos_k0 + 1) // (tk * unroll))
        res = lax.fori_loop(lo, hi_full, functools.partial(step, causal=False), tuple(init))
        res = lax.fori_loop(hi_full, hi, step, res)
    else:
        res = lax.fori_loop(lo, hi, step, tuple(init))
    outs = []
    for p in range(len(pairs)):
        m, l, acc = res[3 * p:3 * p + 3]
        outs.append(jnp.where(m[:, :HEAD_DIM] > 0.5 * NEG_BIG, acc / jnp.maximum(l[:, :HEAD_DIM], 1e-30), 0.0))
    o_ref[0] = jnp.concatenate(outs, 1)


def _attn(q, kv, mode, pairs, kv_width, pos_q0, pos_k0, tq, tk, sel=None, q_col0=0, q_gain=None, n_rows=None):
    B = q.shape[0]
    T = q.shape[1] if n_rows is None else n_rows
    n_k = kv.shape[1]
    in_specs = [pl.BlockSpec((1, tq, LANES), lambda b, h, i: (b, i, q_col0 + h)),
                pl.BlockSpec((1, n_k, kv_width), lambda b, h, i: (b, 0, h))]
    args = [q, kv]
    if mode == 'none':
        head_of = np.arange(LANES) // HEAD_DIM
        in_specs += [pl.BlockSpec((1, LANES), lambda b, h, i: (0, 0)), pl.BlockSpec((LANES, LANES), lambda b, h, i: (0, 0))]
        args += [q_gain.reshape(1, LANES), jnp.asarray(head_of[:, None] == head_of[None, :], dtype=BF16)]
    if mode == 'slc':
        nbp = sel.shape[-1]
        in_specs.append(pl.BlockSpec((1, 1, tq, nbp), lambda b, h, i: (b, h, i, 0)))
        args.append(sel)
        key_blk = (pos_k0 + np.arange(n_k)) // SLC_BLOCK
        in_specs.append(pl.BlockSpec((n_k, nbp), lambda b, h, i: (0, 0)))
        args.append(jnp.asarray(key_blk[:, None] == np.arange(nbp)[None, :], dtype=BF16))
    return pl.pallas_call(
        functools.partial(_attn_body, mode=mode, pairs=pairs, tk=tk, pos_q0=pos_q0, pos_k0=pos_k0, scale=SCALE),
        out_shape=jax.ShapeDtypeStruct((B, T, GW), F32), grid=(B, 2, T // tq),
        in_specs=in_specs, out_specs=pl.BlockSpec((1, tq, LANES), lambda b, h, i: (b, i, h)),
        compiler_params=_cparams(("parallel", "parallel", "arbitrary")), name="attn_" + mode)(*args)


_GQA_PAIRS = ((0, 0, HEAD_DIM), (HEAD_DIM, 0, HEAD_DIM))
_MHA_PAIRS = ((0, 0, HEAD_DIM), (HEAD_DIM, 2 * HEAD_DIM, 3 * HEAD_DIM))


def _slc_paged_body(idx_ref, phys_ref, *refs, pos0, blk0, t_real):
    del phys_ref
    n_slots = t_real * N_SELECT
    q_ref = refs[0]
    blk_refs = refs[1:1 + n_slots]
    new_ref = refs[1 + n_slots]
    o_ref = refs[2 + n_slots]
    b, h = pl.program_id(0), pl.program_id(1)
    bpp = PAGE_SIZE // SLC_BLOCK
    tok = lax.broadcasted_iota(jnp.int32, (1, PAGE_SIZE), 1)
    lane = lax.broadcasted_iota(jnp.int32, (1, SLC_BLOCK), 1)
    newblk = new_ref[0]

    def chain(t):
        base = ((b * NSA_KV_HEADS + h) * t_real + t) * N_SELECT
        qrow = q_ref[0, t:t + 1, :]
        q2 = jnp.concatenate([qrow[:, :HEAD_DIM], qrow[:, HEAD_DIM:], jnp.zeros((6, HEAD_DIM), F32)], 0) * SCALE
        pos = pos0 + t
        scores, vts = [], []
        n_new = jnp.int32(0)
        for j in range(N_SELECT):
            idx = idx_ref[base + j]
            idc = jnp.maximum(idx, 0)
            kv_t = blk_refs[t * N_SELECT + j][0, 0, 0]
            ok = ((idx >= 0) & (idx < blk0) & (tok // SLC_BLOCK == idc % bpp)
                  & ((idc // bpp) * PAGE_SIZE + tok <= pos))
            scores.append(_mm(q2, kv_t[0]) + jnp.where(ok, 0.0, -jnp.inf))
            vts.append(kv_t[1])
            n_new = n_new + jnp.where(idx >= blk0, 1, 0)
        ok_new = (n_new > 0) & (blk0 * SLC_BLOCK + lane <= pos)
        s_new = _mm_nt(q2, newblk[:, :HEAD_DIM]) + jnp.where(ok_new, 0.0, -jnp.inf)
        yield
        m = jnp.max(s_new, -1, keepdims=True)
        for s in scores:
            m = jnp.maximum(m, jnp.max(s, -1, keepdims=True))
        m = jnp.where(m == -jnp.inf, 0.0, m)
        e_new = jnp.exp(s_new - m)
        den = jnp.sum(e_new, -1, keepdims=True)
        es = [jnp.exp(s - m) for s in scores]
        yield
        o = _mm(e_new, newblk[:, HEAD_DIM:])
        for e, vt in zip(es, vts):
            den = den + jnp.sum(e, -1, keepdims=True)
            o = o + _mm_nt(e, vt)
        o = o / jnp.maximum(den, 1e-30)
        orow = jnp.concatenate([o[0:1], o[1:2]], 1)
        o_ref[0, 0, t] = jnp.broadcast_to(orow, (8, LANES))

    _run_interleaved([chain(t) for t in range(t_real)])


def _slc_paged(q_rot, cache_t, new_rows, idx_flat, page_flat, layer, pos0, blk0, t_real):
    B = q_rot.shape[0]
    tp = q_rot.shape[1]
    n_slots = t_real * N_SELECT

    def blk_spec(s):
        def imap(b, h, idx, page):
            return (layer, page[(b * NSA_KV_HEADS + h) * n_slots + s], h, 0, 0, 0)
        return pl.BlockSpec((1, 1, 1, 2, HEAD_DIM, PAGE_SIZE), imap)

    gs = pltpu.PrefetchScalarGridSpec(
        num_scalar_prefetch=2, grid=(B, NSA_KV_HEADS),
        in_specs=[pl.BlockSpec((1, tp, LANES), lambda b, h, idx, page: (b, 0, h))]
        + [blk_spec(s) for s in range(n_slots)]
        + [pl.BlockSpec((1, SLC_BLOCK, LANES), lambda b, h, idx, page: (b, 0, h))],
        out_specs=pl.BlockSpec((1, 1, t_real, 8, LANES), lambda b, h, idx, page: (b, h, 0, 0, 0)))
    out = pl.pallas_call(
        functools.partial(_slc_paged_body, pos0=pos0, blk0=blk0, t_real=t_real),
        out_shape=jax.ShapeDtypeStruct((B, NSA_KV_HEADS, t_real, 8, LANES), F32), grid_spec=gs,
        compiler_params=_cparams(("parallel", "arbitrary")), name="slc_paged")(
            idx_flat, page_flat, q_rot, *([cache_t] * n_slots), new_rows)
    return jnp.transpose(out[:, :, :, 0, :], (0, 2, 1, 3)).reshape(B, t_real, GW)


def _nsa_prep_body(q_ref, s_ref, w_ref, cos_ref, sin_ref, g_ref, ones_ref, qn_ref, qr_ref, so_ref, wo_ref):
    lane = lax.broadcasted_iota(jnp.int32, (1, GW), 1)
    first_half = (lane % HEAD_DIM) < (HEAD_DIM // 2)
    is_k = (lane // HEAD_DIM) % 2 == 0
    cos, sin = cos_ref[...], sin_ref[...]
    ones = ones_ref[...]

    def norm(x, g):
        ms = _mm(x * x, ones, 'l3') * (1.0 / HEAD_DIM)
        return x * lax.rsqrt(ms + NORM_EPS) * g

    def rope(x):
        swapped = jnp.where(first_half, pltpu.roll(x, GW - HEAD_DIM // 2, 1), pltpu.roll(x, HEAD_DIM // 2, 1))
        return x * cos + swapped * sin

    qn = norm(q_ref[0], g_ref[0:1, :])
    qn_ref[0] = qn
    qr_ref[0] = rope(qn)
    for x_ref, o_ref, gi in ((s_ref, so_ref, 1), (w_ref, wo_ref, 2)):
        x = x_ref[0]
        o_ref[0] = jnp.where(is_k, rope(norm(x, g_ref[gi:gi + 1, :])), x)


def _nsa_prep(P3, n_rows, tr, cos, sin, g_blk, ones_bd):
    B = P3.shape[0]
    col = lambda c0: pl.BlockSpec((1, tr, GW), lambda b, i: (b, i, c0 // GW))
    tab = pl.BlockSpec((tr, GW), lambda b, i: (i, 0))
    out = pl.BlockSpec((1, tr, GW), lambda b, i: (b, i, 0))
    shp = jax.ShapeDtypeStruct((B, n_rows, GW), F32)
    return pl.pallas_call(
        _nsa_prep_body, out_shape=(shp, shp, shp, shp), grid=(B, n_rows // tr),
        in_specs=[col(C_NQ), col(C_NSLC), col(C_NWIN), tab, tab, pl.BlockSpec((8, GW), lambda b, i: (0, 0)),
                  pl.BlockSpec((GW, GW), lambda b, i: (0, 0))],
        out_specs=(out, out, out, out),
        compiler_params=_cparams(("parallel", "parallel")), name="nsa_prep")(P3, P3, P3, cos, sin, g_blk, ones_bd)


def _rms(x, g):
    return x * lax.rsqrt(jnp.mean(x * x, -1, keepdims=True) + NORM_EPS) * g


def _rope(x, pos):
    half = HEAD_DIM // 2
    inv = ROPE_THETA ** (-jnp.arange(half, dtype=F32) / half)
    ang = pos.astype(F32)[:, None] * inv
    cos, sin = jnp.cos(ang)[:, None, :], jnp.sin(ang)[:, None, :]
    x1, x2 = x[..., :half], x[..., half:]
    return jnp.concatenate([x1 * cos - x2 * sin, x1 * sin + x2 * cos], -1)


def _pad_t(x, tp, value=0.0):
    t = x.shape[1]
    if t == tp:
        return x
    return jnp.pad(x, ((0, 0), (0, tp - t)) + ((0, 0),) * (x.ndim - 2), constant_values=value)


def _round_up(n, m):
    return -(-n // m) * m


def _cover_matrix(n_cmp, n_cmp_pad, n_slc, nbp):
    start = np.arange(n_cmp_pad)[:, None] * CMP_STRIDE
    blk = np.arange(nbp)[None, :]
    cov = (start < (blk + 1) * SLC_BLOCK) & (start + CMP_BLOCK > blk * SLC_BLOCK)
    cov &= (np.arange(n_cmp_pad)[:, None] < n_cmp) & (blk < n_slc)
    return jnp.asarray(cov.astype(np.float32), dtype=BF16)


def _mlstm_mixer(P, Pp, conv_buf, c0, n0, m0, prm):
    B, T, _ = P.shape
    keep = MLSTM_CONV - 1
    tail = jnp.pad(conv_buf, ((0, 0), (8 - keep, 0), (0, 0)))
    m0p = jnp.pad(m0, ((0, 0), (0, LANES - N_HEADS))).reshape(B, 1, LANES)
    out, C, n, m = _mlstm(Pp, T, tail[..., :GW], tail[..., GW:], prm['mlstm_blk'], c0, n0, m0p)
    qk_raw = jnp.concatenate([P[:, -keep:, C_MQ:C_MQ + GW], P[:, -keep:, C_MK:C_MK + GW]], -1)
    conv_new = jnp.concatenate([conv_buf, qk_raw], 1)[:, -keep:]
    return out[:, :T], C, n, m[:, 0, :N_HEADS], conv_new


def _rwkv_mixer(P, Pp, shift_buf, s0, prm, prec):
    B, T, _ = P.shape
    out, S = _rwkv(Pp, T, shift_buf, prm['rwkv_blk'], prm['rwkv_lora'], s0, prec)
    shift_new = jnp.concatenate([P[:, -1:, C_RR:C_RR + 3 * GW], P[:, -1:, C_RL:C_RL + LANES]], -1)
    return out[:, :T], S, shift_new


def _memory_kv(mem, prm):
    B = mem.shape[0]
    kv = _proj_in(mem.reshape(B * N_MEM, D_MODEL), prm['mem_norm_g'], prm['w_mem_kv_bf16'], 2 * GW)
    kv = kv.reshape(B, N_MEM, MEM_HEADS, 2, HEAD_DIM)
    return jnp.stack([_rms(kv[:, :, :, 0], prm['mem_qk_g'][1]), kv[:, :, :, 1]], 3)


def _memory_mixer(Pp, T, mem_kv, prm):
    B = Pp.shape[0]
    tp = _round_up(T, 8)
    tq = min(512, tp)
    o = _attn(Pp, mem_kv.reshape(B, N_MEM, 2 * GW), 'none', _MHA_PAIRS, 2 * LANES, 0, 0, tq, N_MEM,
              q_col0=C_CQ // LANES, q_gain=jnp.tile(prm['mem_qk_g'][0], 2), n_rows=tp)
    return o[:, :T]


def _rope_tables(pos):
    half = HEAD_DIM // 2
    inv = ROPE_THETA ** (-jnp.arange(half, dtype=F32) / half)
    ang = pos.astype(F32)[:, None] * inv
    cos, sin = jnp.cos(ang), jnp.sin(ang)
    return (jnp.tile(jnp.concatenate([cos, cos], -1), (1, N_HEADS)),
            jnp.tile(jnp.concatenate([-sin, sin], -1), (1, N_HEADS)))


def _nsa_mixer(P2d, Pp, B, T, pos0, win_prefix, prm, past):
    P = P2d.reshape(B, T, DP)
    g = prm['nsa_qk_g']
    tp = _round_up(T, 8)
    tq = min(256, tp)
    cos, sin = _rope_tables(pos0 + jnp.arange(tp, dtype=jnp.int32))
    qn, q_rot, slc_rows, win_rows = _nsa_prep(Pp, tp, tq, cos, sin, prm['nsa_prep_blk'], prm['head_ones'])
    kvrows = lambda t: t[:, :T].reshape(B, T, NSA_KV_HEADS, 2, HEAD_DIM)
    cmp_new = kvrows(P[..., C_NCMP:C_NCMP + 2 * KVW])
    slc_new, win_new = kvrows(slc_rows), kvrows(win_rows)

    L_all = pos0 + T
    n_sub = max(-(-L_all // CMP_STRIDE), CMP_BLOCK // CMP_STRIDE)
    n_cmp = n_sub - 1
    bd, w2bd, pe_hid = prm['cmp_bd'], prm['cmp_w2bd'], prm['cmp_pe_hid']
    if past is None:
        G = _subproj(P2d, C_NCMP // LANES, bd).reshape(B, T // CMP_STRIDE, 4 * LANES)
    else:
        cache_cmp_t, cache_slc_t, page_table, layer = past
        g_pages = _subproj_pages(cache_cmp_t, page_table, layer, bd)
        new_rows = _pad_t(cmp_new.reshape(B, T, 2 * KVW), CMP_STRIDE).reshape(B * CMP_STRIDE, 2 * KVW)
        g_new = _subproj(new_rows, 0, bd).reshape(B, 1, 4 * LANES)
        G = jnp.concatenate([g_pages, g_new], 1)
    n_cmp_pad = _round_up(n_cmp, LANES)
    gb = G[:, 1:, 256:]
    ga = G[:, :, :256]
    fit = lambda t: _pad_t(t, max(n_cmp_pad, t.shape[1]))[:, :n_cmp_pad]
    hid = fit(ga) + fit(gb) + pe_hid
    kv_cmp = _cmp_mlp(hid, w2bd, g[1])
    n_slc = -(-L_all // SLC_BLOCK)
    nbp = _round_up(n_slc, LANES)
    cover = _cover_matrix(n_cmp, n_cmp_pad, n_slc, nbp)
    tq = min(256, tp)
    o_cmp, imp = _cmp_attn(qn, kv_cmp, cover, pos0, tq)

    R = B * NSA_KV_HEADS * tp
    rp = _round_up(R, LANES)
    imp2d = jnp.pad(imp.reshape(R, nbp), ((0, rp - R), (0, 0)))
    sel, idx_t = _topk(imp2d, pos0, tp, n_slc, want_idx=past is not None)

    if past is None:
        sel4 = sel[:R].reshape(B, NSA_KV_HEADS, tp, nbp)
        o_slc = _attn(q_rot, slc_new.reshape(B, T, 2 * KVW), 'slc', _GQA_PAIRS, LANES, pos0, pos0, tq,
                      min(256, T), sel=sel4)
    else:
        bpp = PAGE_SIZE // SLC_BLOCK
        idx = idx_t[:, :R].T.reshape(B, NSA_KV_HEADS, tp, N_SELECT)[:, :, :T]
        idc = jnp.clip(idx, 0, page_table.shape[1] * bpp - 1)
        page = page_table[jnp.arange(B)[:, None, None, None], idc // bpp]
        new_rows = _pad_t(slc_new.reshape(B, T, 2 * KVW), SLC_BLOCK)
        o_slc = _slc_paged(q_rot, cache_slc_t, new_rows, idx.reshape(-1), page.reshape(-1).astype(jnp.int32),
                           layer, pos0, pos0 // SLC_BLOCK, T)

    if win_prefix.shape[1] == 0:
        win_ctx = win_new
        pos_k0 = pos0
    else:
        win_ctx = jnp.concatenate([win_prefix, win_new], 1)
        pos_k0 = pos0 - win_prefix.shape[1]
    n_k = win_ctx.shape[1]
    tkw = min(256, _round_up(n_k, LANES))
    kv_win = _pad_t(win_ctx.reshape(B, n_k, 2 * KVW), _round_up(n_k, tkw))
    o_win = _attn(q_rot, kv_win, 'win', _GQA_PAIRS, LANES, pos0, pos_k0, tq, tkw)

    keep = win_prefix.shape[1] if past is not None else min(WINDOW, T)
    flat = lambda o: o[:, :T].reshape(B * T, GW)
    return (flat(o_cmp), flat(o_slc), flat(o_win)), cmp_new, slc_new, win_ctx[:, -keep:]


RWKV_PREC = ('r3', 'bf16', 'bf16')


def _layer(x, pos0, st, mem_kv, prm, past, rwkv_prec=RWKV_PREC):
    conv_buf, c0, n0, m0, s0, shift_buf, win_prefix = st
    B, T, _ = x.shape
    x2d = x.reshape(B * T, D_MODEL)
    P2d = _proj_in(x2d, prm['norm_g'], prm['w_in_bf16'], 640)
    P = P2d.reshape(B, T, DP)
    Pp = _pad_t(P, _round_up(T, max(MLSTM_L, RWKV_L)))
    y_m, C, n, m, conv_new = _mlstm_mixer(P, Pp, conv_buf, c0, n0, m0, prm)
    (o_cmp, o_slc, o_win), cmp_new, slc_new, win_new = _nsa_mixer(P2d, Pp, B, T, pos0, win_prefix, prm, past)
    y_r, S, shift_new = _rwkv_mixer(P, Pp, shift_buf, s0, prm, rwkv_prec)
    o_mem = _memory_mixer(Pp, T, mem_kv, prm)
    flat = lambda t: t.reshape(B * T, GW)
    out = _proj_out(x2d, P2d, flat(y_m), flat(y_r), o_cmp, o_slc, o_win, flat(o_mem), prm['gate_expand'],
                    prm['w_out_bf16']).reshape(B, T, D_MODEL)
    return out, (cmp_new, slc_new, win_new, C, n, m, conv_new, S, shift_new)


def _prep_params(l, p):
    prm = {k: v[l] for k, v in p.items()}
    src = jnp.asarray(np.maximum(_SRC, 0), jnp.int32)
    keep = jnp.asarray((_SRC >= 0).astype(np.float32))
    prm['w_in_bf16'] = (jnp.take(prm['w_in'], src, axis=1) * keep).astype(BF16)
    prm['w_out_bf16'] = prm['w_out'].astype(BF16)
    wm = prm['w_mem_kv'].reshape(D_MODEL, 2, MEM_HEADS, HEAD_DIM)
    prm['w_mem_kv_bf16'] = jnp.transpose(wm, (0, 2, 1, 3)).reshape(D_MODEL, 2 * GW).astype(BF16)
    rows = lambda *vs: jnp.concatenate([jnp.pad(v.reshape(-1, v.shape[-1]), ((0, 0), (0, GW - v.shape[-1])))
                                        for v in vs], 0)
    pad16 = lambda blk: jnp.pad(blk, ((0, 16 - blk.shape[0]), (0, 0)))
    cw, cb = prm['mlstm_conv_w'], prm['mlstm_conv_b']
    prm['mlstm_blk'] = pad16(rows(cw[:, :GW], cw[:, GW:], cb[:GW], cb[GW:], prm['mlstm_norm_g'],
                                  prm['mlstm_gate_b'].reshape(1, 2 * N_HEADS)))
    mu = prm['rwkv_mu']
    prm['rwkv_blk'] = pad16(rows(mu[:GW], mu[GW:2 * GW], mu[2 * GW:3 * GW], mu[3 * GW:], prm['rwkv_w0'],
                                 prm['rwkv_a0'], prm['rwkv_kk'], prm['rwkv_ln']))
    g = prm['nsa_qk_g']
    one = jnp.ones((HEAD_DIM,), F32)
    prm['nsa_prep_blk'] = jnp.pad(rows(jnp.tile(g[0], N_HEADS), jnp.concatenate([g[2], one, g[2], one]),
                                       jnp.concatenate([g[3], one, g[3], one])), ((0, 5), (0, 0)))
    head_of = np.arange(GW) // HEAD_DIM
    prm['head_ones'] = jnp.asarray(head_of[:, None] == head_of[None, :], dtype=BF16)
    gate_lane = np.arange(LANES)[:, None]
    out_lane = np.arange(3 * GW)[None, :]
    prm['gate_expand'] = jnp.asarray(gate_lane == (out_lane // GW) * N_HEADS + (out_lane % GW) // HEAD_DIM, dtype=BF16)
    zl = jnp.zeros((DECAY_LORA, GW), F32)
    prm['rwkv_lora'] = jnp.concatenate([jnp.concatenate([prm['rwkv_w2'], zl], 1),
                                        jnp.concatenate([zl, prm['rwkv_a2']], 1)], 0)
    w1 = prm['nsa_cmp_w1']
    eye_h = jnp.eye(NSA_KV_HEADS, dtype=F32)
    eye_c = jnp.eye(2, dtype=F32)
    w1r = w1.reshape(2, 2, CMP_STRIDE, HEAD_DIM, HEAD_DIM)
    bd = jnp.einsum('cC,crsde->scdrCe', eye_c, w1r)
    prm['cmp_bd'] = bd.reshape(CMP_STRIDE // 2, 2 * LANES, 2 * LANES).astype(BF16)
    w2 = prm['nsa_cmp_w2']
    prm['cmp_w2bd'] = jnp.einsum('hH,cC,ced->hceHCd', eye_h, eye_c, w2).reshape(2 * KVW, 2 * KVW)
    pe_hid = jnp.einsum('csd,csde->ce', prm['nsa_pe'], w1, precision=HI)
    prm['cmp_pe_hid'] = jnp.tile(pe_hid.reshape(1, 2 * HEAD_DIM), (1, NSA_KV_HEADS)).reshape(2 * KVW)
    return prm


def kernel(x_prompt, x_sample, cache_cmp_kv, cache_slc_kv, cache_win_kv, cache_mem_kv, state_mlstm_C, state_mlstm_n, state_mlstm_m, state_mlstm_conv, state_rwkv_S, state_rwkv_shift, page_table, mem_prompt, norm_g, w_in, w_out, mlstm_conv_w, mlstm_conv_b, mlstm_gate_b, mlstm_norm_g, nsa_qk_g, nsa_pe, nsa_cmp_w1, nsa_cmp_w2, rwkv_mu, rwkv_w0, rwkv_w2, rwkv_a0, rwkv_a2, rwkv_kk, rwkv_ln, mem_norm_g, w_mem_kv, mem_qk_g):
    params = dict(norm_g=norm_g, w_in=w_in, w_out=w_out, mlstm_conv_w=mlstm_conv_w, mlstm_conv_b=mlstm_conv_b,
                  mlstm_gate_b=mlstm_gate_b, mlstm_norm_g=mlstm_norm_g, nsa_qk_g=nsa_qk_g, nsa_pe=nsa_pe,
                  nsa_cmp_w1=nsa_cmp_w1, nsa_cmp_w2=nsa_cmp_w2, rwkv_mu=rwkv_mu, rwkv_w0=rwkv_w0, rwkv_w2=rwkv_w2,
                  rwkv_a0=rwkv_a0, rwkv_a2=rwkv_a2, rwkv_kk=rwkv_kk, rwkv_ln=rwkv_ln, mem_norm_g=mem_norm_g,
                  w_mem_kv=w_mem_kv, mem_qk_g=mem_qk_g)
    depth = norm_g.shape[0]
    B = x_prompt.shape[0]
    past_len = page_table.shape[1] * PAGE_SIZE
    cache_cmp_t = jnp.transpose(cache_cmp_kv, (0, 1, 3, 4, 5, 2))
    cache_slc_t = jnp.transpose(cache_slc_kv, (0, 1, 3, 4, 5, 2))
    xp, xs = x_prompt, x_sample
    new_p, new_s, new_mem = [], [], []
    for l in range(depth):
        prm = _prep_params(l, params)
        mem_kv_p = _memory_kv(mem_prompt, prm)
        st_p = (jnp.zeros((B, MLSTM_CONV - 1, 2 * GW), F32),
                jnp.zeros((B, N_HEADS, HEAD_DIM, HEAD_DIM), F32),
                jnp.zeros((B, N_HEADS, HEAD_DIM), F32),
                jnp.full((B, N_HEADS), M_INIT, F32),
                jnp.zeros((B, N_HEADS, HEAD_DIM, HEAD_DIM), F32),
                jnp.zeros((B, 1, RWKV_SHIFT), F32),
                jnp.zeros((B, 0, NSA_KV_HEADS, 2, HEAD_DIM), F32))
        xp, sp = _layer(xp, 0, st_p, mem_kv_p, prm, None)
        st_s = (state_mlstm_conv[l], state_mlstm_C[l], state_mlstm_n[l], state_mlstm_m[l],
                state_rwkv_S[l], state_rwkv_shift[l], cache_win_kv[l])
        xs, ss = _layer(xs, past_len, st_s, cache_mem_kv[l], prm, (cache_cmp_t, cache_slc_t, page_table, l))
        new_p.append(sp)
        new_s.append(ss)
        new_mem.append(mem_kv_p)
    stack = lambda states, i: jnp.stack([s[i] for s in states])
    outs = [xp, xs]
    for i in range(3):
        outs += [stack(new_p, i), stack(new_s, i)]
    outs.append(jnp.stack(new_mem))
    for i in range(3, 9):
        outs += [stack(new_p, i), stack(new_s, i)]
    return tuple(outs)
```

```python
import functools

import numpy as np
import jax
import jax.numpy as jnp
from jax import lax
from jax.experimental import pallas as pl
from jax.experimental.pallas import tpu as pltpu

F32 = jnp.float32
BF16 = jnp.bfloat16
HI = lax.Precision.HIGHEST

D_MODEL = 1024
PAGE_SIZE = 128
HEAD_DIM = 64
GW = D_MODEL // 4
N_HEADS = GW // HEAD_DIM
SCALE = HEAD_DIM ** -0.5
MLSTM_CONV = 4
M_INIT = -1e30
NSA_KV_HEADS = 2
KVW = NSA_KV_HEADS * HEAD_DIM
CMP_BLOCK = 32
CMP_STRIDE = 16
SLC_BLOCK = 64
N_SELECT = 16
WINDOW = 512
DECAY_LORA = 64
AAA_LORA = 64
RWKV_SHIFT = 3 * GW + DECAY_LORA + AAA_LORA
RWKV_LN_EPS = HEAD_DIM * 1e-5
N_MEM = 256
MEM_HEADS = 4
ROPE_THETA = 10000.0
NORM_EPS = 1e-6

LANES = 128
MLSTM_L = 128
RWKV_L = 64
NEG_BIG = -1e30

_M0, _N0, _R0, _C0 = 0, 1288, 2580, 3732
C_MQ, C_MK, C_MV, C_MO, C_MZ = 0, 256, 512, 768, 1024
C_NQ, C_NCMP, C_NSLC, C_NWIN, C_NZ = 1280, 1536, 1792, 2048, 2304
C_RR, C_RK, C_RV, C_RZ = 2560, 2816, 3072, 3328
C_CQ, C_CZ = 3584, 3840
C_RL = 4096
C_MG = 4224
C_NG = 4352
DP = 4480


def _packed_src():
    src = -np.ones((DP,), np.int64)

    def put(dst, lo, n):
        src[dst:dst + n] = np.arange(lo, lo + n)

    put(C_MQ, _M0, 256); put(C_MK, _M0 + 256, 256); put(C_MV, _M0 + 512, 256)
    put(C_MG, _M0 + 768, 8); put(C_MO, _M0 + 776, 256); put(C_MZ, _M0 + 1032, 256)
    put(C_NQ, _N0, 256)
    for i, base in enumerate((C_NCMP, C_NSLC, C_NWIN)):
        ksrc = _N0 + 256 + 256 * i
        vsrc = ksrc + 128
        for h in range(2):
            put(base + 128 * h, ksrc + 64 * h, 64)
            put(base + 128 * h + 64, vsrc + 64 * h, 64)
    put(C_NG, _N0 + 1024, 12); put(C_NZ, _N0 + 1036, 256)
    put(C_RR, _R0, 256); put(C_RK, _R0 + 256, 256); put(C_RV, _R0 + 512, 256)
    put(C_RL, _R0 + 768, 128); put(C_RZ, _R0 + 896, 256)
    put(C_CQ, _C0, 256); put(C_CZ, _C0 + 256, 256)
    return src


_SRC = _packed_src()


def _split2(a):
    hi = a.astype(BF16)
    return hi, (a - hi.astype(F32)).astype(BF16)


def _dg(a, b, dims, prec):
    dn = (dims, ((), ()))
    if prec == 'bf16':
        return lax.dot_general(a.astype(BF16), b.astype(BF16), dn, preferred_element_type=F32)
    d = lambda x, y: lax.dot_general(x, y, dn, preferred_element_type=F32)
    if prec == 'x3':
        ah, al = _split2(a)
        bh, bl = _split2(b)
        return d(ah, bh) + (d(ah, bl) + d(al, bh))
    if prec in ('r3', 'l3'):
        exact, other = (a, b) if prec == 'r3' else (b, a)
        o1, rest = other.astype(BF16), None
        rest = other - o1.astype(F32)
        o2 = rest.astype(BF16)
        o3 = (rest - o2.astype(F32)).astype(BF16)
        e = exact.astype(BF16)
        if prec == 'r3':
            return d(e, o1) + (d(e, o2) + d(e, o3))
        return d(o1, e) + (d(o2, e) + d(o3, e))
    return lax.dot_general(a, b, dn, preferred_element_type=F32, precision=prec)


def _mm(a, b, prec=None):
    return _dg(a, b, ((1,), (0,)), prec)


def _mm_nt(a, b, prec=None):
    return _dg(a, b, ((1,), (1,)), prec)


def _mm_tn(a, b, prec=None):
    return _dg(a, b, ((0,), (0,)), prec)


def _run_interleaved(chains):
    chains = list(chains)
    while chains:
        alive = []
        for ch in chains:
            try:
                next(ch)
                alive.append(ch)
            except StopIteration:
                pass
        chains = alive


def _cparams(sem, vmem_mb=None):
    kw = dict(dimension_semantics=sem)
    if vmem_mb is not None:
        kw['vmem_limit_bytes'] = vmem_mb * 1024 * 1024
    return pltpu.CompilerParams(**kw)


def _proj_in_body(x_ref, g_ref, w_ref, o_ref, *, tn):
    x = x_ref[...]
    h = (x * lax.rsqrt(jnp.mean(x * x, -1, keepdims=True) + NORM_EPS) * g_ref[...]).astype(BF16)
    for j in range(o_ref.shape[1] // tn):
        o_ref[:, j * tn:(j + 1) * tn] = jnp.dot(h, w_ref[:, j * tn:(j + 1) * tn], preferred_element_type=F32)


def _proj_in(x2d, g, w_bf16, tn):
    n, d = x2d.shape
    dn = w_bf16.shape[1]
    tm = min(512, n)
    return pl.pallas_call(
        functools.partial(_proj_in_body, tn=tn), out_shape=jax.ShapeDtypeStruct((n, dn), F32), grid=(n // tm,),
        in_specs=[pl.BlockSpec((tm, d), lambda i: (i, 0)),
                  pl.BlockSpec((1, d), lambda i: (0, 0)),
                  pl.BlockSpec((d, dn), lambda i: (0, 0))],
        out_specs=pl.BlockSpec((tm, dn), lambda i: (i, 0)),
        compiler_params=_cparams(("parallel",), 56), name="proj_in")(x2d, g.reshape(1, d), w_bf16)


def _silu(x):
    return x * jax.nn.sigmoid(x)


def _proj_out_body(x_ref, ym_ref, yr_ref, oc_ref, os_ref, ow_ref, om_ref, g_ref, zn_ref, zc_ref, e_ref, w_ref, o_ref):
    ge = _mm(jax.nn.sigmoid(g_ref[...]), e_ref[...], 'l3')
    y_n = (ge[:, :GW] * oc_ref[...] + ge[:, GW:2 * GW] * os_ref[...] + ge[:, 2 * GW:] * ow_ref[...]) * _silu(zn_ref[...])
    y_c = om_ref[...] * _silu(zc_ref[...])
    acc = x_ref[...]
    for i, y in enumerate((ym_ref[...], y_n, yr_ref[...], y_c)):
        acc = acc + jnp.dot(y.astype(BF16), w_ref[i * GW:(i + 1) * GW, :], preferred_element_type=F32)
    o_ref[...] = acc


def _proj_out(x2d, P2d, y_m, y_r, o_cmp, o_slc, o_win, o_mem, gate_expand, w_bf16):
    n, d = x2d.shape
    tm = min(512, n)
    yspec = pl.BlockSpec((tm, GW), lambda i: (i, 0))
    return pl.pallas_call(
        _proj_out_body, out_shape=jax.ShapeDtypeStruct((n, d), F32), grid=(n // tm,),
        in_specs=[pl.BlockSpec((tm, d), lambda i: (i, 0)), yspec, yspec, yspec, yspec, yspec, yspec,
                  pl.BlockSpec((tm, LANES), lambda i: (i, C_NG // LANES)),
                  pl.BlockSpec((tm, GW), lambda i: (i, C_NZ // GW)), pl.BlockSpec((tm, GW), lambda i: (i, C_CZ // GW)),
                  pl.BlockSpec((LANES, 3 * GW), lambda i: (0, 0)), pl.BlockSpec((d, d), lambda i: (0, 0))],
        out_specs=pl.BlockSpec((tm, d), lambda i: (i, 0)),
        compiler_params=_cparams(("parallel",), 48), name="proj_out")(
            x2d, y_m, y_r, o_cmp, o_slc, o_win, o_mem, P2d, P2d, P2d, gate_expand, w_bf16)


_MP_CONV_Q, _MP_CONV_K, _MP_BIAS_Q, _MP_BIAS_K, _MP_NORM, _MP_GATE_B = 0, 4, 8, 9, 10, 11


def _mlstm_body(q_ref, k_ref, v_ref, o_ref, z_ref, g_ref, tq_ref, tk_ref, prm_ref, c0_ref, n0_ref, m0_ref,
                h_ref, c_ref, n_ref, m_ref, c_scr, n_scr, m_scr, pq_scr, pk_scr, *, t_valid):
    c = pl.program_id(1)
    nb, L = q_ref.shape[0], q_ref.shape[1]

    @pl.when(c == 0)
    def _():
        c_scr[...] = c0_ref[...]
        n_scr[...] = n0_ref[...]
        m_scr[...] = m0_ref[...]
        pq_scr[...] = jnp.zeros(pq_scr.shape, F32)
        pk_scr[...] = jnp.zeros(pk_scr.shape, F32)
        pq_scr[:, L - 8:L, :] = tq_ref[...]
        pk_scr[:, L - 8:L, :] = tk_ref[...]

    row = lax.broadcasted_iota(jnp.int32, (L, L), 0)
    col = lax.broadcasted_iota(jnp.int32, (L, L), 1)
    causal = row >= col
    tril = causal.astype(F32)
    triu = (row <= col).astype(F32)
    trow = lax.broadcasted_iota(jnp.int32, (L, 1), 0)
    valid = (c * L + trow) < t_valid
    lane = lax.broadcasted_iota(jnp.int32, (1, LANES), 1)
    prm = prm_ref[...]

    def conv(x, prev, w0, b):
        acc = prm[b:b + 1, :] + prm[w0 + MLSTM_CONV - 1:w0 + MLSTM_CONV, :] * x
        for s in range(1, MLSTM_CONV):
            shifted = jnp.where(trow >= s, pltpu.roll(x, s, 0), pltpu.roll(prev, s, 0))
            acc = acc + prm[w0 + MLSTM_CONV - 1 - s:w0 + MLSTM_CONV - s, :] * shifted
        return _silu(acc)

    cums, qs, ks = [], [], []
    for gi in range(nb):
        q_raw, k_raw = q_ref[gi], k_ref[gi]
        qs.append(conv(q_raw, pq_scr[gi], _MP_CONV_Q, _MP_BIAS_Q) * (HEAD_DIM ** -0.5))
        ks.append(conv(k_raw, pk_scr[gi], _MP_CONV_K, _MP_BIAS_K))
        pq_scr[gi] = q_raw
        pk_scr[gi] = k_raw
        x = g_ref[gi] + prm[_MP_GATE_B:_MP_GATE_B + 1, :LANES]
        log_f = jnp.minimum(x, 0.0) - jnp.log1p(jnp.exp(-jnp.abs(x)))
        g = jnp.where(lane < N_HEADS, x, jnp.where(lane < 2 * N_HEADS, log_f, 0.0))
        g = jnp.where(valid, g, jnp.where(lane < N_HEADS, NEG_BIG, 0.0))
        gt = g.T
        cums.append((g, gt, _mm(tril, g, 'r3'), _mm(gt, triu, 'l3')))

    def chain(gi, h):
        g, gt, bc, br = cums[gi]
        sl = slice(h * HEAD_DIM, (h + 1) * HEAD_DIM)
        qh = qs[gi][:, sl]
        kh = ks[gi][:, sl]
        vh = v_ref[gi, :, sl]
        qk = _mm_nt(qh, kh)
        ch = c_scr[gi, h]
        qc = _mm(qh, ch)
        yield
        b_col = bc[:, 4 + h:5 + h]
        li_col = g[:, h:h + 1]
        b_row = br[4 + h:5 + h, :]
        li_row = gt[h:h + 1, :]
        m_prev = m_scr[gi, :, h:h + 1]
        log_d = jnp.where(causal, b_col - b_row + li_row, -jnp.inf)
        log_inter = b_col + m_prev
        m_t = jnp.maximum(jnp.max(log_d, -1, keepdims=True), log_inter)
        s = qk * jnp.exp(log_d - m_t)
        w_inter = jnp.exp(log_inter - m_t)
        nh = n_scr[gi, h:h + 1, :]
        sv = _mm(s, vh)
        yield
        num = sv + w_inter * qc
        den = jnp.sum(s, -1, keepdims=True) + w_inter * jnp.sum(qh * nh, -1, keepdims=True)
        hh = num / jnp.maximum(jnp.abs(den), jnp.exp(-m_t))
        hn = hh * lax.rsqrt(jnp.mean(hh * hh, -1, keepdims=True) + NORM_EPS) * prm[_MP_NORM:_MP_NORM + 1, sl]
        h_ref[gi, :, sl] = jax.nn.sigmoid(o_ref[gi, :, sl]) * hn * _silu(z_ref[gi, :, sl])
        b_end = b_col[L - 1:L, :]
        log_w = b_end - b_col + li_col
        m_new = jnp.maximum(b_end + m_prev, jnp.max(log_w, 0, keepdims=True))
        wk = jnp.exp(log_w - m_new)
        decay = jnp.exp(b_end + m_prev - m_new)
        kw = kh * wk
        c_scr[gi, h] = decay * ch + _mm_tn(kw, vh)
        n_scr[gi, h:h + 1, :] = decay * nh + jnp.sum(kw, 0, keepdims=True)
        m_scr[gi, :, h:h + 1] = m_new

    _run_interleaved([chain(gi, h) for gi in range(nb) for h in range(N_HEADS)])

    @pl.when(c == pl.num_programs(1) - 1)
    def _():
        c_ref[...] = c_scr[...]
        n_ref[...] = n_scr[...]
        m_ref[...] = m_scr[...]


BATCH_ROWS = 4


def _mlstm(P, t_valid, tail_q, tail_k, prm_blk, c0, n0, m0):
    B, tp, _ = P.shape
    L = MLSTM_L
    nb = BATCH_ROWS if B % BATCH_ROWS == 0 else 1
    col = lambda c0_: pl.BlockSpec((nb, L, GW), lambda b, c: (b, c, c0_ // GW))
    tok = pl.BlockSpec((nb, L, GW), lambda b, c: (b, c, 0))
    tail = pl.BlockSpec((nb, 8, GW), lambda b, c: (b, 0, 0))
    sc = pl.BlockSpec((nb, N_HEADS, HEAD_DIM, HEAD_DIM), lambda b, c: (b, 0, 0, 0))
    sn = pl.BlockSpec((nb, N_HEADS, HEAD_DIM), lambda b, c: (b, 0, 0))
    sm = pl.BlockSpec((nb, 1, LANES), lambda b, c: (b, 0, 0))
    return pl.pallas_call(
        functools.partial(_mlstm_body, t_valid=t_valid),
        out_shape=(jax.ShapeDtypeStruct((B, tp, GW), F32), jax.ShapeDtypeStruct(c0.shape, F32),
                   jax.ShapeDtypeStruct(n0.shape, F32), jax.ShapeDtypeStruct(m0.shape, F32)),
        grid=(B // nb, -(-t_valid // L)),
        in_specs=[col(C_MQ), col(C_MK), col(C_MV), col(C_MO), col(C_MZ),
                  pl.BlockSpec((nb, L, LANES), lambda b, c: (b, c, C_MG // LANES)), tail, tail,
                  pl.BlockSpec((16, GW), lambda b, c: (0, 0)), sc, sn, sm],
        out_specs=(tok, sc, sn, sm),
        scratch_shapes=[pltpu.VMEM((nb, N_HEADS, HEAD_DIM, HEAD_DIM), F32), pltpu.VMEM((nb, N_HEADS, HEAD_DIM), F32),
                        pltpu.VMEM((nb, 1, LANES), F32), pltpu.VMEM((nb, L, GW), F32), pltpu.VMEM((nb, L, GW), F32)],
        compiler_params=_cparams(("parallel", "arbitrary")), name="mlstm")(
            P, P, P, P, P, P, tail_q, tail_k, prm_blk, c0, n0, m0)


(_RP_MU_R, _RP_MU_K, _RP_MU_V, _RP_MU_L, _RP_W0, _RP_A0, _RP_KK, _RP_KA, _RP_RK, _RP_LN_G, _RP_LN_B) = range(11)


def _rwkv_body(r_ref, k_ref, v_ref, z_ref, l_ref, sr_ref, sk_ref, sv_ref, sl_ref, prm_ref, lora_ref, s0_ref,
               y_ref, s_ref, s_scr, cr_scr, ck_scr, cv_scr, cl_scr, *, prec, t_valid):
    c_id = pl.program_id(1)

    @pl.when(c_id == 0)
    def _():
        s_scr[...] = s0_ref[...]
        cr_scr[...] = sr_ref[...]
        ck_scr[...] = sk_ref[...]
        cv_scr[...] = sv_ref[...]
        cl_scr[...] = sl_ref[...]

    nb, L = r_ref.shape[0], r_ref.shape[1]
    D = HEAD_DIM
    row = lax.broadcasted_iota(jnp.int32, (L, L), 0)
    col = lax.broadcasted_iota(jnp.int32, (L, L), 1)
    lower = row >= col
    strict = row > col
    tril = lower.astype(F32)
    n_sq = int(np.log2(L)) - 1
    pc, pa, prec = prec
    trow = lax.broadcasted_iota(jnp.int32, (L, 1), 0)
    valid = (c_id * L + trow) < t_valid
    prm = prm_ref[...]
    prow = lambda i, n=GW: prm[i:i + 1, :n]

    def shifted_mix(x_ref, carry_scr, g, mu):
        x = x_ref[g]
        prev = jnp.where(trow >= 1, pltpu.roll(x, 1, 0), carry_scr[g])
        carry_scr[g] = x[L - 1:L, :]
        return x + (prev - x) * mu

    prep = []
    for g in range(nb):
        r = shifted_mix(r_ref, cr_scr, g, prow(_RP_MU_R))
        k = shifted_mix(k_ref, ck_scr, g, prow(_RP_MU_K))
        v = shifted_mix(v_ref, cv_scr, g, prow(_RP_MU_V))
        lo = shifted_mix(l_ref, cl_scr, g, prow(_RP_MU_L, LANES))
        lo_in = jnp.where(lax.broadcasted_iota(jnp.int32, (1, LANES), 1) < DECAY_LORA, jnp.tanh(lo), lo)
        lora = _mm(lo_in, lora_ref[...])
        w_pre = prow(_RP_W0) + lora[:, :GW]
        log_w = -(float(np.exp(-0.5)) * jax.nn.sigmoid(w_pre))
        a = jax.nn.sigmoid(prow(_RP_A0) + lora[:, GW:])
        k_eff = k * (1.0 + (a - 1.0) * prow(_RP_KA))
        kk_raw = k * prow(_RP_KK)
        log_w = jnp.where(valid, log_w, 0.0)
        k_eff = jnp.where(valid, k_eff, 0.0)
        kk_raw = jnp.where(valid, kk_raw, 0.0)
        v = jnp.where(valid, v, 0.0)
        prep.append((r, log_w, k_eff, v, kk_raw, a))

    def chain(g, h):
        sl = slice(h * D, (h + 1) * D)
        r, w, k, v, kk, a = (x[:, sl] for x in prep[g])
        kk = kk / jnp.maximum(jnp.sqrt(jnp.sum(kk * kk, -1, keepdims=True)), 1e-12)
        cum = _mm(tril, w, pc)
        yield
        c_last = cum[L - 1:L, :]
        e_neg = jnp.exp(-cum)
        kh = kk * jnp.exp(cum - w)
        bt = kk * a * e_neg
        kt = k * e_neg
        rh = r * jnp.exp(cum)
        gram = _mm_nt(jnp.concatenate([kh, rh], 0), jnp.concatenate([bt, kt], 0), pa)
        yield
        A = jnp.where(strict, gram[:L, :L], 0.0)
        Bm = jnp.where(strict, gram[:L, L:], 0.0)
        Mb = jnp.where(lower, gram[L:, :L], 0.0)
        Mk = jnp.where(lower, gram[L:, L:], 0.0)
        X = jnp.concatenate([kh, _mm(Bm, v, pa)], 1)
        yield
        Pw = A
        X = X - _mm(Pw, X, pa)
        yield
        for _ in range(n_sq):
            Pw = _mm(Pw, Pw, pa)
            yield
            X = X + _mm(Pw, X, pa)
            yield
        e_end = jnp.exp(c_last - cum)
        bp = kk * a * e_end
        kp = k * e_end
        xtb = _mm_tn(X, bp, prec)
        yield
        wtb = xtb[:D]
        N = _mm_tn(v, kp, prec) - xtb[D:]
        yield
        mbx = _mm(Mb, X, prec)
        yield
        qp = rh - mbx[:, :D]
        y0 = _mm(Mk, v, prec) - mbx[:, D:]
        yield
        s0 = s_scr[g, h]
        y = _mm_nt(qp, s0, prec) + y0
        yield
        s_scr[g, h] = s0 * jnp.exp(c_last) - _mm(s0, wtb, prec) + N
        mu = jnp.mean(y, -1, keepdims=True)
        var = jnp.mean(jnp.square(y - mu), -1, keepdims=True)
        y = (y - mu) * lax.rsqrt(var + RWKV_LN_EPS) * prm[_RP_LN_G:_RP_LN_G + 1, sl] + prm[_RP_LN_B:_RP_LN_B + 1, sl]
        y = y + jnp.sum(r * k * prm[_RP_RK:_RP_RK + 1, sl], -1, keepdims=True) * v
        y_ref[g, :, sl] = y * _silu(z_ref[g, :, sl])

    _run_interleaved([chain(g, h) for g in range(nb) for h in range(N_HEADS)])

    @pl.when(c_id == pl.num_programs(1) - 1)
    def _():
        s_ref[...] = s_scr[...]


def _rwkv(P, t_valid, shift_buf, prm_blk, lora_w, s0, prec):
    B, tp, _ = P.shape
    L = RWKV_L
    nb = BATCH_ROWS if B % BATCH_ROWS == 0 else 1
    col = lambda c0_: pl.BlockSpec((nb, L, GW), lambda b, c: (b, c, c0_ // GW))
    tok = pl.BlockSpec((nb, L, GW), lambda b, c: (b, c, 0))
    car = lambda w: pl.BlockSpec((nb, 1, w), lambda b, c: (b, 0, 0))
    st = pl.BlockSpec((nb, N_HEADS, HEAD_DIM, HEAD_DIM), lambda b, c: (b, 0, 0, 0))
    shifts = [shift_buf[..., i * GW:(i + 1) * GW] for i in range(3)] + [shift_buf[..., 3 * GW:]]
    return pl.pallas_call(
        functools.partial(_rwkv_body, prec=prec, t_valid=t_valid),
        out_shape=(jax.ShapeDtypeStruct((B, tp, GW), F32), jax.ShapeDtypeStruct(s0.shape, F32)),
        grid=(B // nb, -(-t_valid // L)),
        in_specs=[col(C_RR), col(C_RK), col(C_RV), col(C_RZ),
                  pl.BlockSpec((nb, L, LANES), lambda b, c: (b, c, C_RL // LANES)),
                  car(GW), car(GW), car(GW), car(LANES),
                  pl.BlockSpec((16, GW), lambda b, c: (0, 0)), pl.BlockSpec((LANES, 2 * GW), lambda b, c: (0, 0)), st],
        out_specs=(tok, st),
        scratch_shapes=[pltpu.VMEM((nb, N_HEADS, HEAD_DIM, HEAD_DIM), F32), pltpu.VMEM((nb, 1, GW), F32),
                        pltpu.VMEM((nb, 1, GW), F32), pltpu.VMEM((nb, 1, GW), F32), pltpu.VMEM((nb, 1, LANES), F32)],
        compiler_params=_cparams(("parallel", "arbitrary")), name="rwkv")(
            P, P, P, P, P, *shifts, prm_blk, lora_w, s0)


def _subproj_accumulate(load_rows, w_ref, o_ref):
    n = o_ref.shape[-2]
    accs = []
    for h in range(NSA_KV_HEADS):
        acc = jnp.zeros((n, 2 * LANES), F32)
        for s in range(0, CMP_STRIDE, 2):
            xs = jnp.concatenate([load_rows(h, s, n), load_rows(h, s + 1, n)], 1)
            acc = acc + jnp.dot(xs.astype(BF16), w_ref[s // 2], preferred_element_type=F32)
        accs.append(acc)
    out = jnp.concatenate([accs[0][:, :LANES], accs[1][:, :LANES], accs[0][:, LANES:], accs[1][:, LANES:]], 1)
    o_ref[...] = out.reshape(o_ref.shape)


def _subproj_body(x0_ref, x1_ref, w_ref, o_ref):
    xs = (x0_ref, x1_ref)
    _subproj_accumulate(lambda h, s, n: xs[h][pl.ds(s, n, stride=CMP_STRIDE), :], w_ref, o_ref)


def _subproj(rows2d, col0, w):
    n = rows2d.shape[0]
    tm = min(2048, n)
    return pl.pallas_call(
        _subproj_body, out_shape=jax.ShapeDtypeStruct((n // CMP_STRIDE, 4 * LANES), F32), grid=(n // tm,),
        in_specs=[pl.BlockSpec((tm, LANES), lambda i: (i, col0)), pl.BlockSpec((tm, LANES), lambda i: (i, col0 + 1)),
                  pl.BlockSpec((CMP_STRIDE // 2, 2 * LANES, 2 * LANES), lambda i: (0, 0, 0))],
        out_specs=pl.BlockSpec((tm // CMP_STRIDE, 4 * LANES), lambda i: (i, 0)),
        compiler_params=_cparams(("parallel",)), name="cmp_subproj")(rows2d, rows2d, w)


_PAGES_PER_STEP = 32


def _subproj_pages_body(pt_ref, *refs):
    del pt_ref
    npg = len(refs) - 3
    w_ref, o_ref, rows_scr = refs[npg], refs[npg + 1], refs[npg + 2]
    n_half = 2 if npg % 2 == 0 else 1
    pph = npg // n_half
    spp = PAGE_SIZE // CMP_STRIDE
    for half in range(n_half):
        for p in range(half * pph, (half + 1) * pph):
            for h in range(NSA_KV_HEADS):
                rows_scr[h, p * PAGE_SIZE:(p + 1) * PAGE_SIZE, :] = refs[p][0, 0, h].reshape(2 * HEAD_DIM, PAGE_SIZE).T
        row0 = half * pph * PAGE_SIZE
        _subproj_accumulate(lambda h, s, n: rows_scr[h, pl.ds(row0 + s, n, stride=CMP_STRIDE), :], w_ref,
                            o_ref.at[:, half * pph * spp:(half + 1) * pph * spp, :])


def _subproj_pages(cache_t, page_table, layer, w):
    B, n_pages = page_table.shape
    npg = min(_PAGES_PER_STEP, n_pages)
    spp = PAGE_SIZE // CMP_STRIDE

    def page_spec(p):
        return pl.BlockSpec((1, 1, NSA_KV_HEADS, 2, HEAD_DIM, PAGE_SIZE),
                            lambda b, g, pt: (layer, pt[b, g * npg + p], 0, 0, 0, 0))

    gs = pltpu.PrefetchScalarGridSpec(
        num_scalar_prefetch=1, grid=(B, n_pages // npg),
        in_specs=[page_spec(p) for p in range(npg)]
        + [pl.BlockSpec((CMP_STRIDE // 2, 2 * LANES, 2 * LANES), lambda b, g, pt: (0, 0, 0))],
        out_specs=pl.BlockSpec((1, npg * spp, 4 * LANES), lambda b, g, pt: (b, g, 0)),
        scratch_shapes=[pltpu.VMEM((NSA_KV_HEADS, npg * PAGE_SIZE, LANES), F32)])
    return pl.pallas_call(
        _subproj_pages_body, out_shape=jax.ShapeDtypeStruct((B, n_pages * spp, 4 * LANES), F32), grid_spec=gs,
        compiler_params=_cparams(("parallel", "arbitrary"), 48), name="cmp_subproj_pages")(
            page_table, *([cache_t] * npg), w)


def _cmp_mlp_body(h_ref, w2_ref, g_ref, o_ref):
    x = h_ref[0]
    kv = _mm(jax.nn.gelu(x), w2_ref[...])
    g = g_ref[...]
    segs = []
    for j in range(4):
        seg = kv[:, j * HEAD_DIM:(j + 1) * HEAD_DIM]
        if j % 2 == 0:
            seg = seg * lax.rsqrt(jnp.mean(seg * seg, -1, keepdims=True) + NORM_EPS) * g
        segs.append(seg)
    o_ref[0] = jnp.concatenate(segs, 1)


def _cmp_mlp(hid, w2bd, g):
    B, n, _ = hid.shape
    tn = min(512, n)
    return pl.pallas_call(
        _cmp_mlp_body, out_shape=jax.ShapeDtypeStruct((B, n, 256), F32), grid=(B, n // tn),
        in_specs=[pl.BlockSpec((1, tn, 256), lambda b, i: (b, i, 0)), pl.BlockSpec((256, 256), lambda b, i: (0, 0)),
                  pl.BlockSpec((1, HEAD_DIM), lambda b, i: (0, 0))],
        out_specs=pl.BlockSpec((1, tn, 256), lambda b, i: (b, i, 0)),
        compiler_params=_cparams(("parallel", "parallel")), name="cmp_mlp")(hid, w2bd, g.reshape(1, HEAD_DIM))


def _cmp_attn_body(q_ref, kv_ref, cov_ref, o_ref, imp_ref, *, pos0):
    qi = pl.program_id(2)
    tq = q_ref.shape[1]
    n = kv_ref.shape[1]
    kv = kv_ref[0]
    k = kv[:, :HEAD_DIM]
    v = kv[:, HEAD_DIM:]
    q = q_ref[0]
    pos = pos0 + qi * tq + lax.broadcasted_iota(jnp.int32, (tq, 1), 0)
    end = lax.broadcasted_iota(jnp.int32, (1, n), 1) * CMP_STRIDE + (CMP_BLOCK - 1)
    mask = end <= pos
    psum = jnp.zeros((tq, n), F32)
    outs = []
    for g in range(2):
        s = _mm_nt(q[:, g * HEAD_DIM:(g + 1) * HEAD_DIM], k) * SCALE
        s = jnp.where(mask, s, -jnp.inf)
        m = jnp.max(s, -1, keepdims=True)
        e = jnp.exp(s - jnp.where(m == -jnp.inf, 0.0, m))
        p = e / jnp.maximum(jnp.sum(e, -1, keepdims=True), 1e-30)
        outs.append(_mm(p, v))
        psum = psum + p
    o_ref[0] = jnp.concatenate(outs, 1)
    hi = psum.astype(BF16)
    lo = (psum - hi.astype(F32)).astype(BF16)
    cov = cov_ref[...]
    imp_ref[0, 0] = _mm(hi, cov) + _mm(lo, cov)


def _cmp_attn(qn, kv_cmp, cover, pos0, tq):
    B, T, _ = qn.shape
    n = kv_cmp.shape[1]
    nbp = cover.shape[1]
    return pl.pallas_call(
        functools.partial(_cmp_attn_body, pos0=pos0),
        out_shape=(jax.ShapeDtypeStruct((B, T, GW), F32), jax.ShapeDtypeStruct((B, NSA_KV_HEADS, T, nbp), F32)),
        grid=(B, NSA_KV_HEADS, T // tq),
        in_specs=[pl.BlockSpec((1, tq, LANES), lambda b, h, i: (b, i, h)),
                  pl.BlockSpec((1, n, LANES), lambda b, h, i: (b, 0, h)),
                  pl.BlockSpec((n, nbp), lambda b, h, i: (0, 0))],
        out_specs=(pl.BlockSpec((1, tq, LANES), lambda b, h, i: (b, i, h)),
                   pl.BlockSpec((1, 1, tq, nbp), lambda b, h, i: (b, h, i, 0))),
        compiler_params=_cparams(("parallel", "parallel", "arbitrary")), name="cmp_attn")(qn, kv_cmp, cover)


def _topk_body(imp_ref, sel_ref, idx_ref, cnt_scr, *, pos0, t_rows, n_blk, want_idx):
    for sub in range(imp_ref.shape[0] // LANES):
        rows = slice(sub * LANES, (sub + 1) * LANES)
        _topk_tile(imp_ref.at[rows, :], sel_ref.at[rows, :], idx_ref.at[:, rows], cnt_scr,
                   pl.program_id(0) * (imp_ref.shape[0] // LANES) + sub,
                   pos0=pos0, t_rows=t_rows, n_blk=n_blk, want_idx=want_idx)


def _topk_tile(imp_ref, sel_ref, idx_ref, cnt_scr, ti, *, pos0, t_rows, n_blk, want_idx):
    x = imp_ref[...]
    nbp = x.shape[1]
    nbu = cnt_scr.shape[0]
    xt = jnp.concatenate([x[:, j * LANES:(j + 1) * LANES].T for j in range(nbp // LANES)], 0)[:nbu]
    r = ti * LANES + lax.broadcasted_iota(jnp.int32, (1, LANES), 1)
    cur = (pos0 + r % t_rows) // SLC_BLOCK
    t_last = (ti * LANES) % t_rows + LANES - 1 if t_rows % LANES == 0 else t_rows - 1
    cur_max = (pos0 + t_last) // SLC_BLOCK
    blk = lax.broadcasted_iota(jnp.int32, (nbu, 1), 0)
    forced = (blk == 0) | (blk == cur) | (blk == cur - 1)
    val = jnp.where(forced, jnp.inf, jnp.where(blk <= cur, xt, -jnp.inf))
    val = jnp.where(blk < n_blk, val, -jnp.inf)
    cnt_scr[...] = jnp.zeros(cnt_scr.shape, F32)
    for g0 in range(0, n_blk, 8):
        @pl.when(g0 <= cur_max)
        def _():
            cnt = cnt_scr[...]
            for i in range(g0, min(g0 + 8, n_blk)):
                vi = val[i:i + 1, :]
                ahead = (vi > val) | ((vi == val) & (blk > i))
                cnt = cnt + jnp.where(ahead, 1.0, 0.0)
            cnt_scr[...] = cnt
    cnt = cnt_scr[...]
    chosen = (cnt < float(N_SELECT)) & (val > -jnp.inf)
    self32 = jnp.where(chosen, 1.0, 0.0)
    if nbu < nbp:
        self32 = jnp.concatenate([self32, jnp.zeros((nbp - nbu, LANES), F32)], 0)
    sel_ref[...] = jnp.concatenate([self32[j * LANES:(j + 1) * LANES, :].T for j in range(nbp // LANES)], 1)
    if want_idx:
        blk_f = blk.astype(F32)
        rows = []
        for j in range(N_SELECT):
            hit = chosen & (cnt == float(j))
            rows.append(jnp.sum(jnp.where(hit, blk_f + 1.0, 0.0), 0, keepdims=True) - 1.0)
        idx_ref[...] = jnp.concatenate(rows, 0).astype(jnp.int32)
    else:
        idx_ref[...] = jnp.zeros(idx_ref.shape, jnp.int32)


def _topk(imp2d, pos0, t_rows, n_blk, want_idx):
    R, nbp = imp2d.shape
    tr = 4 * LANES if R % (4 * LANES) == 0 else LANES
    return pl.pallas_call(
        functools.partial(_topk_body, pos0=pos0, t_rows=t_rows, n_blk=n_blk, want_idx=want_idx),
        out_shape=(jax.ShapeDtypeStruct((R, nbp), F32), jax.ShapeDtypeStruct((N_SELECT, R), jnp.int32)),
        scratch_shapes=[pltpu.VMEM((_round_up(n_blk, 8), LANES), F32)],
        grid=(R // tr,),
        in_specs=[pl.BlockSpec((tr, nbp), lambda i: (i, 0))],
        out_specs=(pl.BlockSpec((tr, nbp), lambda i: (i, 0)), pl.BlockSpec((N_SELECT, tr), lambda i: (0, i))),
        compiler_params=_cparams(("parallel",)), name="topk")(imp2d)


def _attn_body(*refs, mode, pairs, tk, pos_q0, pos_k0, scale):
    if mode == 'slc':
        q_ref, kv_ref, sel_ref, blk_ref, o_ref = refs
    elif mode == 'none':
        q_ref, kv_ref, g_ref, ones_ref, o_ref = refs
    else:
        q_ref, kv_ref, o_ref = refs
    qi = pl.program_id(2)
    tq = q_ref.shape[1]
    n_k = kv_ref.shape[1]
    n_tiles = n_k // tk
    q = q_ref[0]
    if mode == 'none':
        ms = _mm(q * q, ones_ref[...], 'l3') * (1.0 / HEAD_DIM)
        q = q * lax.rsqrt(ms + NORM_EPS) * g_ref[...]
    pq0 = pos_q0 + qi * tq
    qpos = pq0 + lax.broadcasted_iota(jnp.int32, (tq, 1), 0)
    unroll = (4 if mode == 'slc' and n_tiles % 4 == 0 else 2) if n_tiles % 2 == 0 else 1
    if mode == 'none':
        lo, hi = 0, n_tiles // unroll
    else:
        r_hi = jnp.minimum(n_k - 1, pq0 + tq - 1 - pos_k0)
        hi = r_hi // (tk * unroll) + 1
        lo = jnp.maximum(0, pq0 - (WINDOW - 1) - pos_k0) // (tk * unroll) if mode == 'win' else 0
    qs = [q[:, q_lo:q_lo + HEAD_DIM] * scale for q_lo, _, _ in pairs]
    if mode == 'slc':
        nbp = sel_ref.shape[3]
        sel_bias = ((sel_ref[0, 0] - 1.0) * (-NEG_BIG)).astype(BF16)
        qs = [jnp.concatenate([sel_bias, qh.astype(BF16)], 1) for qh in qs]

    def step(it, carry, causal=True):
        tiles = []
        for u in range(unroll):
            r0 = pl.multiple_of((it * unroll + u) * tk, tk)
            kvt = kv_ref[0, pl.ds(r0, tk), :]
            kpos = pos_k0 + r0 + lax.broadcasted_iota(jnp.int32, (1, tk), 1)
            onehot = None
            if mode == 'win':
                d = qpos - kpos
                mask = (d >= 0) & (d < WINDOW) & (kpos >= 0)
            elif mode == 'slc':
                onehot = blk_ref[pl.ds(r0, tk), :]
                mask = (kpos <= qpos) if causal else None
            else:
                mask = None
            tiles.append((kvt, mask, onehot))
        new = [None] * (3 * len(pairs))

        def chain(p):
            _, k_lo, v_lo = pairs[p]
            m, l, acc = carry[3 * p:3 * p + 3]
            if mode == 'slc':
                ss = [_mm_nt(qs[p], jnp.concatenate([oh, kvt[:, k_lo:k_lo + HEAD_DIM].astype(BF16)], 1))
                      for kvt, _, oh in tiles]
            else:
                ss = [_mm_nt(qs[p], kvt[:, k_lo:k_lo + HEAD_DIM]) for kvt, _, _ in tiles]
            yield
            ss = [s if mask is None else jnp.where(mask, s, -jnp.inf) for s, (_, mask, _) in zip(ss, tiles)]
            m_new = m
            for s in ss:
                m_new = jnp.maximum(m_new, jnp.max(s, -1, keepdims=True))
            alpha = jnp.exp(m - m_new)
            m_wide = jnp.concatenate([m_new] * (tk // LANES), 1)
            prs = [jnp.exp(s - m_wide) for s in ss]
            yield
            l = alpha * l
            acc = alpha[:, :HEAD_DIM] * acc
            for pr, (kvt, _, _) in zip(prs, tiles):
                l = l + jnp.sum(pr, -1, keepdims=True)
                acc = acc + _mm(pr, kvt[:, v_lo:v_lo + HEAD_DIM])
            new[3 * p:3 * p + 3] = [m_new, l, acc]

        _run_interleaved([chain(p) for p in range(len(pairs))])
        return tuple(new)

    init = []
    for _ in pairs:
        init += [jnp.full((tq, LANES), NEG_BIG, F32), jnp.zeros((tq, LANES), F32), jnp.zeros((tq, HEAD_DIM), F32)]
    if mode == 'slc':
        hi_full = jnp.minimum(hi, (pq0 - pos_k0 + 1) // (tk * unroll))
        res = lax.fori_loop(lo, hi_full, functools.partial(step, causal=False), tuple(init))
        res = lax.fori_loop(hi_full, hi, step, res)
    else:
        res = lax.fori_loop(lo, hi, step, tuple(init))
    outs = []
    for p in range(len(pairs)):
        m, l, acc = res[3 * p:3 * p + 3]
        outs.append(jnp.where(m[:, :HEAD_DIM] > 0.5 * NEG_BIG, acc / jnp.maximum(l[:, :HEAD_DIM], 1e-30), 0.0))
    o_ref[0] = jnp.concatenate(outs, 1)


def _attn(q, kv, mode, pairs, kv_width, pos_q0, pos_k0, tq, tk, sel=None, q_col0=0, q_gain=None, n_rows=None):
    B = q.shape[0]
    T = q.shape[1] if n_rows is None else n_rows
    n_k = kv.shape[1]
    in_specs = [pl.BlockSpec((1, tq, LANES), lambda b, h, i: (b, i, q_col0 + h)),
                pl.BlockSpec((1, n_k, kv_width), lambda b, h, i: (b, 0, h))]
    args = [q, kv]
    if mode == 'none':
        head_of = np.arange(LANES) // HEAD_DIM
        in_specs += [pl.BlockSpec((1, LANES), lambda b, h, i: (0, 0)), pl.BlockSpec((LANES, LANES), lambda b, h, i: (0, 0))]
        args += [q_gain.reshape(1, LANES), jnp.asarray(head_of[:, None] == head_of[None, :], dtype=BF16)]
    if mode == 'slc':
        nbp = sel.shape[-1]
        in_specs.append(pl.BlockSpec((1, 1, tq, nbp), lambda b, h, i: (b, h, i, 0)))
        args.append(sel)
        key_blk = (pos_k0 + np.arange(n_k)) // SLC_BLOCK
        in_specs.append(pl.BlockSpec((n_k, nbp), lambda b, h, i: (0, 0)))
        args.append(jnp.asarray(key_blk[:, None] == np.arange(nbp)[None, :], dtype=BF16))
    return pl.pallas_call(
        functools.partial(_attn_body, mode=mode, pairs=pairs, tk=tk, pos_q0=pos_q0, pos_k0=pos_k0, scale=SCALE),
        out_shape=jax.ShapeDtypeStruct((B, T, GW), F32), grid=(B, 2, T // tq),
        in_specs=in_specs, out_specs=pl.BlockSpec((1, tq, LANES), lambda b, h, i: (b, i, h)),
        compiler_params=_cparams(("parallel", "parallel", "arbitrary")), name="attn_" + mode)(*args)


_GQA_PAIRS = ((0, 0, HEAD_DIM), (HEAD_DIM, 0, HEAD_DIM))
_MHA_PAIRS = ((0, 0, HEAD_DIM), (HEAD_DIM, 2 * HEAD_DIM, 3 * HEAD_DIM))


def _slc_paged_body(idx_ref, phys_ref, *refs, pos0, blk0, t_real):
    del phys_ref
    n_slots = t_real * N_SELECT
    q_ref = refs[0]
    blk_refs = refs[1:1 + n_slots]
    new_ref = refs[1 + n_slots]
    o_ref = refs[2 + n_slots]
    b, h = pl.program_id(0), pl.program_id(1)
    bpp = PAGE_SIZE // SLC_BLOCK
    tok = lax.broadcasted_iota(jnp.int32, (1, PAGE_SIZE), 1)
    lane = lax.broadcasted_iota(jnp.int32, (1, SLC_BLOCK), 1)
    newblk = new_ref[0]

    def chain(t):
        base = ((b * NSA_KV_HEADS + h) * t_real + t) * N_SELECT
        qrow = q_ref[0, t:t + 1, :]
        q2 = jnp.concatenate([qrow[:, :HEAD_DIM], qrow[:, HEAD_DIM:], jnp.zeros((6, HEAD_DIM), F32)], 0) * SCALE
        pos = pos0 + t
        scores, vts = [], []
        n_new = jnp.int32(0)
        for j in range(N_SELECT):
            idx = idx_ref[base + j]
            idc = jnp.maximum(idx, 0)
            kv_t = blk_refs[t * N_SELECT + j][0, 0, 0]
            ok = ((idx >= 0) & (idx < blk0) & (tok // SLC_BLOCK == idc % bpp)
                  & ((idc // bpp) * PAGE_SIZE + tok <= pos))
            scores.append(_mm(q2, kv_t[0]) + jnp.where(ok, 0.0, -jnp.inf))
            vts.append(kv_t[1])
            n_new = n_new + jnp.where(idx >= blk0, 1, 0)
        ok_new = (n_new > 0) & (blk0 * SLC_BLOCK + lane <= pos)
        s_new = _mm_nt(q2, newblk[:, :HEAD_DIM]) + jnp.where(ok_new, 0.0, -jnp.inf)
        yield
        m = jnp.max(s_new, -1, keepdims=True)
        for s in scores:
            m = jnp.maximum(m, jnp.max(s, -1, keepdims=True))
        m = jnp.where(m == -jnp.inf, 0.0, m)
        e_new = jnp.exp(s_new - m)
        den = jnp.sum(e_new, -1, keepdims=True)
        es = [jnp.exp(s - m) for s in scores]
        yield
        o = _mm(e_new, newblk[:, HEAD_DIM:])
        for e, vt in zip(es, vts):
            den = den + jnp.sum(e, -1, keepdims=True)
            o = o + _mm_nt(e, vt)
        o = o / jnp.maximum(den, 1e-30)
        orow = jnp.concatenate([o[0:1], o[1:2]], 1)
        o_ref[0, 0, t] = jnp.broadcast_to(orow, (8, LANES))

    _run_interleaved([chain(t) for t in range(t_real)])


def _slc_paged(q_rot, cache_t, new_rows, idx_flat, page_flat, layer, pos0, blk0, t_real):
    B = q_rot.shape[0]
    tp = q_rot.shape[1]
    n_slots = t_real * N_SELECT

    def blk_spec(s):
        def imap(b, h, idx, page):
            return (layer, page[(b * NSA_KV_HEADS + h) * n_slots + s], h, 0, 0, 0)
        return pl.BlockSpec((1, 1, 1, 2, HEAD_DIM, PAGE_SIZE), imap)

    gs = pltpu.PrefetchScalarGridSpec(
        num_scalar_prefetch=2, grid=(B, NSA_KV_HEADS),
        in_specs=[pl.BlockSpec((1, tp, LANES), lambda b, h, idx, page: (b, 0, h))]
        + [blk_spec(s) for s in range(n_slots)]
        + [pl.BlockSpec((1, SLC_BLOCK, LANES), lambda b, h, idx, page: (b, 0, h))],
        out_specs=pl.BlockSpec((1, 1, t_real, 8, LANES), lambda b, h, idx, page: (b, h, 0, 0, 0)))
    out = pl.pallas_call(
        functools.partial(_slc_paged_body, pos0=pos0, blk0=blk0, t_real=t_real),
        out_shape=jax.ShapeDtypeStruct((B, NSA_KV_HEADS, t_real, 8, LANES), F32), grid_spec=gs,
        compiler_params=_cparams(("parallel", "arbitrary")), name="slc_paged")(
            idx_flat, page_flat, q_rot, *([cache_t] * n_slots), new_rows)
    return jnp.transpose(out[:, :, :, 0, :], (0, 2, 1, 3)).reshape(B, t_real, GW)


def _nsa_prep_body(q_ref, s_ref, w_ref, cos_ref, sin_ref, g_ref, ones_ref, qn_ref, qr_ref, so_ref, wo_ref):
    lane = lax.broadcasted_iota(jnp.int32, (1, GW), 1)
    first_half = (lane % HEAD_DIM) < (HEAD_DIM // 2)
    is_k = (lane // HEAD_DIM) % 2 == 0
    cos, sin = cos_ref[...], sin_ref[...]
    ones = ones_ref[...]

    def norm(x, g):
        ms = _mm(x * x, ones, 'l3') * (1.0 / HEAD_DIM)
        return x * lax.rsqrt(ms + NORM_EPS) * g

    def rope(x):
        swapped = jnp.where(first_half, pltpu.roll(x, GW - HEAD_DIM // 2, 1), pltpu.roll(x, HEAD_DIM // 2, 1))
        return x * cos + swapped * sin

    qn = norm(q_ref[0], g_ref[0:1, :])
    qn_ref[0] = qn
    qr_ref[0] = rope(qn)
    for x_ref, o_ref, gi in ((s_ref, so_ref, 1), (w_ref, wo_ref, 2)):
        x = x_ref[0]
        o_ref[0] = jnp.where(is_k, rope(norm(x, g_ref[gi:gi + 1, :])), x)


def _nsa_prep(P3, n_rows, tr, cos, sin, g_blk, ones_bd):
    B = P3.shape[0]
    col = lambda c0: pl.BlockSpec((1, tr, GW), lambda b, i: (b, i, c0 // GW))
    tab = pl.BlockSpec((tr, GW), lambda b, i: (i, 0))
    out = pl.BlockSpec((1, tr, GW), lambda b, i: (b, i, 0))
    shp = jax.ShapeDtypeStruct((B, n_rows, GW), F32)
    return pl.pallas_call(
        _nsa_prep_body, out_shape=(shp, shp, shp, shp), grid=(B, n_rows // tr),
        in_specs=[col(C_NQ), col(C_NSLC), col(C_NWIN), tab, tab, pl.BlockSpec((8, GW), lambda b, i: (0, 0)),
                  pl.BlockSpec((GW, GW), lambda b, i: (0, 0))],
        out_specs=(out, out, out, out),
        compiler_params=_cparams(("parallel", "parallel")), name="nsa_prep")(P3, P3, P3, cos, sin, g_blk, ones_bd)


def _rms(x, g):
    return x * lax.rsqrt(jnp.mean(x * x, -1, keepdims=True) + NORM_EPS) * g


def _rope(x, pos):
    half = HEAD_DIM // 2
    inv = ROPE_THETA ** (-jnp.arange(half, dtype=F32) / half)
    ang = pos.astype(F32)[:, None] * inv
    cos, sin = jnp.cos(ang)[:, None, :], jnp.sin(ang)[:, None, :]
    x1, x2 = x[..., :half], x[..., half:]
    return jnp.concatenate([x1 * cos - x2 * sin, x1 * sin + x2 * cos], -1)


def _pad_t(x, tp, value=0.0):
    t = x.shape[1]
    if t == tp:
        return x
    return jnp.pad(x, ((0, 0), (0, tp - t)) + ((0, 0),) * (x.ndim - 2), constant_values=value)


def _round_up(n, m):
    return -(-n // m) * m


def _cover_matrix(n_cmp, n_cmp_pad, n_slc, nbp):
    start = np.arange(n_cmp_pad)[:, None] * CMP_STRIDE
    blk = np.arange(nbp)[None, :]
    cov = (start < (blk + 1) * SLC_BLOCK) & (start + CMP_BLOCK > blk * SLC_BLOCK)
    cov &= (np.arange(n_cmp_pad)[:, None] < n_cmp) & (blk < n_slc)
    return jnp.asarray(cov.astype(np.float32), dtype=BF16)


def _mlstm_mixer(P, Pp, conv_buf, c0, n0, m0, prm):
    B, T, _ = P.shape
    keep = MLSTM_CONV - 1
    tail = jnp.pad(conv_buf, ((0, 0), (8 - keep, 0), (0, 0)))
    m0p = jnp.pad(m0, ((0, 0), (0, LANES - N_HEADS))).reshape(B, 1, LANES)
    out, C, n, m = _mlstm(Pp, T, tail[..., :GW], tail[..., GW:], prm['mlstm_blk'], c0, n0, m0p)
    qk_raw = jnp.concatenate([P[:, -keep:, C_MQ:C_MQ + GW], P[:, -keep:, C_MK:C_MK + GW]], -1)
    conv_new = jnp.concatenate([conv_buf, qk_raw], 1)[:, -keep:]
    return out[:, :T], C, n, m[:, 0, :N_HEADS], conv_new


def _rwkv_mixer(P, Pp, shift_buf, s0, prm, prec):
    B, T, _ = P.shape
    out, S = _rwkv(Pp, T, shift_buf, prm['rwkv_blk'], prm['rwkv_lora'], s0, prec)
    shift_new = jnp.concatenate([P[:, -1:, C_RR:C_RR + 3 * GW], P[:, -1:, C_RL:C_RL + LANES]], -1)
    return out[:, :T], S, shift_new


def _memory_kv(mem, prm):
    B = mem.shape[0]
    kv = _proj_in(mem.reshape(B * N_MEM, D_MODEL), prm['mem_norm_g'], prm['w_mem_kv_bf16'], 2 * GW)
    kv = kv.reshape(B, N_MEM, MEM_HEADS, 2, HEAD_DIM)
    return jnp.stack([_rms(kv[:, :, :, 0], prm['mem_qk_g'][1]), kv[:, :, :, 1]], 3)


def _memory_mixer(Pp, T, mem_kv, prm):
    B = Pp.shape[0]
    tp = _round_up(T, 8)
    tq = min(512, tp)
    o = _attn(Pp, mem_kv.reshape(B, N_MEM, 2 * GW), 'none', _MHA_PAIRS, 2 * LANES, 0, 0, tq, N_MEM,
              q_col0=C_CQ // LANES, q_gain=jnp.tile(prm['mem_qk_g'][0], 2), n_rows=tp)
    return o[:, :T]


def _rope_tables(pos):
    half = HEAD_DIM // 2
    inv = ROPE_THETA ** (-jnp.arange(half, dtype=F32) / half)
    ang = pos.astype(F32)[:, None] * inv
    cos, sin = jnp.cos(ang), jnp.sin(ang)
    return (jnp.tile(jnp.concatenate([cos, cos], -1), (1, N_HEADS)),
            jnp.tile(jnp.concatenate([-sin, sin], -1), (1, N_HEADS)))


def _nsa_mixer(P2d, Pp, B, T, pos0, win_prefix, prm, past):
    P = P2d.reshape(B, T, DP)
    g = prm['nsa_qk_g']
    tp = _round_up(T, 8)
    tq = min(256, tp)
    cos, sin = _rope_tables(pos0 + jnp.arange(tp, dtype=jnp.int32))
    qn, q_rot, slc_rows, win_rows = _nsa_prep(Pp, tp, min(512, tp), cos, sin, prm['nsa_prep_blk'], prm['head_ones'])
    kvrows = lambda t: t[:, :T].reshape(B, T, NSA_KV_HEADS, 2, HEAD_DIM)
    cmp_new = kvrows(P[..., C_NCMP:C_NCMP + 2 * KVW])
    slc_new, win_new = kvrows(slc_rows), kvrows(win_rows)

    L_all = pos0 + T
    n_sub = max(-(-L_all // CMP_STRIDE), CMP_BLOCK // CMP_STRIDE)
    n_cmp = n_sub - 1
    bd, w2bd, pe_hid = prm['cmp_bd'], prm['cmp_w2bd'], prm['cmp_pe_hid']
    if past is None:
        G = _subproj(P2d, C_NCMP // LANES, bd).reshape(B, T // CMP_STRIDE, 4 * LANES)
    else:
        cache_cmp_t, cache_slc_t, page_table, layer = past
        g_pages = _subproj_pages(cache_cmp_t, page_table, layer, bd)
        new_rows = _pad_t(cmp_new.reshape(B, T, 2 * KVW), CMP_STRIDE).reshape(B * CMP_STRIDE, 2 * KVW)
        g_new = _subproj(new_rows, 0, bd).reshape(B, 1, 4 * LANES)
        G = jnp.concatenate([g_pages, g_new], 1)
    n_cmp_pad = _round_up(n_cmp, LANES)
    gb = G[:, 1:, 256:]
    ga = G[:, :, :256]
    fit = lambda t: _pad_t(t, max(n_cmp_pad, t.shape[1]))[:, :n_cmp_pad]
    hid = fit(ga) + fit(gb) + pe_hid
    kv_cmp = _cmp_mlp(hid, w2bd, g[1])
    n_slc = -(-L_all // SLC_BLOCK)
    nbp = _round_up(n_slc, LANES)
    cover = _cover_matrix(n_cmp, n_cmp_pad, n_slc, nbp)
    tq = min(256, tp)
    o_cmp, imp = _cmp_attn(qn, kv_cmp, cover, pos0, tq)

    R = B * NSA_KV_HEADS * tp
    rp = _round_up(R, LANES)
    imp2d = jnp.pad(imp.reshape(R, nbp), ((0, rp - R), (0, 0)))
    sel, idx_t = _topk(imp2d, pos0, tp, n_slc, want_idx=past is not None)

    if past is None:
        sel4 = sel[:R].reshape(B, NSA_KV_HEADS, tp, nbp)
        o_slc = _attn(q_rot, slc_new.reshape(B, T, 2 * KVW), 'slc', _GQA_PAIRS, LANES, pos0, pos0, tq,
                      min(256, T), sel=sel4)
    else:
        bpp = PAGE_SIZE // SLC_BLOCK
        idx = idx_t[:, :R].T.reshape(B, NSA_KV_HEADS, tp, N_SELECT)[:, :, :T]
        idc = jnp.clip(idx, 0, page_table.shape[1] * bpp - 1)
        page = page_table[jnp.arange(B)[:, None, None, None], idc // bpp]
        new_rows = _pad_t(slc_new.reshape(B, T, 2 * KVW), SLC_BLOCK)
        o_slc = _slc_paged(q_rot, cache_slc_t, new_rows, idx.reshape(-1), page.reshape(-1).astype(jnp.int32),
                           layer, pos0, pos0 // SLC_BLOCK, T)

    if win_prefix.shape[1] == 0:
        win_ctx = win_new
        pos_k0 = pos0
    else:
        win_ctx = jnp.concatenate([win_prefix, win_new], 1)
        pos_k0 = pos0 - win_prefix.shape[1]
    n_k = win_ctx.shape[1]
    tkw = min(256, _round_up(n_k, LANES))
    kv_win = _pad_t(win_ctx.reshape(B, n_k, 2 * KVW), _round_up(n_k, tkw))
    o_win = _attn(q_rot, kv_win, 'win', _GQA_PAIRS, LANES, pos0, pos_k0, tq, tkw)

    keep = win_prefix.shape[1] if past is not None else min(WINDOW, T)
    flat = lambda o: o[:, :T].reshape(B * T, GW)
    return (flat(o_cmp), flat(o_slc), flat(o_win)), cmp_new, slc_new, win_ctx[:, -keep:]


RWKV_PREC = ('r3', 'bf16', 'bf16')


def _layer(x, pos0, st, mem_kv, prm, past, rwkv_prec=RWKV_PREC):
    conv_buf, c0, n0, m0, s0, shift_buf, win_prefix = st
    B, T, _ = x.shape
    x2d = x.reshape(B * T, D_MODEL)
    P2d = _proj_in(x2d, prm['norm_g'], prm['w_in_bf16'], 640)
    P = P2d.reshape(B, T, DP)
    Pp = _pad_t(P, _round_up(T, max(MLSTM_L, RWKV_L)))
    y_m, C, n, m, conv_new = _mlstm_mixer(P, Pp, conv_buf, c0, n0, m0, prm)
    (o_cmp, o_slc, o_win), cmp_new, slc_new, win_new = _nsa_mixer(P2d, Pp, B, T, pos0, win_prefix, prm, past)
    y_r, S, shift_new = _rwkv_mixer(P, Pp, shift_buf, s0, prm, rwkv_prec)
    o_mem = _memory_mixer(Pp, T, mem_kv, prm)
    flat = lambda t: t.reshape(B * T, GW)
    out = _proj_out(x2d, P2d, flat(y_m), flat(y_r), o_cmp, o_slc, o_win, flat(o_mem), prm['gate_expand'],
                    prm['w_out_bf16']).reshape(B, T, D_MODEL)
    return out, (cmp_new, slc_new, win_new, C, n, m, conv_new, S, shift_new)


def _prep_params(l, p):
    prm = {k: v[l] for k, v in p.items()}
    src = jnp.asarray(np.maximum(_SRC, 0), jnp.int32)
    keep = jnp.asarray((_SRC >= 0).astype(np.float32))
    prm['w_in_bf16'] = (jnp.take(prm['w_in'], src, axis=1) * keep).astype(BF16)
    prm['w_out_bf16'] = prm['w_out'].astype(BF16)
    wm = prm['w_mem_kv'].reshape(D_MODEL, 2, MEM_HEADS, HEAD_DIM)
    prm['w_mem_kv_bf16'] = jnp.transpose(wm, (0, 2, 1, 3)).reshape(D_MODEL, 2 * GW).astype(BF16)
    rows = lambda *vs: jnp.concatenate([jnp.pad(v.reshape(-1, v.shape[-1]), ((0, 0), (0, GW - v.shape[-1])))
                                        for v in vs], 0)
    pad16 = lambda blk: jnp.pad(blk, ((0, 16 - blk.shape[0]), (0, 0)))
    cw, cb = prm['mlstm_conv_w'], prm['mlstm_conv_b']
    prm['mlstm_blk'] = pad16(rows(cw[:, :GW], cw[:, GW:], cb[:GW], cb[GW:], prm['mlstm_norm_g'],
                                  prm['mlstm_gate_b'].reshape(1, 2 * N_HEADS)))
    mu = prm['rwkv_mu']
    prm['rwkv_blk'] = pad16(rows(mu[:GW], mu[GW:2 * GW], mu[2 * GW:3 * GW], mu[3 * GW:], prm['rwkv_w0'],
                                 prm['rwkv_a0'], prm['rwkv_kk'], prm['rwkv_ln']))
    g = prm['nsa_qk_g']
    one = jnp.ones((HEAD_DIM,), F32)
    prm['nsa_prep_blk'] = jnp.pad(rows(jnp.tile(g[0], N_HEADS), jnp.concatenate([g[2], one, g[2], one]),
                                       jnp.concatenate([g[3], one, g[3], one])), ((0, 5), (0, 0)))
    head_of = np.arange(GW) // HEAD_DIM
    prm['head_ones'] = jnp.asarray(head_of[:, None] == head_of[None, :], dtype=BF16)
    gate_lane = np.arange(LANES)[:, None]
    out_lane = np.arange(3 * GW)[None, :]
    prm['gate_expand'] = jnp.asarray(gate_lane == (out_lane // GW) * N_HEADS + (out_lane % GW) // HEAD_DIM, dtype=BF16)
    zl = jnp.zeros((DECAY_LORA, GW), F32)
    prm['rwkv_lora'] = jnp.concatenate([jnp.concatenate([prm['rwkv_w2'], zl], 1),
                                        jnp.concatenate([zl, prm['rwkv_a2']], 1)], 0)
    w1 = prm['nsa_cmp_w1']
    eye_h = jnp.eye(NSA_KV_HEADS, dtype=F32)
    eye_c = jnp.eye(2, dtype=F32)
    w1r = w1.reshape(2, 2, CMP_STRIDE, HEAD_DIM, HEAD_DIM)
    bd = jnp.einsum('cC,crsde->scdrCe', eye_c, w1r)
    prm['cmp_bd'] = bd.reshape(CMP_STRIDE // 2, 2 * LANES, 2 * LANES).astype(BF16)
    w2 = prm['nsa_cmp_w2']
    prm['cmp_w2bd'] = jnp.einsum('hH,cC,ced->hceHCd', eye_h, eye_c, w2).reshape(2 * KVW, 2 * KVW)
    pe_hid = jnp.einsum('csd,csde->ce', prm['nsa_pe'], w1, precision=HI)
    prm['cmp_pe_hid'] = jnp.tile(pe_hid.reshape(1, 2 * HEAD_DIM), (1, NSA_KV_HEADS)).reshape(2 * KVW)
    return prm


def kernel(x_prompt, x_sample, cache_cmp_kv, cache_slc_kv, cache_win_kv, cache_mem_kv, state_mlstm_C, state_mlstm_n, state_mlstm_m, state_mlstm_conv, state_rwkv_S, state_rwkv_shift, page_table, mem_prompt, norm_g, w_in, w_out, mlstm_conv_w, mlstm_conv_b, mlstm_gate_b, mlstm_norm_g, nsa_qk_g, nsa_pe, nsa_cmp_w1, nsa_cmp_w2, rwkv_mu, rwkv_w0, rwkv_w2, rwkv_a0, rwkv_a2, rwkv_kk, rwkv_ln, mem_norm_g, w_mem_kv, mem_qk_g):
    params = dict(norm_g=norm_g, w_in=w_in, w_out=w_out, mlstm_conv_w=mlstm_conv_w, mlstm_conv_b=mlstm_conv_b,
                  mlstm_gate_b=mlstm_gate_b, mlstm_norm_g=mlstm_norm_g, nsa_qk_g=nsa_qk_g, nsa_pe=nsa_pe,
                  nsa_cmp_w1=nsa_cmp_w1, nsa_cmp_w2=nsa_cmp_w2, rwkv_mu=rwkv_mu, rwkv_w0=rwkv_w0, rwkv_w2=rwkv_w2,
                  rwkv_a0=rwkv_a0, rwkv_a2=rwkv_a2, rwkv_kk=rwkv_kk, rwkv_ln=rwkv_ln, mem_norm_g=mem_norm_g,
                  w_mem_kv=w_mem_kv, mem_qk_g=mem_qk_g)
    depth = norm_g.shape[0]
    B = x_prompt.shape[0]
    past_len = page_table.shape[1] * PAGE_SIZE
    cache_cmp_t = jnp.transpose(cache_cmp_kv, (0, 1, 3, 4, 5, 2))
    cache_slc_t = jnp.transpose(cache_slc_kv, (0, 1, 3, 4, 5, 2))
    xp, xs = x_prompt, x_sample
    new_p, new_s, new_mem = [], [], []
    for l in range(depth):
        prm = _prep_params(l, params)
        mem_kv_p = _memory_kv(mem_prompt, prm)
        st_p = (jnp.zeros((B, MLSTM_CONV - 1, 2 * GW), F32),
                jnp.zeros((B, N_HEADS, HEAD_DIM, HEAD_DIM), F32),
                jnp.zeros((B, N_HEADS, HEAD_DIM), F32),
                jnp.full((B, N_HEADS), M_INIT, F32),
                jnp.zeros((B, N_HEADS, HEAD_DIM, HEAD_DIM), F32),
                jnp.zeros((B, 1, RWKV_SHIFT), F32),
                jnp.zeros((B, 0, NSA_KV_HEADS, 2, HEAD_DIM), F32))
        xp, sp = _layer(xp, 0, st_p, mem_kv_p, prm, None)
        st_s = (state_mlstm_conv[l], state_mlstm_C[l], state_mlstm_n[l], state_mlstm_m[l],
                state_rwkv_S[l], state_rwkv_shift[l], cache_win_kv[l])
        xs, ss = _layer(xs, past_len, st_s, cache_mem_kv[l], prm, (cache_cmp_t, cache_slc_t, page_table, l))
        new_p.append(sp)
        new_s.append(ss)
        new_mem.append(mem_kv_p)
    stack = lambda states, i: jnp.stack([s[i] for s in states])
    outs = [xp, xs]
    for i in range(3):
        outs += [stack(new_p, i), stack(new_s, i)]
    outs.append(jnp.stack(new_mem))
    for i in range(3, 9):
        outs += [stack(new_p, i), stack(new_s, i)]
    return tuple(outs)
```

```python
import functools

import numpy as np
import jax
import jax.numpy as jnp
from jax import lax
from jax.experimental import pallas as pl
from jax.experimental.pallas import tpu as pltpu

F32 = jnp.float32
BF16 = jnp.bfloat16
HI = lax.Precision.HIGHEST

D_MODEL = 1024
PAGE_SIZE = 128
HEAD_DIM = 64
GW = D_MODEL // 4
N_HEADS = GW // HEAD_DIM
SCALE = HEAD_DIM ** -0.5
MLSTM_CONV = 4
M_INIT = -1e30
NSA_KV_HEADS = 2
KVW = NSA_KV_HEADS * HEAD_DIM
CMP_BLOCK = 32
CMP_STRIDE = 16
SLC_BLOCK = 64
N_SELECT = 16
WINDOW = 512
DECAY_LORA = 64
AAA_LORA = 64
RWKV_SHIFT = 3 * GW + DECAY_LORA + AAA_LORA
RWKV_LN_EPS = HEAD_DIM * 1e-5
N_MEM = 256
MEM_HEADS = 4
ROPE_THETA = 10000.0
NORM_EPS = 1e-6

LANES = 128
MLSTM_L = 128
RWKV_L = 64
NEG_BIG = -1e30

_M0, _N0, _R0, _C0 = 0, 1288, 2580, 3732
C_MQ, C_MK, C_MV, C_MO, C_MZ = 0, 256, 512, 768, 1024
C_NQ, C_NCMP, C_NSLC, C_NWIN, C_NZ = 1280, 1536, 1792, 2048, 2304
C_RR, C_RK, C_RV, C_RZ = 2560, 2816, 3072, 3328
C_CQ, C_CZ = 3584, 3840
C_RL = 4096
C_MG = 4224
C_NG = 4352
DP = 4480


def _packed_src():
    src = -np.ones((DP,), np.int64)

    def put(dst, lo, n):
        src[dst:dst + n] = np.arange(lo, lo + n)

    put(C_MQ, _M0, 256); put(C_MK, _M0 + 256, 256); put(C_MV, _M0 + 512, 256)
    put(C_MG, _M0 + 768, 8); put(C_MO, _M0 + 776, 256); put(C_MZ, _M0 + 1032, 256)
    put(C_NQ, _N0, 256)
    for i, base in enumerate((C_NCMP, C_NSLC, C_NWIN)):
        ksrc = _N0 + 256 + 256 * i
        vsrc = ksrc + 128
        for h in range(2):
            put(base + 128 * h, ksrc + 64 * h, 64)
            put(base + 128 * h + 64, vsrc + 64 * h, 64)
    put(C_NG, _N0 + 1024, 12); put(C_NZ, _N0 + 1036, 256)
    put(C_RR, _R0, 256); put(C_RK, _R0 + 256, 256); put(C_RV, _R0 + 512, 256)
    put(C_RL, _R0 + 768, 128); put(C_RZ, _R0 + 896, 256)
    put(C_CQ, _C0, 256); put(C_CZ, _C0 + 256, 256)
    return src


_SRC = _packed_src()


def _split2(a):
    hi = a.astype(BF16)
    return hi, (a - hi.astype(F32)).astype(BF16)


def _dg(a, b, dims, prec):
    dn = (dims, ((), ()))
    if prec == 'bf16':
        return lax.dot_general(a.astype(BF16), b.astype(BF16), dn, preferred_element_type=F32)
    d = lambda x, y: lax.dot_general(x, y, dn, preferred_element_type=F32)
    if prec == 'x3':
        ah, al = _split2(a)
        bh, bl = _split2(b)
        return d(ah, bh) + (d(ah, bl) + d(al, bh))
    if prec in ('r3', 'l3'):
        exact, other = (a, b) if prec == 'r3' else (b, a)
        o1, rest = other.astype(BF16), None
        rest = other - o1.astype(F32)
        o2 = rest.astype(BF16)
        o3 = (rest - o2.astype(F32)).astype(BF16)
        e = exact.astype(BF16)
        if prec == 'r3':
            return d(e, o1) + (d(e, o2) + d(e, o3))
        return d(o1, e) + (d(o2, e) + d(o3, e))
    return lax.dot_general(a, b, dn, preferred_element_type=F32, precision=prec)


def _mm(a, b, prec=None):
    return _dg(a, b, ((1,), (0,)), prec)


def _mm_nt(a, b, prec=None):
    return _dg(a, b, ((1,), (1,)), prec)


def _mm_tn(a, b, prec=None):
    return _dg(a, b, ((0,), (0,)), prec)


def _run_interleaved(chains):
    chains = list(chains)
    while chains:
        alive = []
        for ch in chains:
            try:
                next(ch)
                alive.append(ch)
            except StopIteration:
                pass
        chains = alive


def _cparams(sem, vmem_mb=None):
    kw = dict(dimension_semantics=sem)
    if vmem_mb is not None:
        kw['vmem_limit_bytes'] = vmem_mb * 1024 * 1024
    return pltpu.CompilerParams(**kw)


def _proj_in_body(x_ref, g_ref, w_ref, o_ref, *, tn):
    x = x_ref[...]
    h = (x * lax.rsqrt(jnp.mean(x * x, -1, keepdims=True) + NORM_EPS) * g_ref[...]).astype(BF16)
    for j in range(o_ref.shape[1] // tn):
        o_ref[:, j * tn:(j + 1) * tn] = jnp.dot(h, w_ref[:, j * tn:(j + 1) * tn], preferred_element_type=F32)


def _proj_in(x2d, g, w_bf16, tn):
    n, d = x2d.shape
    dn = w_bf16.shape[1]
    tm = min(512, n)
    return pl.pallas_call(
        functools.partial(_proj_in_body, tn=tn), out_shape=jax.ShapeDtypeStruct((n, dn), F32), grid=(n // tm,),
        in_specs=[pl.BlockSpec((tm, d), lambda i: (i, 0)),
                  pl.BlockSpec((1, d), lambda i: (0, 0)),
                  pl.BlockSpec((d, dn), lambda i: (0, 0))],
        out_specs=pl.BlockSpec((tm, dn), lambda i: (i, 0)),
        compiler_params=_cparams(("parallel",), 56), name="proj_in")(x2d, g.reshape(1, d), w_bf16)


def _silu(x):
    return x * jax.nn.sigmoid(x)


def _proj_out_body(x_ref, ym_ref, yr_ref, oc_ref, os_ref, ow_ref, om_ref, g_ref, zn_ref, zc_ref, e_ref, w_ref, o_ref):
    ge = _mm(jax.nn.sigmoid(g_ref[...]), e_ref[...], 'l3')
    y_n = (ge[:, :GW] * oc_ref[...] + ge[:, GW:2 * GW] * os_ref[...] + ge[:, 2 * GW:] * ow_ref[...]) * _silu(zn_ref[...])
    y_c = om_ref[...] * _silu(zc_ref[...])
    acc = x_ref[...]
    for i, y in enumerate((ym_ref[...], y_n, yr_ref[...], y_c)):
        acc = acc + jnp.dot(y.astype(BF16), w_ref[i * GW:(i + 1) * GW, :], preferred_element_type=F32)
    o_ref[...] = acc


def _proj_out(x2d, P2d, y_m, y_r, o_cmp, o_slc, o_win, o_mem, gate_expand, w_bf16):
    n, d = x2d.shape
    tm = min(512, n)
    yspec = pl.BlockSpec((tm, GW), lambda i: (i, 0))
    return pl.pallas_call(
        _proj_out_body, out_shape=jax.ShapeDtypeStruct((n, d), F32), grid=(n // tm,),
        in_specs=[pl.BlockSpec((tm, d), lambda i: (i, 0)), yspec, yspec, yspec, yspec, yspec, yspec,
                  pl.BlockSpec((tm, LANES), lambda i: (i, C_NG // LANES)),
                  pl.BlockSpec((tm, GW), lambda i: (i, C_NZ // GW)), pl.BlockSpec((tm, GW), lambda i: (i, C_CZ // GW)),
                  pl.BlockSpec((LANES, 3 * GW), lambda i: (0, 0)), pl.BlockSpec((d, d), lambda i: (0, 0))],
        out_specs=pl.BlockSpec((tm, d), lambda i: (i, 0)),
        compiler_params=_cparams(("parallel",), 48), name="proj_out")(
            x2d, y_m, y_r, o_cmp, o_slc, o_win, o_mem, P2d, P2d, P2d, gate_expand, w_bf16)


_MP_CONV_Q, _MP_CONV_K, _MP_BIAS_Q, _MP_BIAS_K, _MP_NORM, _MP_GATE_B = 0, 4, 8, 9, 10, 11


def _mlstm_body(q_ref, k_ref, v_ref, o_ref, z_ref, g_ref, tq_ref, tk_ref, prm_ref, c0_ref, n0_ref, m0_ref,
                h_ref, c_ref, n_ref, m_ref, c_scr, n_scr, m_scr, pq_scr, pk_scr, *, t_valid):
    c = pl.program_id(1)
    nb, L = q_ref.shape[0], q_ref.shape[1]

    @pl.when(c == 0)
    def _():
        c_scr[...] = c0_ref[...]
        n_scr[...] = n0_ref[...]
        m_scr[...] = m0_ref[...]
        pq_scr[...] = jnp.zeros(pq_scr.shape, F32)
        pk_scr[...] = jnp.zeros(pk_scr.shape, F32)
        pq_scr[:, L - 8:L, :] = tq_ref[...]
        pk_scr[:, L - 8:L, :] = tk_ref[...]

    row = lax.broadcasted_iota(jnp.int32, (L, L), 0)
    col = lax.broadcasted_iota(jnp.int32, (L, L), 1)
    causal = row >= col
    tril = causal.astype(F32)
    triu = (row <= col).astype(F32)
    trow = lax.broadcasted_iota(jnp.int32, (L, 1), 0)
    valid = (c * L + trow) < t_valid
    lane = lax.broadcasted_iota(jnp.int32, (1, LANES), 1)
    prm = prm_ref[...]

    def conv(x, prev, w0, b):
        acc = prm[b:b + 1, :] + prm[w0 + MLSTM_CONV - 1:w0 + MLSTM_CONV, :] * x
        for s in range(1, MLSTM_CONV):
            shifted = jnp.where(trow >= s, pltpu.roll(x, s, 0), pltpu.roll(prev, s, 0))
            acc = acc + prm[w0 + MLSTM_CONV - 1 - s:w0 + MLSTM_CONV - s, :] * shifted
        return _silu(acc)

    cums, qs, ks = [], [], []
    for gi in range(nb):
        q_raw, k_raw = q_ref[gi], k_ref[gi]
        qs.append(conv(q_raw, pq_scr[gi], _MP_CONV_Q, _MP_BIAS_Q) * (HEAD_DIM ** -0.5))
        ks.append(conv(k_raw, pk_scr[gi], _MP_CONV_K, _MP_BIAS_K))
        pq_scr[gi] = q_raw
        pk_scr[gi] = k_raw
        x = g_ref[gi] + prm[_MP_GATE_B:_MP_GATE_B + 1, :LANES]
        log_f = jnp.minimum(x, 0.0) - jnp.log1p(jnp.exp(-jnp.abs(x)))
        g = jnp.where(lane < N_HEADS, x, jnp.where(lane < 2 * N_HEADS, log_f, 0.0))
        g = jnp.where(valid, g, jnp.where(lane < N_HEADS, NEG_BIG, 0.0))
        gt = g.T
        cums.append((g, gt, _mm(tril, g, 'r3'), _mm(gt, triu, 'l3')))

    def chain(gi, h):
        g, gt, bc, br = cums[gi]
        sl = slice(h * HEAD_DIM, (h + 1) * HEAD_DIM)
        qh = qs[gi][:, sl]
        kh = ks[gi][:, sl]
        vh = v_ref[gi, :, sl]
        qk = _mm_nt(qh, kh)
        ch = c_scr[gi, h]
        qc = _mm(qh, ch)
        yield
        b_col = bc[:, 4 + h:5 + h]
        li_col = g[:, h:h + 1]
        b_row = br[4 + h:5 + h, :]
        li_row = gt[h:h + 1, :]
        m_prev = m_scr[gi, :, h:h + 1]
        log_d = jnp.where(causal, b_col - b_row + li_row, -jnp.inf)
        log_inter = b_col + m_prev
        m_t = jnp.maximum(jnp.max(log_d, -1, keepdims=True), log_inter)
        s = qk * jnp.exp(log_d - m_t)
        w_inter = jnp.exp(log_inter - m_t)
        nh = n_scr[gi, h:h + 1, :]
        sv = _mm(s, vh)
        yield
        num = sv + w_inter * qc
        den = jnp.sum(s, -1, keepdims=True) + w_inter * jnp.sum(qh * nh, -1, keepdims=True)
        hh = num / jnp.maximum(jnp.abs(den), jnp.exp(-m_t))
        hn = hh * lax.rsqrt(jnp.mean(hh * hh, -1, keepdims=True) + NORM_EPS) * prm[_MP_NORM:_MP_NORM + 1, sl]
        h_ref[gi, :, sl] = jax.nn.sigmoid(o_ref[gi, :, sl]) * hn * _silu(z_ref[gi, :, sl])
        b_end = b_col[L - 1:L, :]
        log_w = b_end - b_col + li_col
        m_new = jnp.maximum(b_end + m_prev, jnp.max(log_w, 0, keepdims=True))
        wk = jnp.exp(log_w - m_new)
        decay = jnp.exp(b_end + m_prev - m_new)
        kw = kh * wk
        c_scr[gi, h] = decay * ch + _mm_tn(kw, vh)
        n_scr[gi, h:h + 1, :] = decay * nh + jnp.sum(kw, 0, keepdims=True)
        m_scr[gi, :, h:h + 1] = m_new

    _run_interleaved([chain(gi, h) for gi in range(nb) for h in range(N_HEADS)])

    @pl.when(c == pl.num_programs(1) - 1)
    def _():
        c_ref[...] = c_scr[...]
        n_ref[...] = n_scr[...]
        m_ref[...] = m_scr[...]


BATCH_ROWS = 4


def _mlstm(P, t_valid, tail_q, tail_k, prm_blk, c0, n0, m0):
    B, tp, _ = P.shape
    L = MLSTM_L
    nb = BATCH_ROWS if B % BATCH_ROWS == 0 else 1
    col = lambda c0_: pl.BlockSpec((nb, L, GW), lambda b, c: (b, c, c0_ // GW))
    tok = pl.BlockSpec((nb, L, GW), lambda b, c: (b, c, 0))
    tail = pl.BlockSpec((nb, 8, GW), lambda b, c: (b, 0, 0))
    sc = pl.BlockSpec((nb, N_HEADS, HEAD_DIM, HEAD_DIM), lambda b, c: (b, 0, 0, 0))
    sn = pl.BlockSpec((nb, N_HEADS, HEAD_DIM), lambda b, c: (b, 0, 0))
    sm = pl.BlockSpec((nb, 1, LANES), lambda b, c: (b, 0, 0))
    return pl.pallas_call(
        functools.partial(_mlstm_body, t_valid=t_valid),
        out_shape=(jax.ShapeDtypeStruct((B, tp, GW), F32), jax.ShapeDtypeStruct(c0.shape, F32),
                   jax.ShapeDtypeStruct(n0.shape, F32), jax.ShapeDtypeStruct(m0.shape, F32)),
        grid=(B // nb, -(-t_valid // L)),
        in_specs=[col(C_MQ), col(C_MK), col(C_MV), col(C_MO), col(C_MZ),
                  pl.BlockSpec((nb, L, LANES), lambda b, c: (b, c, C_MG // LANES)), tail, tail,
                  pl.BlockSpec((16, GW), lambda b, c: (0, 0)), sc, sn, sm],
        out_specs=(tok, sc, sn, sm),
        scratch_shapes=[pltpu.VMEM((nb, N_HEADS, HEAD_DIM, HEAD_DIM), F32), pltpu.VMEM((nb, N_HEADS, HEAD_DIM), F32),
                        pltpu.VMEM((nb, 1, LANES), F32), pltpu.VMEM((nb, L, GW), F32), pltpu.VMEM((nb, L, GW), F32)],
        compiler_params=_cparams(("parallel", "arbitrary")), name="mlstm")(
            P, P, P, P, P, P, tail_q, tail_k, prm_blk, c0, n0, m0)


(_RP_MU_R, _RP_MU_K, _RP_MU_V, _RP_MU_L, _RP_W0, _RP_A0, _RP_KK, _RP_KA, _RP_RK, _RP_LN_G, _RP_LN_B) = range(11)


def _rwkv_body(r_ref, k_ref, v_ref, z_ref, l_ref, sr_ref, sk_ref, sv_ref, sl_ref, prm_ref, lora_ref, s0_ref,
               y_ref, s_ref, s_scr, cr_scr, ck_scr, cv_scr, cl_scr, *, prec, t_valid):
    c_id = pl.program_id(1)

    @pl.when(c_id == 0)
    def _():
        s_scr[...] = s0_ref[...]
        cr_scr[...] = sr_ref[...]
        ck_scr[...] = sk_ref[...]
        cv_scr[...] = sv_ref[...]
        cl_scr[...] = sl_ref[...]

    nb, L = r_ref.shape[0], r_ref.shape[1]
    D = HEAD_DIM
    row = lax.broadcasted_iota(jnp.int32, (L, L), 0)
    col = lax.broadcasted_iota(jnp.int32, (L, L), 1)
    lower = row >= col
    strict = row > col
    tril = lower.astype(F32)
    n_sq = int(np.log2(L)) - 1
    pc, pa, prec = prec
    trow = lax.broadcasted_iota(jnp.int32, (L, 1), 0)
    valid = (c_id * L + trow) < t_valid
    prm = prm_ref[...]
    prow = lambda i, n=GW: prm[i:i + 1, :n]

    def shifted_mix(x_ref, carry_scr, g, mu):
        x = x_ref[g]
        prev = jnp.where(trow >= 1, pltpu.roll(x, 1, 0), carry_scr[g])
        carry_scr[g] = x[L - 1:L, :]
        return x + (prev - x) * mu

    prep = []
    for g in range(nb):
        r = shifted_mix(r_ref, cr_scr, g, prow(_RP_MU_R))
        k = shifted_mix(k_ref, ck_scr, g, prow(_RP_MU_K))
        v = shifted_mix(v_ref, cv_scr, g, prow(_RP_MU_V))
        lo = shifted_mix(l_ref, cl_scr, g, prow(_RP_MU_L, LANES))
        lo_in = jnp.where(lax.broadcasted_iota(jnp.int32, (1, LANES), 1) < DECAY_LORA, jnp.tanh(lo), lo)
        lora = _mm(lo_in, lora_ref[...])
        w_pre = prow(_RP_W0) + lora[:, :GW]
        log_w = -(float(np.exp(-0.5)) * jax.nn.sigmoid(w_pre))
        a = jax.nn.sigmoid(prow(_RP_A0) + lora[:, GW:])
        k_eff = k * (1.0 + (a - 1.0) * prow(_RP_KA))
        kk_raw = k * prow(_RP_KK)
        log_w = jnp.where(valid, log_w, 0.0)
        k_eff = jnp.where(valid, k_eff, 0.0)
        kk_raw = jnp.where(valid, kk_raw, 0.0)
        v = jnp.where(valid, v, 0.0)
        prep.append((r, log_w, k_eff, v, kk_raw, a))

    def chain(g, h):
        sl = slice(h * D, (h + 1) * D)
        r, w, k, v, kk, a = (x[:, sl] for x in prep[g])
        kk = kk / jnp.maximum(jnp.sqrt(jnp.sum(kk * kk, -1, keepdims=True)), 1e-12)
        cum = _mm(tril, w, pc)
        yield
        c_last = cum[L - 1:L, :]
        e_neg = jnp.exp(-cum)
        kh = kk * jnp.exp(cum - w)
        bt = kk * a * e_neg
        kt = k * e_neg
        rh = r * jnp.exp(cum)
        gram = _mm_nt(jnp.concatenate([kh, rh], 0), jnp.concatenate([bt, kt], 0), pa)
        yield
        A = jnp.where(strict, gram[:L, :L], 0.0)
        Bm = jnp.where(strict, gram[:L, L:], 0.0)
        Mb = jnp.where(lower, gram[L:, :L], 0.0)
        Mk = jnp.where(lower, gram[L:, L:], 0.0)
        X = jnp.concatenate([kh, _mm(Bm, v, pa)], 1)
        yield
        Pw = A
        X = X - _mm(Pw, X, pa)
        yield
        for _ in range(n_sq):
            Pw = _mm(Pw, Pw, pa)
            yield
            X = X + _mm(Pw, X, pa)
            yield
        e_end = jnp.exp(c_last - cum)
        bp = kk * a * e_end
        kp = k * e_end
        xtb = _mm_tn(X, bp, prec)
        yield
        wtb = xtb[:D]
        N = _mm_tn(v, kp, prec) - xtb[D:]
        yield
        mbx = _mm(Mb, X, prec)
        yield
        qp = rh - mbx[:, :D]
        y0 = _mm(Mk, v, prec) - mbx[:, D:]
        yield
        s0 = s_scr[g, h]
        y = _mm_nt(qp, s0, prec) + y0
        yield
        s_scr[g, h] = s0 * jnp.exp(c_last) - _mm(s0, wtb, prec) + N
        mu = jnp.mean(y, -1, keepdims=True)
        var = jnp.mean(jnp.square(y - mu), -1, keepdims=True)
        y = (y - mu) * lax.rsqrt(var + RWKV_LN_EPS) * prm[_RP_LN_G:_RP_LN_G + 1, sl] + prm[_RP_LN_B:_RP_LN_B + 1, sl]
        y = y + jnp.sum(r * k * prm[_RP_RK:_RP_RK + 1, sl], -1, keepdims=True) * v
        y_ref[g, :, sl] = y * _silu(z_ref[g, :, sl])

    _run_interleaved([chain(g, h) for g in range(nb) for h in range(N_HEADS)])

    @pl.when(c_id == pl.num_programs(1) - 1)
    def _():
        s_ref[...] = s_scr[...]


def _rwkv(P, t_valid, shift_buf, prm_blk, lora_w, s0, prec):
    B, tp, _ = P.shape
    L = RWKV_L
    nb = BATCH_ROWS if B % BATCH_ROWS == 0 else 1
    col = lambda c0_: pl.BlockSpec((nb, L, GW), lambda b, c: (b, c, c0_ // GW))
    tok = pl.BlockSpec((nb, L, GW), lambda b, c: (b, c, 0))
    car = lambda w: pl.BlockSpec((nb, 1, w), lambda b, c: (b, 0, 0))
    st = pl.BlockSpec((nb, N_HEADS, HEAD_DIM, HEAD_DIM), lambda b, c: (b, 0, 0, 0))
    shifts = [shift_buf[..., i * GW:(i + 1) * GW] for i in range(3)] + [shift_buf[..., 3 * GW:]]
    return pl.pallas_call(
        functools.partial(_rwkv_body, prec=prec, t_valid=t_valid),
        out_shape=(jax.ShapeDtypeStruct((B, tp, GW), F32), jax.ShapeDtypeStruct(s0.shape, F32)),
        grid=(B // nb, -(-t_valid // L)),
        in_specs=[col(C_RR), col(C_RK), col(C_RV), col(C_RZ),
                  pl.BlockSpec((nb, L, LANES), lambda b, c: (b, c, C_RL // LANES)),
                  car(GW), car(GW), car(GW), car(LANES),
                  pl.BlockSpec((16, GW), lambda b, c: (0, 0)), pl.BlockSpec((LANES, 2 * GW), lambda b, c: (0, 0)), st],
        out_specs=(tok, st),
        scratch_shapes=[pltpu.VMEM((nb, N_HEADS, HEAD_DIM, HEAD_DIM), F32), pltpu.VMEM((nb, 1, GW), F32),
                        pltpu.VMEM((nb, 1, GW), F32), pltpu.VMEM((nb, 1, GW), F32), pltpu.VMEM((nb, 1, LANES), F32)],
        compiler_params=_cparams(("parallel", "arbitrary")), name="rwkv")(
            P, P, P, P, P, *shifts, prm_blk, lora_w, s0)


def _subproj_accumulate(load_rows, w_ref, o_ref):
    n = o_ref.shape[-2]
    accs = []
    for h in range(NSA_KV_HEADS):
        acc = jnp.zeros((n, 2 * LANES), F32)
        for s in range(0, CMP_STRIDE, 2):
            xs = jnp.concatenate([load_rows(h, s, n), load_rows(h, s + 1, n)], 1)
            acc = acc + jnp.dot(xs.astype(BF16), w_ref[s // 2], preferred_element_type=F32)
        accs.append(acc)
    out = jnp.concatenate([accs[0][:, :LANES], accs[1][:, :LANES], accs[0][:, LANES:], accs[1][:, LANES:]], 1)
    o_ref[...] = out.reshape(o_ref.shape)


def _subproj_body(x0_ref, x1_ref, w_ref, o_ref):
    xs = (x0_ref, x1_ref)
    _subproj_accumulate(lambda h, s, n: xs[h][pl.ds(s, n, stride=CMP_STRIDE), :], w_ref, o_ref)


def _subproj(rows2d, col0, w):
    n = rows2d.shape[0]
    tm = min(2048, n)
    return pl.pallas_call(
        _subproj_body, out_shape=jax.ShapeDtypeStruct((n // CMP_STRIDE, 4 * LANES), F32), grid=(n // tm,),
        in_specs=[pl.BlockSpec((tm, LANES), lambda i: (i, col0)), pl.BlockSpec((tm, LANES), lambda i: (i, col0 + 1)),
                  pl.BlockSpec((CMP_STRIDE // 2, 2 * LANES, 2 * LANES), lambda i: (0, 0, 0))],
        out_specs=pl.BlockSpec((tm // CMP_STRIDE, 4 * LANES), lambda i: (i, 0)),
        compiler_params=_cparams(("parallel",)), name="cmp_subproj")(rows2d, rows2d, w)


_PAGES_PER_STEP = 32


def _subproj_pages_body(pt_ref, *refs):
    del pt_ref
    npg = len(refs) - 3
    w_ref, o_ref, rows_scr = refs[npg], refs[npg + 1], refs[npg + 2]
    n_half = 2 if npg % 2 == 0 else 1
    pph = npg // n_half
    spp = PAGE_SIZE // CMP_STRIDE
    for half in range(n_half):
        for p in range(half * pph, (half + 1) * pph):
            for h in range(NSA_KV_HEADS):
                rows_scr[h, p * PAGE_SIZE:(p + 1) * PAGE_SIZE, :] = refs[p][0, 0, h].reshape(2 * HEAD_DIM, PAGE_SIZE).T
        row0 = half * pph * PAGE_SIZE
        _subproj_accumulate(lambda h, s, n: rows_scr[h, pl.ds(row0 + s, n, stride=CMP_STRIDE), :], w_ref,
                            o_ref.at[:, half * pph * spp:(half + 1) * pph * spp, :])


def _subproj_pages(cache_t, page_table, layer, w):
    B, n_pages = page_table.shape
    npg = min(_PAGES_PER_STEP, n_pages)
    spp = PAGE_SIZE // CMP_STRIDE

    def page_spec(p):
        return pl.BlockSpec((1, 1, NSA_KV_HEADS, 2, HEAD_DIM, PAGE_SIZE),
                            lambda b, g, pt: (layer, pt[b, g * npg + p], 0, 0, 0, 0))

    gs = pltpu.PrefetchScalarGridSpec(
        num_scalar_prefetch=1, grid=(B, n_pages // npg),
        in_specs=[page_spec(p) for p in range(npg)]
        + [pl.BlockSpec((CMP_STRIDE // 2, 2 * LANES, 2 * LANES), lambda b, g, pt: (0, 0, 0))],
        out_specs=pl.BlockSpec((1, npg * spp, 4 * LANES), lambda b, g, pt: (b, g, 0)),
        scratch_shapes=[pltpu.VMEM((NSA_KV_HEADS, npg * PAGE_SIZE, LANES), F32)])
    return pl.pallas_call(
        _subproj_pages_body, out_shape=jax.ShapeDtypeStruct((B, n_pages * spp, 4 * LANES), F32), grid_spec=gs,
        compiler_params=_cparams(("parallel", "arbitrary"), 48), name="cmp_subproj_pages")(
            page_table, *([cache_t] * npg), w)


def _cmp_mlp_body(h_ref, w2_ref, g_ref, o_ref):
    x = h_ref[0]
    kv = _mm(jax.nn.gelu(x), w2_ref[...])
    g = g_ref[...]
    segs = []
    for j in range(4):
        seg = kv[:, j * HEAD_DIM:(j + 1) * HEAD_DIM]
        if j % 2 == 0:
            seg = seg * lax.rsqrt(jnp.mean(seg * seg, -1, keepdims=True) + NORM_EPS) * g
        segs.append(seg)
    o_ref[0] = jnp.concatenate(segs, 1)


def _cmp_mlp(hid, w2bd, g):
    B, n, _ = hid.shape
    tn = min(512, n)
    return pl.pallas_call(
        _cmp_mlp_body, out_shape=jax.ShapeDtypeStruct((B, n, 256), F32), grid=(B, n // tn),
        in_specs=[pl.BlockSpec((1, tn, 256), lambda b, i: (b, i, 0)), pl.BlockSpec((256, 256), lambda b, i: (0, 0)),
                  pl.BlockSpec((1, HEAD_DIM), lambda b, i: (0, 0))],
        out_specs=pl.BlockSpec((1, tn, 256), lambda b, i: (b, i, 0)),
        compiler_params=_cparams(("parallel", "parallel")), name="cmp_mlp")(hid, w2bd, g.reshape(1, HEAD_DIM))


def _cmp_attn_body(q_ref, kv_ref, cov_ref, o_ref, imp_ref, *, pos0):
    qi = pl.program_id(2)
    tq = q_ref.shape[1]
    n = kv_ref.shape[1]
    kv = kv_ref[0]
    k = kv[:, :HEAD_DIM]
    v = kv[:, HEAD_DIM:]
    q = q_ref[0]
    pos = pos0 + qi * tq + lax.broadcasted_iota(jnp.int32, (tq, 1), 0)
    end = lax.broadcasted_iota(jnp.int32, (1, n), 1) * CMP_STRIDE + (CMP_BLOCK - 1)
    mask = end <= pos
    psum = jnp.zeros((tq, n), F32)
    outs = []
    for g in range(2):
        s = _mm_nt(q[:, g * HEAD_DIM:(g + 1) * HEAD_DIM], k) * SCALE
        s = jnp.where(mask, s, -jnp.inf)
        m = jnp.max(s, -1, keepdims=True)
        e = jnp.exp(s - jnp.where(m == -jnp.inf, 0.0, m))
        p = e / jnp.maximum(jnp.sum(e, -1, keepdims=True), 1e-30)
        outs.append(_mm(p, v))
        psum = psum + p
    o_ref[0] = jnp.concatenate(outs, 1)
    hi = psum.astype(BF16)
    lo = (psum - hi.astype(F32)).astype(BF16)
    cov = cov_ref[...]
    imp_ref[0, 0] = _mm(hi, cov) + _mm(lo, cov)


def _cmp_attn(qn, kv_cmp, cover, pos0, tq):
    B, T, _ = qn.shape
    n = kv_cmp.shape[1]
    nbp = cover.shape[1]
    return pl.pallas_call(
        functools.partial(_cmp_attn_body, pos0=pos0),
        out_shape=(jax.ShapeDtypeStruct((B, T, GW), F32), jax.ShapeDtypeStruct((B, NSA_KV_HEADS, T, nbp), F32)),
        grid=(B, NSA_KV_HEADS, T // tq),
        in_specs=[pl.BlockSpec((1, tq, LANES), lambda b, h, i: (b, i, h)),
                  pl.BlockSpec((1, n, LANES), lambda b, h, i: (b, 0, h)),
                  pl.BlockSpec((n, nbp), lambda b, h, i: (0, 0))],
        out_specs=(pl.BlockSpec((1, tq, LANES), lambda b, h, i: (b, i, h)),
                   pl.BlockSpec((1, 1, tq, nbp), lambda b, h, i: (b, h, i, 0))),
        compiler_params=_cparams(("parallel", "parallel", "arbitrary")), name="cmp_attn")(qn, kv_cmp, cover)


def _topk_body(imp_ref, sel_ref, idx_ref, cnt_scr, *, pos0, t_rows, n_blk, want_idx):
    for sub in range(imp_ref.shape[0] // LANES):
        rows = slice(sub * LANES, (sub + 1) * LANES)
        _topk_tile(imp_ref.at[rows, :], sel_ref.at[rows, :], idx_ref.at[:, rows], cnt_scr,
                   pl.program_id(0) * (imp_ref.shape[0] // LANES) + sub,
                   pos0=pos0, t_rows=t_rows, n_blk=n_blk, want_idx=want_idx)


def _topk_tile(imp_ref, sel_ref, idx_ref, cnt_scr, ti, *, pos0, t_rows, n_blk, want_idx):
    x = imp_ref[...]
    nbp = x.shape[1]
    nbu = cnt_scr.shape[0]
    xt = jnp.concatenate([x[:, j * LANES:(j + 1) * LANES].T for j in range(nbp // LANES)], 0)[:nbu]
    r = ti * LANES + lax.broadcasted_iota(jnp.int32, (1, LANES), 1)
    cur = (pos0 + r % t_rows) // SLC_BLOCK
    t_last = (ti * LANES) % t_rows + LANES - 1 if t_rows % LANES == 0 else t_rows - 1
    cur_max = (pos0 + t_last) // SLC_BLOCK
    blk = lax.broadcasted_iota(jnp.int32, (nbu, 1), 0)
    forced = (blk == 0) | (blk == cur) | (blk == cur - 1)
    val = jnp.where(forced, jnp.inf, jnp.where(blk <= cur, xt, -jnp.inf))
    val = jnp.where(blk < n_blk, val, -jnp.inf)
    cnt_scr[...] = jnp.zeros(cnt_scr.shape, F32)
    for g0 in range(0, n_blk, 8):
        @pl.when(g0 <= cur_max)
        def _():
            cnt = cnt_scr[...]
            for i in range(g0, min(g0 + 8, n_blk)):
                vi = val[i:i + 1, :]
                ahead = (vi > val) | ((vi == val) & (blk > i))
                cnt = cnt + jnp.where(ahead, 1.0, 0.0)
            cnt_scr[...] = cnt
    cnt = cnt_scr[...]
    chosen = (cnt < float(N_SELECT)) & (val > -jnp.inf)
    self32 = jnp.where(chosen, 1.0, 0.0)
    if nbu < nbp:
        self32 = jnp.concatenate([self32, jnp.zeros((nbp - nbu, LANES), F32)], 0)
    sel_ref[...] = jnp.concatenate([self32[j * LANES:(j + 1) * LANES, :].T for j in range(nbp // LANES)], 1)
    if want_idx:
        blk_f = blk.astype(F32)
        rows = []
        for j in range(N_SELECT):
            hit = chosen & (cnt == float(j))
            rows.append(jnp.sum(jnp.where(hit, blk_f + 1.0, 0.0), 0, keepdims=True) - 1.0)
        idx_ref[...] = jnp.concatenate(rows, 0).astype(jnp.int32)
    else:
        idx_ref[...] = jnp.zeros(idx_ref.shape, jnp.int32)


def _topk(imp2d, pos0, t_rows, n_blk, want_idx):
    R, nbp = imp2d.shape
    tr = 4 * LANES if R % (4 * LANES) == 0 else LANES
    return pl.pallas_call(
        functools.partial(_topk_body, pos0=pos0, t_rows=t_rows, n_blk=n_blk, want_idx=want_idx),
        out_shape=(jax.ShapeDtypeStruct((R, nbp), F32), jax.ShapeDtypeStruct((N_SELECT, R), jnp.int32)),
        scratch_shapes=[pltpu.VMEM((_round_up(n_blk, 8), LANES), F32)],
        grid=(R // tr,),
        in_specs=[pl.BlockSpec((tr, nbp), lambda i: (i, 0))],
        out_specs=(pl.BlockSpec((tr, nbp), lambda i: (i, 0)), pl.BlockSpec((N_SELECT, tr), lambda i: (0, i))),
        compiler_params=_cparams(("parallel",)), name="topk")(imp2d)


def _attn_body(*refs, mode, pairs, tk, pos_q0, pos_k0, scale):
    if mode == 'slc':
        q_ref, kv_ref, sel_ref, blk_ref, o_ref = refs
    elif mode == 'none':
        q_ref, kv_ref, g_ref, ones_ref, o_ref = refs
    else:
        q_ref, kv_ref, o_ref = refs
    qi = pl.program_id(2)
    tq = q_ref.shape[1]
    n_k = kv_ref.shape[1]
    n_tiles = n_k // tk
    q = q_ref[0]
    if mode == 'none':
        ms = _mm(q * q, ones_ref[...], 'l3') * (1.0 / HEAD_DIM)
        q = q * lax.rsqrt(ms + NORM_EPS) * g_ref[...]
    pq0 = pos_q0 + qi * tq
    qpos = pq0 + lax.broadcasted_iota(jnp.int32, (tq, 1), 0)
    unroll = (4 if mode == 'slc' and n_tiles % 4 == 0 else 2) if n_tiles % 2 == 0 else 1
    if mode == 'none':
        lo, hi = 0, n_tiles // unroll
    else:
        r_hi = jnp.minimum(n_k - 1, pq0 + tq - 1 - pos_k0)
        hi = r_hi // (tk * unroll) + 1
        lo = jnp.maximum(0, pq0 - (WINDOW - 1) - pos_k0) // (tk * unroll) if mode == 'win' else 0
    qs = [q[:, q_lo:q_lo + HEAD_DIM] * scale for q_lo, _, _ in pairs]
    if mode == 'slc':
        nbp = sel_ref.shape[3]
        sel_bias = ((sel_ref[0, 0] - 1.0) * (-NEG_BIG)).astype(BF16)
        qs = [jnp.concatenate([sel_bias, qh.astype(BF16)], 1) for qh in qs]

    def step(it, carry, causal=True):
        tiles = []
        for u in range(unroll):
            r0 = pl.multiple_of((it * unroll + u) * tk, tk)
            kvt = kv_ref[0, pl.ds(r0, tk), :]
            kpos = pos_k0 + r0 + lax.broadcasted_iota(jnp.int32, (1, tk), 1)
            onehot = None
            if mode == 'win':
                d = qpos - kpos
                mask = (d >= 0) & (d < WINDOW) & (kpos >= 0)
            elif mode == 'slc':
                onehot = blk_ref[pl.ds(r0, tk), :]
                mask = (kpos <= qpos) if causal else None
            else:
                mask = None
            tiles.append((kvt, mask, onehot))
        new = [None] * (3 * len(pairs))

        def chain(p):
            _, k_lo, v_lo = pairs[p]
            m, l, acc = carry[3 * p:3 * p + 3]
            if mode == 'slc':
                ss = [_mm_nt(qs[p], jnp.concatenate([oh, kvt[:, k_lo:k_lo + HEAD_DIM].astype(BF16)], 1))
                      for kvt, _, oh in tiles]
            else:
                ss = [_mm_nt(qs[p], kvt[:, k_lo:k_lo + HEAD_DIM]) for kvt, _, _ in tiles]
            yield
            ss = [s if mask is None else jnp.where(mask, s, -jnp.inf) for s, (_, mask, _) in zip(ss, tiles)]
            m_new = m
            for s in ss:
                m_new = jnp.maximum(m_new, jnp.max(s, -1, keepdims=True))
            alpha = jnp.exp(m - m_new)
            m_wide = jnp.concatenate([m_new] * (tk // LANES), 1)
            prs = [jnp.exp(s - m_wide) for s in ss]
            yield
            l = alpha * l
            acc = alpha[:, :HEAD_DIM] * acc
            for pr, (kvt, _, _) in zip(prs, tiles):
                l = l + jnp.sum(pr, -1, keepdims=True)
                acc = acc + _mm(pr, kvt[:, v_lo:v_lo + HEAD_DIM])
            new[3 * p:3 * p + 3] = [m_new, l, acc]

        _run_interleaved([chain(p) for p in range(len(pairs))])
        return tuple(new)

    init = []
    for _ in pairs:
        init += [jnp.full((tq, LANES), NEG_BIG, F32), jnp.zeros((tq, LANES), F32), jnp.zeros((tq, HEAD_DIM), F32)]
    if mode == 'slc':
        hi_full = jnp.minimum(hi, (pq0 - pos_k0 + 1) // (tk * unroll))
        res = lax.fori_loop(lo, hi_full, functools.partial(step, causal=False), tuple(init))
        res = lax.fori_loop(hi_full, hi, step, res)
    else:
        res = lax.fori_loop(lo, hi, step, tuple(init))
    outs = []
    for p in range(len(pairs)):
        m, l, acc = res[3 * p:3 * p + 3]
        outs.append(jnp.where(m[:, :HEAD_DIM] > 0.5 * NEG_BIG, acc / jnp.maximum(l[:, :HEAD_DIM], 1e-30), 0.0))
    o_ref[0] = jnp.concatenate(outs, 1)


def _attn(q, kv, mode, pairs, kv_width, pos_q0, pos_k0, tq, tk, sel=None, q_col0=0, q_gain=None, n_rows=None):
    B = q.shape[0]
    T = q.shape[1] if n_rows is None else n_rows
    n_k = kv.shape[1]
    in_specs = [pl.BlockSpec((1, tq, LANES), lambda b, h, i: (b, i, q_col0 + h)),
                pl.BlockSpec((1, n_k, kv_width), lambda b, h, i: (b, 0, h))]
    args = [q, kv]
    if mode == 'none':
        head_of = np.arange(LANES) // HEAD_DIM
        in_specs += [pl.BlockSpec((1, LANES), lambda b, h, i: (0, 0)), pl.BlockSpec((LANES, LANES), lambda b, h, i: (0, 0))]
        args += [q_gain.reshape(1, LANES), jnp.asarray(head_of[:, None] == head_of[None, :], dtype=BF16)]
    if mode == 'slc':
        nbp = sel.shape[-1]
        in_specs.append(pl.BlockSpec((1, 1, tq, nbp), lambda b, h, i: (b, h, i, 0)))
        args.append(sel)
        key_blk = (pos_k0 + np.arange(n_k)) // SLC_BLOCK
        in_specs.append(pl.BlockSpec((n_k, nbp), lambda b, h, i: (0, 0)))
        args.append(jnp.asarray(key_blk[:, None] == np.arange(nbp)[None, :], dtype=BF16))
    return pl.pallas_call(
        functools.partial(_attn_body, mode=mode, pairs=pairs, tk=tk, pos_q0=pos_q0, pos_k0=pos_k0, scale=SCALE),
        out_shape=jax.ShapeDtypeStruct((B, T, GW), F32), grid=(B, 2, T // tq),
        in_specs=in_specs, out_specs=pl.BlockSpec((1, tq, LANES), lambda b, h, i: (b, i, h)),
        compiler_params=_cparams(("parallel", "parallel", "arbitrary")), name="attn_" + mode)(*args)


_GQA_PAIRS = ((0, 0, HEAD_DIM), (HEAD_DIM, 0, HEAD_DIM))
_MHA_PAIRS = ((0, 0, HEAD_DIM), (HEAD_DIM, 2 * HEAD_DIM, 3 * HEAD_DIM))


def _slc_paged_body(idx_ref, phys_ref, *refs, pos0, blk0, t_real):
    del phys_ref
    n_slots = t_real * N_SELECT
    q_ref = refs[0]
    blk_refs = refs[1:1 + n_slots]
    new_ref = refs[1 + n_slots]
    o_ref = refs[2 + n_slots]
    b, h = pl.program_id(0), pl.program_id(1)
    bpp = PAGE_SIZE // SLC_BLOCK
    tok = lax.broadcasted_iota(jnp.int32, (1, PAGE_SIZE), 1)
    lane = lax.broadcasted_iota(jnp.int32, (1, SLC_BLOCK), 1)
    newblk = new_ref[0]

    def chain(t):
        base = ((b * NSA_KV_HEADS + h) * t_real + t) * N_SELECT
        qrow = q_ref[0, t:t + 1, :]
        q2 = jnp.concatenate([qrow[:, :HEAD_DIM], qrow[:, HEAD_DIM:], jnp.zeros((6, HEAD_DIM), F32)], 0) * SCALE
        pos = pos0 + t
        scores, vts = [], []
        n_new = jnp.int32(0)
        for j in range(N_SELECT):
            idx = idx_ref[base + j]
            idc = jnp.maximum(idx, 0)
            kv_t = blk_refs[t * N_SELECT + j][0, 0, 0]
            ok = ((idx >= 0) & (idx < blk0) & (tok // SLC_BLOCK == idc % bpp)
                  & ((idc // bpp) * PAGE_SIZE + tok <= pos))
            scores.append(_mm(q2, kv_t[0]) + jnp.where(ok, 0.0, -jnp.inf))
            vts.append(kv_t[1])
            n_new = n_new + jnp.where(idx >= blk0, 1, 0)
        ok_new = (n_new > 0) & (blk0 * SLC_BLOCK + lane <= pos)
        s_new = _mm_nt(q2, newblk[:, :HEAD_DIM]) + jnp.where(ok_new, 0.0, -jnp.inf)
        yield
        m = jnp.max(s_new, -1, keepdims=True)
        for s in scores:
            m = jnp.maximum(m, jnp.max(s, -1, keepdims=True))
        m = jnp.where(m == -jnp.inf, 0.0, m)
        e_new = jnp.exp(s_new - m)
        den = jnp.sum(e_new, -1, keepdims=True)
        es = [jnp.exp(s - m) for s in scores]
        yield
        o = _mm(e_new, newblk[:, HEAD_DIM:])
        for e, vt in zip(es, vts):
            den = den + jnp.sum(e, -1, keepdims=True)
            o = o + _mm_nt(e, vt)
        o = o / jnp.maximum(den, 1e-30)
        orow = jnp.concatenate([o[0:1], o[1:2]], 1)
        o_ref[0, 0, t] = jnp.broadcast_to(orow, (8, LANES))

    _run_interleaved([chain(t) for t in range(t_real)])


def _slc_paged(q_rot, cache_t, new_rows, idx_flat, page_flat, layer, pos0, blk0, t_real):
    B = q_rot.shape[0]
    tp = q_rot.shape[1]
    n_slots = t_real * N_SELECT

    def blk_spec(s):
        def imap(b, h, idx, page):
            return (layer, page[(b * NSA_KV_HEADS + h) * n_slots + s], h, 0, 0, 0)
        return pl.BlockSpec((1, 1, 1, 2, HEAD_DIM, PAGE_SIZE), imap)

    gs = pltpu.PrefetchScalarGridSpec(
        num_scalar_prefetch=2, grid=(B, NSA_KV_HEADS),
        in_specs=[pl.BlockSpec((1, tp, LANES), lambda b, h, idx, page: (b, 0, h))]
        + [blk_spec(s) for s in range(n_slots)]
        + [pl.BlockSpec((1, SLC_BLOCK, LANES), lambda b, h, idx, page: (b, 0, h))],
        out_specs=pl.BlockSpec((1, 1, t_real, 8, LANES), lambda b, h, idx, page: (b, h, 0, 0, 0)))
    out = pl.pallas_call(
        functools.partial(_slc_paged_body, pos0=pos0, blk0=blk0, t_real=t_real),
        out_shape=jax.ShapeDtypeStruct((B, NSA_KV_HEADS, t_real, 8, LANES), F32), grid_spec=gs,
        compiler_params=_cparams(("parallel", "arbitrary")), name="slc_paged")(
            idx_flat, page_flat, q_rot, *([cache_t] * n_slots), new_rows)
    return jnp.transpose(out[:, :, :, 0, :], (0, 2, 1, 3)).reshape(B, t_real, GW)


def _nsa_prep_body(q_ref, s_ref, w_ref, cos_ref, sin_ref, g_ref, ones_ref, qn_ref, qr_ref, so_ref, wo_ref):
    lane = lax.broadcasted_iota(jnp.int32, (1, GW), 1)
    first_half = (lane % HEAD_DIM) < (HEAD_DIM // 2)
    is_k = (lane // HEAD_DIM) % 2 == 0
    cos, sin = cos_ref[...], sin_ref[...]
    ones = ones_ref[...]

    def norm(x, g):
        ms = _mm(x * x, ones, 'l3') * (1.0 / HEAD_DIM)
        return x * lax.rsqrt(ms + NORM_EPS) * g

    def rope(x):
        swapped = jnp.where(first_half, pltpu.roll(x, GW - HEAD_DIM // 2, 1), pltpu.roll(x, HEAD_DIM // 2, 1))
        return x * cos + swapped * sin

    qn = norm(q_ref[0], g_ref[0:1, :])
    qn_ref[0] = qn
    qr_ref[0] = rope(qn)
    for x_ref, o_ref, gi in ((s_ref, so_ref, 1), (w_ref, wo_ref, 2)):
        x = x_ref[0]
        o_ref[0] = jnp.where(is_k, rope(norm(x, g_ref[gi:gi + 1, :])), x)


def _nsa_prep(P3, n_rows, tr, cos, sin, g_blk, ones_bd):
    B = P3.shape[0]
    col = lambda c0: pl.BlockSpec((1, tr, GW), lambda b, i: (b, i, c0 // GW))
    tab = pl.BlockSpec((tr, GW), lambda b, i: (i, 0))
    out = pl.BlockSpec((1, tr, GW), lambda b, i: (b, i, 0))
    shp = jax.ShapeDtypeStruct((B, n_rows, GW), F32)
    return pl.pallas_call(
        _nsa_prep_body, out_shape=(shp, shp, shp, shp), grid=(B, n_rows // tr),
        in_specs=[col(C_NQ), col(C_NSLC), col(C_NWIN), tab, tab, pl.BlockSpec((8, GW), lambda b, i: (0, 0)),
                  pl.BlockSpec((GW, GW), lambda b, i: (0, 0))],
        out_specs=(out, out, out, out),
        compiler_params=_cparams(("parallel", "parallel")), name="nsa_prep")(P3, P3, P3, cos, sin, g_blk, ones_bd)


def _rms(x, g):
    return x * lax.rsqrt(jnp.mean(x * x, -1, keepdims=True) + NORM_EPS) * g


def _rope(x, pos):
    half = HEAD_DIM // 2
    inv = ROPE_THETA ** (-jnp.arange(half, dtype=F32) / half)
    ang = pos.astype(F32)[:, None] * inv
    cos, sin = jnp.cos(ang)[:, None, :], jnp.sin(ang)[:, None, :]
    x1, x2 = x[..., :half], x[..., half:]
    return jnp.concatenate([x1 * cos - x2 * sin, x1 * sin + x2 * cos], -1)


def _pad_t(x, tp, value=0.0):
    t = x.shape[1]
    if t == tp:
        return x
    return jnp.pad(x, ((0, 0), (0, tp - t)) + ((0, 0),) * (x.ndim - 2), constant_values=value)


def _round_up(n, m):
    return -(-n // m) * m


def _cover_matrix(n_cmp, n_cmp_pad, n_slc, nbp):
    start = np.arange(n_cmp_pad)[:, None] * CMP_STRIDE
    blk = np.arange(nbp)[None, :]
    cov = (start < (blk + 1) * SLC_BLOCK) & (start + CMP_BLOCK > blk * SLC_BLOCK)
    cov &= (np.arange(n_cmp_pad)[:, None] < n_cmp) & (blk < n_slc)
    return jnp.asarray(cov.astype(np.float32), dtype=BF16)


def _mlstm_mixer(P, Pp, conv_buf, c0, n0, m0, prm):
    B, T, _ = P.shape
    keep = MLSTM_CONV - 1
    tail = jnp.pad(conv_buf, ((0, 0), (8 - keep, 0), (0, 0)))
    m0p = jnp.pad(m0, ((0, 0), (0, LANES - N_HEADS))).reshape(B, 1, LANES)
    out, C, n, m = _mlstm(Pp, T, tail[..., :GW], tail[..., GW:], prm['mlstm_blk'], c0, n0, m0p)
    qk_raw = jnp.concatenate([P[:, -keep:, C_MQ:C_MQ + GW], P[:, -keep:, C_MK:C_MK + GW]], -1)
    conv_new = jnp.concatenate([conv_buf, qk_raw], 1)[:, -keep:]
    return out[:, :T], C, n, m[:, 0, :N_HEADS], conv_new


def _rwkv_mixer(P, Pp, shift_buf, s0, prm, prec):
    B, T, _ = P.shape
    out, S = _rwkv(Pp, T, shift_buf, prm['rwkv_blk'], prm['rwkv_lora'], s0, prec)
    shift_new = jnp.concatenate([P[:, -1:, C_RR:C_RR + 3 * GW], P[:, -1:, C_RL:C_RL + LANES]], -1)
    return out[:, :T], S, shift_new


def _memory_kv(mem, prm):
    B = mem.shape[0]
    kv = _proj_in(mem.reshape(B * N_MEM, D_MODEL), prm['mem_norm_g'], prm['w_mem_kv_bf16'], 2 * GW)
    kv = kv.reshape(B, N_MEM, MEM_HEADS, 2, HEAD_DIM)
    return jnp.stack([_rms(kv[:, :, :, 0], prm['mem_qk_g'][1]), kv[:, :, :, 1]], 3)


def _memory_mixer(Pp, T, mem_kv, prm):
    B = Pp.shape[0]
    tp = _round_up(T, 8)
    tq = min(512, tp)
    o = _attn(Pp, mem_kv.reshape(B, N_MEM, 2 * GW), 'none', _MHA_PAIRS, 2 * LANES, 0, 0, tq, N_MEM,
              q_col0=C_CQ // LANES, q_gain=jnp.tile(prm['mem_qk_g'][0], 2), n_rows=tp)
    return o[:, :T]


def _rope_tables(pos):
    half = HEAD_DIM // 2
    inv = ROPE_THETA ** (-jnp.arange(half, dtype=F32) / half)
    ang = pos.astype(F32)[:, None] * inv
    cos, sin = jnp.cos(ang), jnp.sin(ang)
    return (jnp.tile(jnp.concatenate([cos, cos], -1), (1, N_HEADS)),
            jnp.tile(jnp.concatenate([-sin, sin], -1), (1, N_HEADS)))


def _nsa_mixer(P2d, Pp, B, T, pos0, win_prefix, prm, past):
    P = P2d.reshape(B, T, DP)
    g = prm['nsa_qk_g']
    tp = _round_up(T, 8)
    tq = min(256, tp)
    cos, sin = _rope_tables(pos0 + jnp.arange(tp, dtype=jnp.int32))
    qn, q_rot, slc_rows, win_rows = _nsa_prep(Pp, tp, min(512, tp), cos, sin, prm['nsa_prep_blk'], prm['head_ones'])
    kvrows = lambda t: t[:, :T].reshape(B, T, NSA_KV_HEADS, 2, HEAD_DIM)
    cmp_new = kvrows(P[..., C_NCMP:C_NCMP + 2 * KVW])
    slc_new, win_new = kvrows(slc_rows), kvrows(win_rows)

    L_all = pos0 + T
    n_sub = max(-(-L_all // CMP_STRIDE), CMP_BLOCK // CMP_STRIDE)
    n_cmp = n_sub - 1
    bd, w2bd, pe_hid = prm['cmp_bd'], prm['cmp_w2bd'], prm['cmp_pe_hid']
    if past is None:
        G = _subproj(P2d, C_NCMP // LANES, bd).reshape(B, T // CMP_STRIDE, 4 * LANES)
    else:
        cache_cmp_t, cache_slc_t, page_table, layer = past
        g_pages = _subproj_pages(cache_cmp_t, page_table, layer, bd)
        new_rows = _pad_t(cmp_new.reshape(B, T, 2 * KVW), CMP_STRIDE).reshape(B * CMP_STRIDE, 2 * KVW)
        g_new = _subproj(new_rows, 0, bd).reshape(B, 1, 4 * LANES)
        G = jnp.concatenate([g_pages, g_new], 1)
    n_cmp_pad = _round_up(n_cmp, LANES)
    gb = G[:, 1:, 256:]
    ga = G[:, :, :256]
    fit = lambda t: _pad_t(t, max(n_cmp_pad, t.shape[1]))[:, :n_cmp_pad]
    hid = fit(ga) + fit(gb) + pe_hid
    kv_cmp = _cmp_mlp(hid, w2bd, g[1])
    n_slc = -(-L_all // SLC_BLOCK)
    nbp = _round_up(n_slc, LANES)
    cover = _cover_matrix(n_cmp, n_cmp_pad, n_slc, nbp)
    tq = min(256, tp)
    o_cmp, imp = _cmp_attn(qn, kv_cmp, cover, pos0, min(512, tp))

    R = B * NSA_KV_HEADS * tp
    rp = _round_up(R, LANES)
    imp2d = jnp.pad(imp.reshape(R, nbp), ((0, rp - R), (0, 0)))
    sel, idx_t = _topk(imp2d, pos0, tp, n_slc, want_idx=past is not None)

    if past is None:
        sel4 = sel[:R].reshape(B, NSA_KV_HEADS, tp, nbp)
        o_slc = _attn(q_rot, slc_new.reshape(B, T, 2 * KVW), 'slc', _GQA_PAIRS, LANES, pos0, pos0, tq,
                      min(256, T), sel=sel4)
    else:
        bpp = PAGE_SIZE // SLC_BLOCK
        idx = idx_t[:, :R].T.reshape(B, NSA_KV_HEADS, tp, N_SELECT)[:, :, :T]
        idc = jnp.clip(idx, 0, page_table.shape[1] * bpp - 1)
        page = page_table[jnp.arange(B)[:, None, None, None], idc // bpp]
        new_rows = _pad_t(slc_new.reshape(B, T, 2 * KVW), SLC_BLOCK)
        o_slc = _slc_paged(q_rot, cache_slc_t, new_rows, idx.reshape(-1), page.reshape(-1).astype(jnp.int32),
                           layer, pos0, pos0 // SLC_BLOCK, T)

    if win_prefix.shape[1] == 0:
        win_ctx = win_new
        pos_k0 = pos0
    else:
        win_ctx = jnp.concatenate([win_prefix, win_new], 1)
        pos_k0 = pos0 - win_prefix.shape[1]
    n_k = win_ctx.shape[1]
    tkw = min(256, _round_up(n_k, LANES))
    kv_win = _pad_t(win_ctx.reshape(B, n_k, 2 * KVW), _round_up(n_k, tkw))
    o_win = _attn(q_rot, kv_win, 'win', _GQA_PAIRS, LANES, pos0, pos_k0, tq, tkw)

    keep = win_prefix.shape[1] if past is not None else min(WINDOW, T)
    flat = lambda o: o[:, :T].reshape(B * T, GW)
    return (flat(o_cmp), flat(o_slc), flat(o_win)), cmp_new, slc_new, win_ctx[:, -keep:]


RWKV_PREC = ('r3', 'bf16', 'bf16')


def _layer(x, pos0, st, mem_kv, prm, past, rwkv_prec=RWKV_PREC):
    conv_buf, c0, n0, m0, s0, shift_buf, win_prefix = st
    B, T, _ = x.shape
    x2d = x.reshape(B * T, D_MODEL)
    P2d = _proj_in(x2d, prm['norm_g'], prm['w_in_bf16'], 640)
    P = P2d.reshape(B, T, DP)
    Pp = _pad_t(P, _round_up(T, max(MLSTM_L, RWKV_L)))
    y_m, C, n, m, conv_new = _mlstm_mixer(P, Pp, conv_buf, c0, n0, m0, prm)
    (o_cmp, o_slc, o_win), cmp_new, slc_new, win_new = _nsa_mixer(P2d, Pp, B, T, pos0, win_prefix, prm, past)
    y_r, S, shift_new = _rwkv_mixer(P, Pp, shift_buf, s0, prm, rwkv_prec)
    o_mem = _memory_mixer(Pp, T, mem_kv, prm)
    flat = lambda t: t.reshape(B * T, GW)
    out = _proj_out(x2d, P2d, flat(y_m), flat(y_r), o_cmp, o_slc, o_win, flat(o_mem), prm['gate_expand'],
                    prm['w_out_bf16']).reshape(B, T, D_MODEL)
    return out, (cmp_new, slc_new, win_new, C, n, m, conv_new, S, shift_new)


def _prep_params(l, p):
    prm = {k: v[l] for k, v in p.items()}
    src = jnp.asarray(np.maximum(_SRC, 0), jnp.int32)
    keep = jnp.asarray((_SRC >= 0).astype(np.float32))
    prm['w_in_bf16'] = (jnp.take(prm['w_in'], src, axis=1) * keep).astype(BF16)
    prm['w_out_bf16'] = prm['w_out'].astype(BF16)
    wm = prm['w_mem_kv'].reshape(D_MODEL, 2, MEM_HEADS, HEAD_DIM)
    prm['w_mem_kv_bf16'] = jnp.transpose(wm, (0, 2, 1, 3)).reshape(D_MODEL, 2 * GW).astype(BF16)
    rows = lambda *vs: jnp.concatenate([jnp.pad(v.reshape(-1, v.shape[-1]), ((0, 0), (0, GW - v.shape[-1])))
                                        for v in vs], 0)
    pad16 = lambda blk: jnp.pad(blk, ((0, 16 - blk.shape[0]), (0, 0)))
    cw, cb = prm['mlstm_conv_w'], prm['mlstm_conv_b']
    prm['mlstm_blk'] = pad16(rows(cw[:, :GW], cw[:, GW:], cb[:GW], cb[GW:], prm['mlstm_norm_g'],
                                  prm['mlstm_gate_b'].reshape(1, 2 * N_HEADS)))
    mu = prm['rwkv_mu']
    prm['rwkv_blk'] = pad16(rows(mu[:GW], mu[GW:2 * GW], mu[2 * GW:3 * GW], mu[3 * GW:], prm['rwkv_w0'],
                                 prm['rwkv_a0'], prm['rwkv_kk'], prm['rwkv_ln']))
    g = prm['nsa_qk_g']
    one = jnp.ones((HEAD_DIM,), F32)
    prm['nsa_prep_blk'] = jnp.pad(rows(jnp.tile(g[0], N_HEADS), jnp.concatenate([g[2], one, g[2], one]),
                                       jnp.concatenate([g[3], one, g[3], one])), ((0, 5), (0, 0)))
    head_of = np.arange(GW) // HEAD_DIM
    prm['head_ones'] = jnp.asarray(head_of[:, None] == head_of[None, :], dtype=BF16)
    gate_lane = np.arange(LANES)[:, None]
    out_lane = np.arange(3 * GW)[None, :]
    prm['gate_expand'] = jnp.asarray(gate_lane == (out_lane // GW) * N_HEADS + (out_lane % GW) // HEAD_DIM, dtype=BF16)
    zl = jnp.zeros((DECAY_LORA, GW), F32)
    prm['rwkv_lora'] = jnp.concatenate([jnp.concatenate([prm['rwkv_w2'], zl], 1),
                                        jnp.concatenate([zl, prm['rwkv_a2']], 1)], 0)
    w1 = prm['nsa_cmp_w1']
    eye_h = jnp.eye(NSA_KV_HEADS, dtype=F32)
    eye_c = jnp.eye(2, dtype=F32)
    w1r = w1.reshape(2, 2, CMP_STRIDE, HEAD_DIM, HEAD_DIM)
    bd = jnp.einsum('cC,crsde->scdrCe', eye_c, w1r)
    prm['cmp_bd'] = bd.reshape(CMP_STRIDE // 2, 2 * LANES, 2 * LANES).astype(BF16)
    w2 = prm['nsa_cmp_w2']
    prm['cmp_w2bd'] = jnp.einsum('hH,cC,ced->hceHCd', eye_h, eye_c, w2).reshape(2 * KVW, 2 * KVW)
    pe_hid = jnp.einsum('csd,csde->ce', prm['nsa_pe'], w1, precision=HI)
    prm['cmp_pe_hid'] = jnp.tile(pe_hid.reshape(1, 2 * HEAD_DIM), (1, NSA_KV_HEADS)).reshape(2 * KVW)
    return prm


def kernel(x_prompt, x_sample, cache_cmp_kv, cache_slc_kv, cache_win_kv, cache_mem_kv, state_mlstm_C, state_mlstm_n, state_mlstm_m, state_mlstm_conv, state_rwkv_S, state_rwkv_shift, page_table, mem_prompt, norm_g, w_in, w_out, mlstm_conv_w, mlstm_conv_b, mlstm_gate_b, mlstm_norm_g, nsa_qk_g, nsa_pe, nsa_cmp_w1, nsa_cmp_w2, rwkv_mu, rwkv_w0, rwkv_w2, rwkv_a0, rwkv_a2, rwkv_kk, rwkv_ln, mem_norm_g, w_mem_kv, mem_qk_g):
    params = dict(norm_g=norm_g, w_in=w_in, w_out=w_out, mlstm_conv_w=mlstm_conv_w, mlstm_conv_b=mlstm_conv_b,
                  mlstm_gate_b=mlstm_gate_b, mlstm_norm_g=mlstm_norm_g, nsa_qk_g=nsa_qk_g, nsa_pe=nsa_pe,
                  nsa_cmp_w1=nsa_cmp_w1, nsa_cmp_w2=nsa_cmp_w2, rwkv_mu=rwkv_mu, rwkv_w0=rwkv_w0, rwkv_w2=rwkv_w2,
                  rwkv_a0=rwkv_a0, rwkv_a2=rwkv_a2, rwkv_kk=rwkv_kk, rwkv_ln=rwkv_ln, mem_norm_g=mem_norm_g,
                  w_mem_kv=w_mem_kv, mem_qk_g=mem_qk_g)
    depth = norm_g.shape[0]
    B = x_prompt.shape[0]
    past_len = page_table.shape[1] * PAGE_SIZE
    cache_cmp_t = jnp.transpose(cache_cmp_kv, (0, 1, 3, 4, 5, 2))
    cache_slc_t = jnp.transpose(cache_slc_kv, (0, 1, 3, 4, 5, 2))
    xp, xs = x_prompt, x_sample
    new_p, new_s, new_mem = [], [], []
    for l in range(depth):
        prm = _prep_params(l, params)
        mem_kv_p = _memory_kv(mem_prompt, prm)
        st_p = (jnp.zeros((B, MLSTM_CONV - 1, 2 * GW), F32),
                jnp.zeros((B, N_HEADS, HEAD_DIM, HEAD_DIM), F32),
                jnp.zeros((B, N_HEADS, HEAD_DIM), F32),
                jnp.full((B, N_HEADS), M_INIT, F32),
                jnp.zeros((B, N_HEADS, HEAD_DIM, HEAD_DIM), F32),
                jnp.zeros((B, 1, RWKV_SHIFT), F32),
                jnp.zeros((B, 0, NSA_KV_HEADS, 2, HEAD_DIM), F32))
        xp, sp = _layer(xp, 0, st_p, mem_kv_p, prm, None)
        st_s = (state_mlstm_conv[l], state_mlstm_C[l], state_mlstm_n[l], state_mlstm_m[l],
                state_rwkv_S[l], state_rwkv_shift[l], cache_win_kv[l])
        xs, ss = _layer(xs, past_len, st_s, cache_mem_kv[l], prm, (cache_cmp_t, cache_slc_t, page_table, l))
        new_p.append(sp)
        new_s.append(ss)
        new_mem.append(mem_kv_p)
    stack = lambda states, i: jnp.stack([s[i] for s in states])
    outs = [xp, xs]
    for i in range(3):
        outs += [stack(new_p, i), stack(new_s, i)]
    outs.append(jnp.stack(new_mem))
    for i in range(3, 9):
        outs += [stack(new_p, i), stack(new_s, i)]
    return tuple(outs)
```
